```python
import jax, jax.numpy as jnp
from jax import lax
import numpy as np

D_MODEL = 1024
BATCH = 16
SEQ = 2048
DEPTH = 1
DEC_BATCH = 1
DEC_SEQ = 16384
PAST_LEN = 128

GLA_HEADS = 4
GLA_DK = D_MODEL // 2 // GLA_HEADS
GLA_DV = D_MODEL // GLA_HEADS
GLA_QK_WIDTH = GLA_HEADS * GLA_DK
GLA_V_WIDTH = GLA_HEADS * GLA_DV
GLA_GATE_RANK = 16
GLA_GATE_NORM = 16.0
GLA_CHUNK = 64
CONV_WIDTH = D_MODEL
CONV_KSIZE = 3
N_BRANCH = 2
N_EXPERTS = 16
EC_CAPACITY_FACTOR = 2
D_EXPERT = 2 * D_MODEL
EPS = 1e-6

IN_SPLIT_SIZES = (GLA_QK_WIDTH, GLA_QK_WIDTH, GLA_V_WIDTH, GLA_V_WIDTH,
                  GLA_GATE_RANK, GLA_GATE_RANK,
                  CONV_WIDTH, CONV_WIDTH, CONV_WIDTH,
                  N_BRANCH * D_MODEL)
IN_COLS = sum(IN_SPLIT_SIZES)
IN_SPLIT_POINTS = tuple(np.cumsum(IN_SPLIT_SIZES)[:-1].tolist())

kernel_name = "gla_shortconv_ec_encoder"


def rmsnorm(x, g):
    xf = x.astype(jnp.float32)
    xf = xf * lax.rsqrt(jnp.mean(xf * xf, axis=-1, keepdims=True) + EPS)
    return xf.astype(x.dtype) * g


def gla_one_direction(q, k, v, log_a):
    Bn, T, H, dk = q.shape
    dv = v.shape[-1]
    L = GLA_CHUNK
    N = T // L
    q = q.astype(jnp.float32).reshape(Bn, N, L, H, dk)
    k = k.astype(jnp.float32).reshape(Bn, N, L, H, dk)
    v = v.astype(jnp.float32).reshape(Bn, N, L, H, dv)
    b = jnp.cumsum(log_a.astype(jnp.float32).reshape(Bn, N, L, H, dk), axis=2)
    b_ref = b[:, :, L // 2 - 1][:, :, None]
    b_last = b[:, :, -1]
    qe = q * jnp.exp(b - b_ref)
    ke = k * jnp.exp(b_ref - b)
    A = jnp.einsum('bnihk,bnjhk->bnhij', qe, ke)
    mask = jnp.tril(jnp.ones((L, L), dtype=bool))
    A = jnp.where(mask, A, 0.0)
    o_intra = jnp.einsum('bnhij,bnjhv->bnihv', A, v)
    q_in = q * jnp.exp(b)
    k_out = k * jnp.exp(b_last[:, :, None] - b)
    a_last = jnp.exp(b_last)

    def step(S, inp):
        qn, kn, vn, an = inp
        o = jnp.einsum('bihk,bhkv->bihv', qn, S)
        S = an[..., None] * S + jnp.einsum('bjhk,bjhv->bhkv', kn, vn)
        return S, o

    S0 = jnp.zeros((Bn, H, dk, dv), jnp.float32)
    xs = (jnp.moveaxis(q_in, 1, 0), jnp.moveaxis(k_out, 1, 0),
          jnp.moveaxis(v, 1, 0), jnp.moveaxis(a_last, 1, 0))
    _, o_inter = lax.scan(step, S0, xs)
    o = o_intra + jnp.moveaxis(o_inter, 0, 1)
    return o.reshape(Bn, T, H, dv)


def centred_depthwise_conv3(h, w):
    hp = jnp.pad(h, ((0, 0), (1, 1), (0, 0)))
    return w[0] * hp[:, :-2] + w[1] * hp[:, 1:-1] + w[2] * hp[:, 2:]


def expert_choice_ffn(h, w_router, w_gate, w_up, w_down):
    T, D = h.shape
    C = max(1, EC_CAPACITY_FACTOR * T // N_EXPERTS)
    aff = jax.nn.softmax((h @ w_router).astype(jnp.float32), axis=-1)
    gates, idx = lax.top_k(aff.T, C)
    xe = h[idx]
    hid = jax.nn.silu(jnp.einsum('ecd,edf->ecf', xe, w_gate)) * jnp.einsum('ecd,edf->ecf', xe, w_up)
    ye = jnp.einsum('ecf,efd->ecd', hid, w_down) * gates[..., None].astype(h.dtype)
    return jnp.zeros_like(h).at[idx.reshape(-1)].add(ye.reshape(-1, D))


def trunk(x, norm_mix_g, w_in, w_gk2_fwd, b_gk_fwd, w_gk2_bwd, b_gk_bwd, gla_norm_g,
          w_gla_out, conv_w, w_conv_out, b_merge, w_out, norm_ffn_g, w_router,
          w_exp_gate, w_exp_up, w_exp_down, norm_final_g):
    Bn, T, D = x.shape
    for l in range(DEPTH):
        h = rmsnorm(x, norm_mix_g[l])
        p = h @ w_in[l]
        q, k, v, og, lr_f, lr_b, cb, cc, cx, gm = jnp.split(p, IN_SPLIT_POINTS, axis=-1)
        la_f = jax.nn.log_sigmoid((lr_f @ w_gk2_fwd[l] + b_gk_fwd[l]).astype(jnp.float32)) / GLA_GATE_NORM
        la_b = jax.nn.log_sigmoid((lr_b @ w_gk2_bwd[l] + b_gk_bwd[l]).astype(jnp.float32)) / GLA_GATE_NORM
        qh = q.reshape(Bn, T, GLA_HEADS, GLA_DK) * (GLA_DK ** -0.5)
        kh = k.reshape(Bn, T, GLA_HEADS, GLA_DK)
        vh = v.reshape(Bn, T, GLA_HEADS, GLA_DV)
        la_f = la_f.reshape(Bn, T, GLA_HEADS, GLA_DK)
        la_b = la_b.reshape(Bn, T, GLA_HEADS, GLA_DK)
        o_fwd = gla_one_direction(qh, kh, vh, la_f)
        o_bwd = gla_one_direction(qh[:, ::-1], kh[:, ::-1], vh[:, ::-1], la_b[:, ::-1])[:, ::-1]
        o = o_fwd + o_bwd
        o = o * lax.rsqrt(jnp.mean(o * o, axis=-1, keepdims=True) + EPS)
        o = o.astype(x.dtype) * gla_norm_g[l]
        o = o.reshape(Bn, T, GLA_V_WIDTH) * jax.nn.silu(og)
        y_a = o @ w_gla_out[l]
        hc = centred_depthwise_conv3(cc * cx, conv_w[l])
        y_b = (cb * hc) @ w_conv_out[l]
        gates = jax.nn.sigmoid(gm + b_merge[l])
        g_a, g_b = jnp.split(gates, N_BRANCH, axis=-1)
        x = x + (g_a * y_a + g_b * y_b) @ w_out[l]
        h2 = rmsnorm(x, norm_ffn_g[l]).reshape(Bn * T, D)
        x = x + expert_choice_ffn(h2, w_router[l], w_exp_gate[l], w_exp_up[l], w_exp_down[l]).reshape(Bn, T, D)
    return rmsnorm(x, norm_final_g)


def setup_inputs(seed: int = 0) -> dict:
    key = jax.random.key(seed)
    ks = jax.random.split(key, 20)

    def nrm(k, shape, scale):
        return jax.random.normal(k, shape, jnp.float32) * scale

    return {
        "x_prompt": nrm(ks[0], (BATCH, SEQ, D_MODEL), 1.0),
        "x_sample": nrm(ks[1], (DEC_BATCH, DEC_SEQ, D_MODEL), 1.0),
        "norm_mix_g": 1.0 + nrm(ks[2], (DEPTH, D_MODEL), 0.02),
        "w_in": nrm(ks[3], (DEPTH, D_MODEL, IN_COLS), D_MODEL ** -0.5),
        "w_gk2_fwd": nrm(ks[4], (DEPTH, GLA_GATE_RANK, GLA_QK_WIDTH), GLA_GATE_RANK ** -0.5),
        "b_gk_fwd": 2.0 + nrm(ks[5], (DEPTH, GLA_QK_WIDTH), 0.5),
        "w_gk2_bwd": nrm(ks[6], (DEPTH, GLA_GATE_RANK, GLA_QK_WIDTH), GLA_GATE_RANK ** -0.5),
        "b_gk_bwd": 2.0 + nrm(ks[7], (DEPTH, GLA_QK_WIDTH), 0.5),
        "gla_norm_g": 1.0 + nrm(ks[8], (DEPTH, GLA_HEADS, GLA_DV), 0.02),
        "w_gla_out": nrm(ks[9], (DEPTH, GLA_V_WIDTH, D_MODEL), GLA_V_WIDTH ** -0.5),
        "conv_w": nrm(ks[10], (DEPTH, CONV_KSIZE, CONV_WIDTH), CONV_KSIZE ** -0.5),
        "w_conv_out": nrm(ks[11], (DEPTH, CONV_WIDTH, D_MODEL), CONV_WIDTH ** -0.5),
        "b_merge": nrm(ks[12], (DEPTH, N_BRANCH * D_MODEL), 0.1),
        "w_out": nrm(ks[13], (DEPTH, D_MODEL, D_MODEL), D_MODEL ** -0.5),
        "norm_ffn_g": 1.0 + nrm(ks[14], (DEPTH, D_MODEL), 0.02),
        "w_router": nrm(ks[15], (DEPTH, D_MODEL, N_EXPERTS), D_MODEL ** -0.5),
        "w_exp_gate": nrm(ks[16], (DEPTH, N_EXPERTS, D_MODEL, D_EXPERT), D_MODEL ** -0.5),
        "w_exp_up": nrm(ks[17], (DEPTH, N_EXPERTS, D_MODEL, D_EXPERT), D_MODEL ** -0.5),
        "w_exp_down": nrm(ks[18], (DEPTH, N_EXPERTS, D_EXPERT, D_MODEL), D_EXPERT ** -0.5),
        "norm_final_g": 1.0 + nrm(ks[19], (D_MODEL,), 0.02),
    }


def reference(x_prompt, x_sample, norm_mix_g, w_in, w_gk2_fwd, b_gk_fwd, w_gk2_bwd, b_gk_bwd,
              gla_norm_g, w_gla_out, conv_w, w_conv_out, b_merge, w_out, norm_ffn_g, w_router,
              w_exp_gate, w_exp_up, w_exp_down, norm_final_g):
    y_prompt = trunk(x_prompt, norm_mix_g, w_in, w_gk2_fwd, b_gk_fwd, w_gk2_bwd, b_gk_bwd, gla_norm_g,
                     w_gla_out, conv_w, w_conv_out, b_merge, w_out, norm_ffn_g, w_router,
                     w_exp_gate, w_exp_up, w_exp_down, norm_final_g)
    y_sample = trunk(x_sample, norm_mix_g, w_in, w_gk2_fwd, b_gk_fwd, w_gk2_bwd, b_gk_bwd, gla_norm_g,
                     w_gla_out, conv_w, w_conv_out, b_merge, w_out, norm_ffn_g, w_router,
                     w_exp_gate, w_exp_up, w_exp_down, norm_final_g)
    return (y_prompt, y_sample)
```

```python
import functools

import jax
import jax.numpy as jnp
from jax import lax
from jax.experimental import pallas as pl
from jax.experimental.pallas import tpu as pltpu

D_MODEL = 1024
GLA_HEADS = 4
GLA_DK = 128
GLA_DV = 256
GLA_QK_WIDTH = GLA_HEADS * GLA_DK
GLA_V_WIDTH = GLA_HEADS * GLA_DV
GLA_GATE_RANK = 16
GLA_GATE_NORM = 16.0
GLA_CHUNK = 64
N_EXPERTS = 16
EC_CAPACITY_FACTOR = 2
D_EXPERT = 2 * D_MODEL
EPS = 1e-6

BF16 = jnp.bfloat16
F32 = jnp.float32

VMEM_LIMIT_BYTES = 56 * 1024 * 1024

_QKVOG = 2 * GLA_QK_WIDTH + 2 * GLA_V_WIDTH
_LR = 2 * GLA_GATE_RANK


def _dot(a, b):
    return jnp.dot(a, b, preferred_element_type=F32)


def _dot_nt(a, b):
    return lax.dot_general(a, b, (((1,), (1,)), ((), ())), preferred_element_type=F32)


def _dot_tn(a, b):
    return lax.dot_general(a, b, (((0,), (0,)), ((), ())), preferred_element_type=F32)


def _params(*sem):
    return pltpu.CompilerParams(dimension_semantics=sem, vmem_limit_bytes=VMEM_LIMIT_BYTES)


def _inproj_kernel(x_ref, g_ref, w_ref, wlr_ref, bm_ref, qkvog_ref, cb_ref, u_ref, gates_ref, lr_ref):
    x = x_ref[...]
    h = x * lax.rsqrt(jnp.mean(x * x, axis=-1, keepdims=True) + EPS) * g_ref[...]
    hb = h.astype(BF16)
    lr_ref[...] = _dot(hb, wlr_ref[...])
    q = _dot(hb, w_ref[:, 0:GLA_QK_WIDTH]) * (GLA_DK ** -0.5)
    qkvog_ref[:, 0:GLA_QK_WIDTH] = q.astype(BF16)
    for c0 in range(GLA_QK_WIDTH, _QKVOG, 512):
        qkvog_ref[:, c0:c0 + 512] = _dot(hb, w_ref[:, c0:c0 + 512]).astype(BF16)
    o = _QKVOG
    for c0 in range(0, D_MODEL, 512):
        cb_ref[:, c0:c0 + 512] = _dot(hb, w_ref[:, o + c0:o + c0 + 512]).astype(BF16)
    for c0 in range(0, D_MODEL, 512):
        cc = _dot(hb, w_ref[:, o + D_MODEL + c0:o + D_MODEL + c0 + 512])
        cx = _dot(hb, w_ref[:, o + 2 * D_MODEL + c0:o + 2 * D_MODEL + c0 + 512])
        u_ref[:, c0:c0 + 512] = (cc * cx).astype(BF16)
    o = _QKVOG + 3 * D_MODEL
    for c0 in range(0, 2 * D_MODEL, 512):
        gm = _dot(hb, w_ref[:, o + c0:o + c0 + 512]) + bm_ref[:, c0:c0 + 512]
        gates_ref[:, c0:c0 + 512] = jax.nn.sigmoid(gm).astype(BF16)


def _inproj(x, g, w_main, w_lr, b_merge, tm):
    m = x.shape[0]
    ncols = w_main.shape[1]
    const = lambda i: (0, 0)
    row = lambda i: (i, 0)
    return pl.pallas_call(
        _inproj_kernel,
        grid=(m // tm,),
        in_specs=[
            pl.BlockSpec((tm, D_MODEL), row),
            pl.BlockSpec((1, D_MODEL), const),
            pl.BlockSpec((D_MODEL, ncols), const, pipeline_mode=pl.Buffered(1)),
            pl.BlockSpec((D_MODEL, _LR), const),
            pl.BlockSpec((1, 2 * D_MODEL), const),
        ],
        out_specs=[
            pl.BlockSpec((tm, _QKVOG), row),
            pl.BlockSpec((tm, D_MODEL), row),
            pl.BlockSpec((tm, D_MODEL), row),
            pl.BlockSpec((tm, 2 * D_MODEL), row),
            pl.BlockSpec((tm, _LR), row),
        ],
        out_shape=[
            jax.ShapeDtypeStruct((m, _QKVOG), BF16),
            jax.ShapeDtypeStruct((m, D_MODEL), BF16),
            jax.ShapeDtypeStruct((m, D_MODEL), BF16),
            jax.ShapeDtypeStruct((m, 2 * D_MODEL), BF16),
            jax.ShapeDtypeStruct((m, _LR), F32),
        ],
        compiler_params=_params("parallel"),
        name="inproj",
    )(x, g, w_main, w_lr, b_merge)


def _gla_body(rev, tb, lr_ref, w2_ref, b2_ref, q_ref, k_ref, v_ref, st_ref):
    L = GLA_CHUNK
    nc = tb // L
    g = lax.dot_general(lr_ref[...], w2_ref[0], (((1,), (0,)), ((), ())),
                        precision=lax.Precision.HIGHEST, preferred_element_type=F32) + b2_ref[0]
    la = (jnp.minimum(g, 0.0) - jnp.log(1.0 + jnp.exp(-jnp.abs(g)))) * (1.0 / GLA_GATE_NORM)
    r = lax.broadcasted_iota(jnp.int32, (tb, GLA_DK), 0) % L
    b = la
    s = 1
    while s < L:
        if rev:
            b = b + jnp.where(r < L - s, pltpu.roll(b, tb - s, 0), 0.0)
        else:
            b = b + jnp.where(r >= s, pltpu.roll(b, s, 0), 0.0)
        s *= 2
    ref_row = L // 2 if rev else L // 2 - 1
    end_row = 0 if rev else L - 1
    b_end_rows = [b[c * L + end_row:c * L + end_row + 1, :] for c in range(nc)]
    b_ref = jnp.concatenate(
        [jnp.broadcast_to(b[c * L + ref_row:c * L + ref_row + 1, :], (L, GLA_DK)) for c in range(nc)], axis=0)
    b_end = jnp.concatenate([jnp.broadcast_to(x, (L, GLA_DK)) for x in b_end_rows], axis=0)

    q = q_ref[...].astype(F32)
    k = k_ref[...].astype(F32)
    v = v_ref[...]
    qe = (q * jnp.exp(b - b_ref)).astype(BF16)
    ke = (k * jnp.exp(b_ref - b)).astype(BF16)
    a = _dot_nt(qe, ke)
    ri = lax.broadcasted_iota(jnp.int32, (tb, tb), 0)
    ci = lax.broadcasted_iota(jnp.int32, (tb, tb), 1)
    same_chunk = (ri // L) == (ci // L)
    keep = same_chunk & ((ci >= ri) if rev else (ci <= ri))
    a = jnp.where(keep, a, 0.0).astype(BF16)
    o_intra = _dot(a, v)

    q_in = (q * jnp.exp(b)).astype(BF16)
    k_out = (k * jnp.exp(b_end - b)).astype(BF16)
    st = st_ref[...]
    o_inter = [None] * nc
    order = range(nc - 1, -1, -1) if rev else range(nc)
    for c in order:
        rows = slice(c * L, (c + 1) * L)
        o_inter[c] = _dot_nt(q_in[rows], st.astype(BF16))
        st = st * jnp.exp(b_end_rows[c]) + _dot_tn(v[rows], k_out[rows])
    st_ref[...] = st
    return o_intra + jnp.concatenate(o_inter, axis=0)


def _gla_fwd_kernel(tb, lr_ref, w2_ref, b2_ref, q_ref, k_ref, v_ref, o_ref, st_ref):
    @pl.when(pl.program_id(2) == 0)
    def _():
        st_ref[...] = jnp.zeros_like(st_ref)
    o_ref[...] = _gla_body(False, tb, lr_ref, w2_ref, b2_ref, q_ref, k_ref, v_ref, st_ref)


def _gla_bwd_kernel(tb, lr_ref, w2_ref, b2_ref, q_ref, k_ref, v_ref, of_ref, og_ref, gn_ref, o_ref, st_ref):
    @pl.when(pl.program_id(2) == 0)
    def _():
        st_ref[...] = jnp.zeros_like(st_ref)
    o = of_ref[...] + _gla_body(True, tb, lr_ref, w2_ref, b2_ref, q_ref, k_ref, v_ref, st_ref)
    o = o * lax.rsqrt(jnp.mean(o * o, axis=-1, keepdims=True) + EPS) * gn_ref[0]
    og = og_ref[...].astype(F32)
    o_ref[...] = (o * (og * jax.nn.sigmoid(og))).astype(BF16)


def _gla(qkvog, lr, w2f, b2f, w2b, b2b, gn, n_seq, seq_len, tb):
    m = qkvog.shape[0]
    nb = seq_len // tb
    grid = (n_seq, GLA_HEADS, nb)
    kq = GLA_QK_WIDTH // GLA_DK
    kv = 2 * GLA_QK_WIDTH // GLA_DV
    kg = kv + GLA_V_WIDTH // GLA_DV

    def specs(rowf):
        return [
            pl.BlockSpec((tb, _LR), lambda b, h, n: (rowf(b, n), 0)),
            pl.BlockSpec((1, _LR, GLA_DK), lambda b, h, n: (h, 0, 0)),
            pl.BlockSpec((1, 1, GLA_DK), lambda b, h, n: (h, 0, 0)),
            pl.BlockSpec((tb, GLA_DK), lambda b, h, n: (rowf(b, n), h)),
            pl.BlockSpec((tb, GLA_DK), lambda b, h, n: (rowf(b, n), kq + h)),
            pl.BlockSpec((tb, GLA_DV), lambda b, h, n: (rowf(b, n), kv + h)),
        ]

    fwd_row = lambda b, n: b * nb + n
    bwd_row = lambda b, n: b * nb + (nb - 1 - n)
    o_fwd = pl.pallas_call(
        functools.partial(_gla_fwd_kernel, tb),
        grid=grid,
        in_specs=specs(fwd_row),
        out_specs=pl.BlockSpec((tb, GLA_DV), lambda b, h, n: (fwd_row(b, n), h)),
        out_shape=jax.ShapeDtypeStruct((m, GLA_V_WIDTH), F32),
        scratch_shapes=[pltpu.VMEM((GLA_DV, GLA_DK), F32)],
        compiler_params=_params("parallel", "parallel", "arbitrary"),
        name="gla_fwd",
    )(lr, w2f, b2f, qkvog, qkvog, qkvog)
    return pl.pallas_call(
        functools.partial(_gla_bwd_kernel, tb),
        grid=grid,
        in_specs=specs(bwd_row) + [
            pl.BlockSpec((tb, GLA_DV), lambda b, h, n: (bwd_row(b, n), h)),
            pl.BlockSpec((tb, GLA_DV), lambda b, h, n: (bwd_row(b, n), kg + h)),
            pl.BlockSpec((1, 1, GLA_DV), lambda b, h, n: (h, 0, 0)),
        ],
        out_specs=pl.BlockSpec((tb, GLA_DV), lambda b, h, n: (bwd_row(b, n), h)),
        out_shape=jax.ShapeDtypeStruct((m, GLA_V_WIDTH), BF16),
        scratch_shapes=[pltpu.VMEM((GLA_DV, GLA_DK), F32)],
        compiler_params=_params("parallel", "parallel", "arbitrary"),
        name="gla_bwd",
    )(lr, w2b, b2b, qkvog, qkvog, qkvog, o_fwd, qkvog, gn)


_HALO = 16


def _merge_kernel(tm, seq_len, x_ref, og_ref, cb_ref, u_ref, up_ref, un_ref, gates_ref, cw_ref,
                  wa_ref, wb_ref, wo_ref, g2_ref, wr_ref, x1_ref, h2_ref, aff_ref):
    i = pl.program_id(0)
    u = u_ref[...].astype(F32)
    first = (i * tm) % seq_len == 0
    last = ((i + 1) * tm) % seq_len == 0
    prev_row = jnp.where(first, 0.0, up_ref[_HALO - 1:_HALO, :].astype(F32))
    next_row = jnp.where(last, 0.0, un_ref[0:1, :].astype(F32))
    r = lax.broadcasted_iota(jnp.int32, (tm, D_MODEL), 0)
    um1 = jnp.where(r == 0, prev_row, pltpu.roll(u, 1, 0))
    up1 = jnp.where(r == tm - 1, next_row, pltpu.roll(u, tm - 1, 0))
    hc = cw_ref[0:1, :] * um1 + cw_ref[1:2, :] * u + cw_ref[2:3, :] * up1
    y_b = _dot((cb_ref[...].astype(F32) * hc).astype(BF16), wb_ref[...])
    y_a = _dot(og_ref[...], wa_ref[...])
    mix = gates_ref[:, 0:D_MODEL].astype(F32) * y_a + gates_ref[:, D_MODEL:2 * D_MODEL].astype(F32) * y_b
    x1 = x_ref[...] + _dot(mix.astype(BF16), wo_ref[...])
    x1_ref[...] = x1
    h2 = (x1 * lax.rsqrt(jnp.mean(x1 * x1, axis=-1, keepdims=True) + EPS) * g2_ref[...]).astype(BF16)
    h2_ref[...] = h2
    logits = _dot(h2, wr_ref[...])
    e = jnp.exp(logits - jnp.max(logits, axis=-1, keepdims=True))
    aff_ref[...] = e / jnp.sum(e, axis=-1, keepdims=True)


def _merge(x, og, cb, u, gates, conv_w, wa, wb, wo, g2, wr, seq_len, tm):
    m = x.shape[0]
    const = lambda i: (0, 0)
    row = lambda i: (i, 0)
    hb = tm // _HALO
    nhb = m // _HALO
    sq = lambda: pl.BlockSpec((D_MODEL, D_MODEL), const)
    return pl.pallas_call(
        functools.partial(_merge_kernel, tm, seq_len),
        grid=(m // tm,),
        in_specs=[
            pl.BlockSpec((tm, D_MODEL), row),
            pl.BlockSpec((tm, D_MODEL), row),
            pl.BlockSpec((tm, D_MODEL), row),
            pl.BlockSpec((tm, D_MODEL), row),
            pl.BlockSpec((_HALO, D_MODEL), lambda i: (jnp.maximum(i * hb - 1, 0), 0)),
            pl.BlockSpec((_HALO, D_MODEL), lambda i: (jnp.minimum((i + 1) * hb, nhb - 1), 0)),
            pl.BlockSpec((tm, 2 * D_MODEL), row),
            pl.BlockSpec((3, D_MODEL), const),
            sq(), sq(), sq(),
            pl.BlockSpec((1, D_MODEL), const),
            pl.BlockSpec((D_MODEL, N_EXPERTS), const),
        ],
        out_specs=[
            pl.BlockSpec((tm, D_MODEL), row),
            pl.BlockSpec((tm, D_MODEL), row),
            pl.BlockSpec((tm, N_EXPERTS), row),
        ],
        out_shape=[
            jax.ShapeDtypeStruct((m, D_MODEL), F32),
            jax.ShapeDtypeStruct((m, D_MODEL), BF16),
            jax.ShapeDtypeStruct((m, N_EXPERTS), F32),
        ],
        compiler_params=_params("parallel"),
        name="merge",
    )(x, og, cb, u, u, u, gates, conv_w, wa, wb, wo, g2, wr)


def _ffn_kernel(x_ref, gate_ref, wg_ref, wu_ref, wd_ref, o_ref, hid_ref):
    x = x_ref[0]
    for f0 in range(0, D_EXPERT, 1024):
        g = _dot(x, wg_ref[0, :, f0:f0 + 1024])
        up = _dot(x, wu_ref[0, :, f0:f0 + 1024])
        hid_ref[:, f0:f0 + 1024] = (g * jax.nn.sigmoid(g) * up).astype(BF16)
    o_ref[0] = _dot(hid_ref[...], wd_ref[0]) * gate_ref[0]


def _ffn(xe, gates, wg, wu, wd, tc):
    e, c, _ = xe.shape
    return pl.pallas_call(
        _ffn_kernel,
        grid=(e, c // tc),
        in_specs=[
            pl.BlockSpec((1, tc, D_MODEL), lambda e, j: (e, j, 0)),
            pl.BlockSpec((1, tc, 1), lambda e, j: (e, j, 0)),
            pl.BlockSpec((1, D_MODEL, D_EXPERT), lambda e, j: (e, 0, 0)),
            pl.BlockSpec((1, D_MODEL, D_EXPERT), lambda e, j: (e, 0, 0)),
            pl.BlockSpec((1, D_EXPERT, D_MODEL), lambda e, j: (e, 0, 0)),
        ],
        out_specs=pl.BlockSpec((1, tc, D_MODEL), lambda e, j: (e, j, 0)),
        out_shape=jax.ShapeDtypeStruct((e, c, D_MODEL), F32),
        scratch_shapes=[pltpu.VMEM((tc, D_EXPERT), BF16)],
        compiler_params=_params("parallel", "parallel"),
        name="experts",
    )(xe, gates, wg, wu, wd)


def _final_kernel(x_ref, y_ref, g_ref, o_ref):
    x = x_ref[...] + y_ref[...]
    o_ref[...] = x * lax.rsqrt(jnp.mean(x * x, axis=-1, keepdims=True) + EPS) * g_ref[...]


def _final(x, y, g, tm):
    m = x.shape[0]
    row = lambda i: (i, 0)
    return pl.pallas_call(
        _final_kernel,
        grid=(m // tm,),
        in_specs=[pl.BlockSpec((tm, D_MODEL), row), pl.BlockSpec((tm, D_MODEL), row),
                  pl.BlockSpec((1, D_MODEL), lambda i: (0, 0))],
        out_specs=pl.BlockSpec((tm, D_MODEL), row),
        out_shape=jax.ShapeDtypeStruct((m, D_MODEL), F32),
        compiler_params=_params("parallel"),
        name="final_norm",
    )(x, y, g)


def _pad_w2(w2, lo):
    w = w2.reshape(GLA_GATE_RANK, GLA_HEADS, GLA_DK).transpose(1, 0, 2)
    return jnp.pad(w, ((0, 0), (lo, _LR - GLA_GATE_RANK - lo), (0, 0)))


def _trunk(x3, wts):
    n_seq, seq_len, _ = x3.shape
    m = n_seq * seq_len
    x = x3.reshape(m, D_MODEL)
    qkvog, cb, u, gates, lr = _inproj(x, wts["norm_mix_g"], wts["w_main"], wts["w_lr"], wts["b_merge"], tm=256)
    og = _gla(qkvog, lr, wts["w2f"], wts["b2f"], wts["w2b"], wts["b2b"], wts["gla_norm_g"],
              n_seq, seq_len, tb=256)
    x1, h2, aff = _merge(x, og, cb, u, gates, wts["conv_w"], wts["w_gla_out"], wts["w_conv_out"], wts["w_out"],
                         wts["norm_ffn_g"], wts["w_router"], seq_len, tm=512)
    cap = max(1, EC_CAPACITY_FACTOR * m // N_EXPERTS)
    gate_vals, idx = lax.top_k(aff.T, cap)
    xe = jnp.take(h2, idx, axis=0)
    ye = _ffn(xe, gate_vals[..., None], wts["w_exp_gate"], wts["w_exp_up"], wts["w_exp_down"], tc=512)
    y = jnp.zeros((m, D_MODEL), F32).at[idx.reshape(-1)].add(ye.reshape(-1, D_MODEL))
    return _final(x1, y, wts["norm_final_g"], tm=512).reshape(n_seq, seq_len, D_MODEL)


def kernel(x_prompt, x_sample, norm_mix_g, w_in, w_gk2_fwd, b_gk_fwd, w_gk2_bwd, b_gk_bwd, gla_norm_g, w_gla_out,
           conv_w, w_conv_out, b_merge, w_out, norm_ffn_g, w_router, w_exp_gate, w_exp_up, w_exp_down, norm_final_g):
    w = w_in[0]
    lr0 = _QKVOG
    wts = {
        "norm_mix_g": norm_mix_g[0][None, :],
        "w_main": jnp.concatenate([w[:, :lr0], w[:, lr0 + _LR:]], axis=1).astype(BF16),
        "w_lr": w[:, lr0:lr0 + _LR].astype(BF16),
        "w2f": _pad_w2(w_gk2_fwd[0], 0),
        "w2b": _pad_w2(w_gk2_bwd[0], GLA_GATE_RANK),
        "b2f": b_gk_fwd[0].reshape(GLA_HEADS, 1, GLA_DK),
        "b2b": b_gk_bwd[0].reshape(GLA_HEADS, 1, GLA_DK),
        "gla_norm_g": gla_norm_g[0].reshape(GLA_HEADS, 1, GLA_DV),
        "w_gla_out": w_gla_out[0].astype(BF16),
        "conv_w": conv_w[0],
        "w_conv_out": w_conv_out[0].astype(BF16),
        "b_merge": b_merge[0][None, :],
        "w_out": w_out[0].astype(BF16),
        "norm_ffn_g": norm_ffn_g[0][None, :],
        "w_router": w_router[0].astype(BF16),
        "w_exp_gate": w_exp_gate[0].astype(BF16),
        "w_exp_up": w_exp_up[0].astype(BF16),
        "w_exp_down": w_exp_down[0].astype(BF16),
        "norm_final_g": norm_final_g[None, :],
    }
    return (_trunk(x_prompt, wts), _trunk(x_sample, wts))
```

```python
import functools

import jax
import jax.numpy as jnp
from jax import lax
from jax.experimental import pallas as pl
from jax.experimental.pallas import tpu as pltpu

D_MODEL = 1024
GLA_HEADS = 4
GLA_DK = 128
GLA_DV = 256
GLA_QK_WIDTH = GLA_HEADS * GLA_DK
GLA_V_WIDTH = GLA_HEADS * GLA_DV
GLA_GATE_RANK = 16
GLA_GATE_NORM = 16.0
GLA_CHUNK = 64
N_EXPERTS = 16
EC_CAPACITY_FACTOR = 2
D_EXPERT = 2 * D_MODEL
EPS = 1e-6

BF16 = jnp.bfloat16
F32 = jnp.float32

VMEM_LIMIT_BYTES = 56 * 1024 * 1024

_QKVOG = 2 * GLA_QK_WIDTH + 2 * GLA_V_WIDTH
_LR = 2 * GLA_GATE_RANK
_LR3 = 128


def _dot(a, b):
    return jnp.dot(a, b, preferred_element_type=F32)


def _dot_nt(a, b):
    return lax.dot_general(a, b, (((1,), (1,)), ((), ())), preferred_element_type=F32)


def _dot_tn(a, b):
    return lax.dot_general(a, b, (((0,), (0,)), ((), ())), preferred_element_type=F32)


def _params(*sem):
    return pltpu.CompilerParams(dimension_semantics=sem, vmem_limit_bytes=VMEM_LIMIT_BYTES)


def _inproj_kernel(x_ref, g_ref, w_ref, wlr_ref, bm_ref, qkvog_ref, cb_ref, u_ref, gates_ref, lr3_ref):
    x = x_ref[...]
    h = x * lax.rsqrt(jnp.mean(x * x, axis=-1, keepdims=True) + EPS) * g_ref[...]
    hb = h.astype(BF16)
    lr = _dot(hb, wlr_ref[...])
    hi = lr.astype(BF16)
    lo = (lr - hi.astype(F32)).astype(BF16)
    lane = lax.broadcasted_iota(jnp.int32, lr.shape, 1)
    lr3_ref[...] = jnp.where((lane >= _LR) & (lane < 2 * _LR), lo, hi)
    q = _dot(hb, w_ref[:, 0:GLA_QK_WIDTH]) * (GLA_DK ** -0.5)
    qkvog_ref[:, 0:GLA_QK_WIDTH] = q.astype(BF16)
    for c0 in range(GLA_QK_WIDTH, _QKVOG, 512):
        qkvog_ref[:, c0:c0 + 512] = _dot(hb, w_ref[:, c0:c0 + 512]).astype(BF16)
    o = _QKVOG
    for c0 in range(0, D_MODEL, 512):
        cb_ref[:, c0:c0 + 512] = _dot(hb, w_ref[:, o + c0:o + c0 + 512]).astype(BF16)
    for c0 in range(0, D_MODEL, 512):
        cc = _dot(hb, w_ref[:, o + D_MODEL + c0:o + D_MODEL + c0 + 512])
        cx = _dot(hb, w_ref[:, o + 2 * D_MODEL + c0:o + 2 * D_MODEL + c0 + 512])
        u_ref[:, c0:c0 + 512] = (cc * cx).astype(BF16)
    o = _QKVOG + 3 * D_MODEL
    for c0 in range(0, 2 * D_MODEL, 512):
        gm = _dot(hb, w_ref[:, o + c0:o + c0 + 512]) + bm_ref[:, c0:c0 + 512]
        gates_ref[:, c0:c0 + 512] = jax.nn.sigmoid(gm).astype(BF16)


def _inproj(x, g, w_main, w_lr3, b_merge, tm):
    m = x.shape[0]
    ncols = w_main.shape[1]
    const = lambda i: (0, 0)
    row = lambda i: (i, 0)
    return pl.pallas_call(
        _inproj_kernel,
        grid=(m // tm,),
        in_specs=[
            pl.BlockSpec((tm, D_MODEL), row),
            pl.BlockSpec((1, D_MODEL), const),
            pl.BlockSpec((D_MODEL, ncols), const, pipeline_mode=pl.Buffered(1)),
            pl.BlockSpec((D_MODEL, _LR3), const),
            pl.BlockSpec((1, 2 * D_MODEL), const),
        ],
        out_specs=[
            pl.BlockSpec((tm, _QKVOG), row),
            pl.BlockSpec((tm, D_MODEL), row),
            pl.BlockSpec((tm, D_MODEL), row),
            pl.BlockSpec((tm, 2 * D_MODEL), row),
            pl.BlockSpec((tm, _LR3), row),
        ],
        out_shape=[
            jax.ShapeDtypeStruct((m, _QKVOG), BF16),
            jax.ShapeDtypeStruct((m, D_MODEL), BF16),
            jax.ShapeDtypeStruct((m, D_MODEL), BF16),
            jax.ShapeDtypeStruct((m, 2 * D_MODEL), BF16),
            jax.ShapeDtypeStruct((m, _LR3), BF16),
        ],
        compiler_params=_params("parallel"),
        name="inproj",
    )(x, g, w_main, w_lr3, b_merge)


def _gla_body(rev, tb, lr3_ref, w2_ref, b2_ref, q_ref, k_ref, v_ref, st_ref):
    L = GLA_CHUNK
    nc = tb // L
    g = _dot(lr3_ref[...], w2_ref[...]) + b2_ref[...]
    la = (jnp.minimum(g, 0.0) - jnp.log(1.0 + jnp.exp(-jnp.abs(g)))) * (1.0 / GLA_GATE_NORM)
    r = lax.broadcasted_iota(jnp.int32, (tb, GLA_QK_WIDTH), 0) % L
    b = la
    s = 1
    while s < L:
        if rev:
            b = b + jnp.where(r < L - s, pltpu.roll(b, tb - s, 0), 0.0)
        else:
            b = b + jnp.where(r >= s, pltpu.roll(b, s, 0), 0.0)
        s *= 2
    ref_row = L // 2 if rev else L // 2 - 1
    end_row = 0 if rev else L - 1
    b_end_rows = [b[c * L + end_row:c * L + end_row + 1, :] for c in range(nc)]
    b_ref = jnp.concatenate(
        [jnp.broadcast_to(b[c * L + ref_row:c * L + ref_row + 1, :], (L, GLA_QK_WIDTH)) for c in range(nc)], axis=0)
    b_end = jnp.concatenate([jnp.broadcast_to(x, (L, GLA_QK_WIDTH)) for x in b_end_rows], axis=0)
    a_end_rows = [jnp.exp(x) for x in b_end_rows]

    q = q_ref[...].astype(F32)
    k = k_ref[...].astype(F32)
    qe = (q * jnp.exp(b - b_ref)).astype(BF16)
    ke = (k * jnp.exp(b_ref - b)).astype(BF16)
    q_in = (q * jnp.exp(b)).astype(BF16)
    k_out = (k * jnp.exp(b_end - b)).astype(BF16)
    ri = lax.broadcasted_iota(jnp.int32, (tb, tb), 0)
    ci = lax.broadcasted_iota(jnp.int32, (tb, tb), 1)
    keep = ((ri // L) == (ci // L)) & ((ci >= ri) if rev else (ci <= ri))
    order = range(nc - 1, -1, -1) if rev else range(nc)
    outs = []
    for h in range(GLA_HEADS):
        ks = slice(h * GLA_DK, (h + 1) * GLA_DK)
        v = v_ref[:, h * GLA_DV:(h + 1) * GLA_DV]
        a = jnp.where(keep, _dot_nt(qe[:, ks], ke[:, ks]), 0.0).astype(BF16)
        o_intra = _dot(a, v)
        st = st_ref[h]
        o_inter = [None] * nc
        for c in order:
            rows = slice(c * L, (c + 1) * L)
            o_inter[c] = _dot_nt(q_in[rows, ks], st.astype(BF16))
            st = st * a_end_rows[c][:, ks] + _dot_tn(v[rows], k_out[rows, ks])
        st_ref[h] = st
        outs.append(o_intra + jnp.concatenate(o_inter, axis=0))
    return outs


def _gla_fwd_kernel(tb, lr3_ref, w2_ref, b2_ref, q_ref, k_ref, v_ref, o_ref, st_ref):
    @pl.when(pl.program_id(1) == 0)
    def _():
        st_ref[...] = jnp.zeros_like(st_ref)
    outs = _gla_body(False, tb, lr3_ref, w2_ref, b2_ref, q_ref, k_ref, v_ref, st_ref)
    for h in range(GLA_HEADS):
        o_ref[:, h * GLA_DV:(h + 1) * GLA_DV] = outs[h]


def _gla_bwd_kernel(tb, lr3_ref, w2_ref, b2_ref, q_ref, k_ref, v_ref, of_ref, og_ref, gn_ref, o_ref, st_ref):
    @pl.when(pl.program_id(1) == 0)
    def _():
        st_ref[...] = jnp.zeros_like(st_ref)
    outs = _gla_body(True, tb, lr3_ref, w2_ref, b2_ref, q_ref, k_ref, v_ref, st_ref)
    for h in range(GLA_HEADS):
        vs = slice(h * GLA_DV, (h + 1) * GLA_DV)
        o = of_ref[:, vs] + outs[h]
        o = o * lax.rsqrt(jnp.mean(o * o, axis=-1, keepdims=True) + EPS) * gn_ref[:, vs]
        og = og_ref[:, vs].astype(F32)
        o_ref[:, vs] = (o * (og * jax.nn.sigmoid(og))).astype(BF16)


def _gla(qkvog, lr3, w2f, b2f, w2b, b2b, gn, n_seq, seq_len, tb):
    m = qkvog.shape[0]
    nb = seq_len // tb
    grid = (n_seq, nb)
    const = lambda b, n: (0, 0)

    def specs(rowf):
        return [
            pl.BlockSpec((tb, _LR3), lambda b, n: (rowf(b, n), 0)),
            pl.BlockSpec((_LR3, GLA_QK_WIDTH), const),
            pl.BlockSpec((1, GLA_QK_WIDTH), const),
            pl.BlockSpec((tb, GLA_QK_WIDTH), lambda b, n: (rowf(b, n), 0)),
            pl.BlockSpec((tb, GLA_QK_WIDTH), lambda b, n: (rowf(b, n), 1)),
            pl.BlockSpec((tb, GLA_V_WIDTH), lambda b, n: (rowf(b, n), 1)),
        ]

    fwd_row = lambda b, n: b * nb + n
    bwd_row = lambda b, n: b * nb + (nb - 1 - n)
    st = pltpu.VMEM((GLA_HEADS, GLA_DV, GLA_DK), F32)
    o_fwd = pl.pallas_call(
        functools.partial(_gla_fwd_kernel, tb),
        grid=grid,
        in_specs=specs(fwd_row),
        out_specs=pl.BlockSpec((tb, GLA_V_WIDTH), lambda b, n: (fwd_row(b, n), 0)),
        out_shape=jax.ShapeDtypeStruct((m, GLA_V_WIDTH), F32),
        scratch_shapes=[st],
        compiler_params=_params("parallel", "arbitrary"),
        name="gla_fwd",
    )(lr3, w2f, b2f, qkvog, qkvog, qkvog)
    return pl.pallas_call(
        functools.partial(_gla_bwd_kernel, tb),
        grid=grid,
        in_specs=specs(bwd_row) + [
            pl.BlockSpec((tb, GLA_V_WIDTH), lambda b, n: (bwd_row(b, n), 0)),
            pl.BlockSpec((tb, GLA_V_WIDTH), lambda b, n: (bwd_row(b, n), 2)),
            pl.BlockSpec((1, GLA_V_WIDTH), const),
        ],
        out_specs=pl.BlockSpec((tb, GLA_V_WIDTH), lambda b, n: (bwd_row(b, n), 0)),
        out_shape=jax.ShapeDtypeStruct((m, GLA_V_WIDTH), BF16),
        scratch_shapes=[st],
        compiler_params=_params("parallel", "arbitrary"),
        name="gla_bwd",
    )(lr3, w2b, b2b, qkvog, qkvog, qkvog, o_fwd, qkvog, gn)


_HALO = 16


def _merge_kernel(tm, seq_len, x_ref, og_ref, cb_ref, u_ref, up_ref, un_ref, gates_ref, cw_ref,
                  wa_ref, wb_ref, wo_ref, g2_ref, wr_ref, x1_ref, h2_ref, aff_ref):
    i = pl.program_id(0)
    u = u_ref[...].astype(F32)
    first = (i * tm) % seq_len == 0
    last = ((i + 1) * tm) % seq_len == 0
    prev_row = jnp.where(first, 0.0, up_ref[_HALO - 1:_HALO, :].astype(F32))
    next_row = jnp.where(last, 0.0, un_ref[0:1, :].astype(F32))
    r = lax.broadcasted_iota(jnp.int32, (tm, D_MODEL), 0)
    um1 = jnp.where(r == 0, prev_row, pltpu.roll(u, 1, 0))
    up1 = jnp.where(r == tm - 1, next_row, pltpu.roll(u, tm - 1, 0))
    hc = cw_ref[0:1, :] * um1 + cw_ref[1:2, :] * u + cw_ref[2:3, :] * up1
    y_b = _dot((cb_ref[...].astype(F32) * hc).astype(BF16), wb_ref[...])
    y_a = _dot(og_ref[...], wa_ref[...])
    mix = gates_ref[:, 0:D_MODEL].astype(F32) * y_a + gates_ref[:, D_MODEL:2 * D_MODEL].astype(F32) * y_b
    x1 = x_ref[...] + _dot(mix.astype(BF16), wo_ref[...])
    x1_ref[...] = x1
    h2 = (x1 * lax.rsqrt(jnp.mean(x1 * x1, axis=-1, keepdims=True) + EPS) * g2_ref[...]).astype(BF16)
    h2_ref[...] = h2
    logits = _dot(h2, wr_ref[...])
    e = jnp.exp(logits - jnp.max(logits, axis=-1, keepdims=True))
    aff_ref[...] = e / jnp.sum(e, axis=-1, keepdims=True)


def _merge(x, og, cb, u, gates, conv_w, wa, wb, wo, g2, wr, seq_len, tm):
    m = x.shape[0]
    const = lambda i: (0, 0)
    row = lambda i: (i, 0)
    hb = tm // _HALO
    nhb = m // _HALO
    sq = lambda: pl.BlockSpec((D_MODEL, D_MODEL), const)
    return pl.pallas_call(
        functools.partial(_merge_kernel, tm, seq_len),
        grid=(m // tm,),
        in_specs=[
            pl.BlockSpec((tm, D_MODEL), row),
            pl.BlockSpec((tm, D_MODEL), row),
            pl.BlockSpec((tm, D_MODEL), row),
            pl.BlockSpec((tm, D_MODEL), row),
            pl.BlockSpec((_HALO, D_MODEL), lambda i: (jnp.maximum(i * hb - 1, 0), 0)),
            pl.BlockSpec((_HALO, D_MODEL), lambda i: (jnp.minimum((i + 1) * hb, nhb - 1), 0)),
            pl.BlockSpec((tm, 2 * D_MODEL), row),
            pl.BlockSpec((3, D_MODEL), const),
            sq(), sq(), sq(),
            pl.BlockSpec((1, D_MODEL), const),
            pl.BlockSpec((D_MODEL, N_EXPERTS), const),
        ],
        out_specs=[
            pl.BlockSpec((tm, D_MODEL), row),
            pl.BlockSpec((tm, D_MODEL), row),
            pl.BlockSpec((tm, N_EXPERTS), row),
        ],
        out_shape=[
            jax.ShapeDtypeStruct((m, D_MODEL), F32),
            jax.ShapeDtypeStruct((m, D_MODEL), BF16),
            jax.ShapeDtypeStruct((m, N_EXPERTS), F32),
        ],
        compiler_params=_params("parallel"),
        name="merge",
    )(x, og, cb, u, u, u, gates, conv_w, wa, wb, wo, g2, wr)


def _ffn_kernel(x_ref, gate_ref, wg_ref, wu_ref, wd_ref, o_ref, hid_ref):
    x = x_ref[0]
    for f0 in range(0, D_EXPERT, 1024):
        g = _dot(x, wg_ref[0, :, f0:f0 + 1024])
        up = _dot(x, wu_ref[0, :, f0:f0 + 1024])
        hid_ref[:, f0:f0 + 1024] = (g * jax.nn.sigmoid(g) * up).astype(BF16)
    o_ref[0] = _dot(hid_ref[...], wd_ref[0]) * gate_ref[0]


def _ffn(xe, gates, wg, wu, wd, tc):
    e, c, _ = xe.shape
    return pl.pallas_call(
        _ffn_kernel,
        grid=(e, c // tc),
        in_specs=[
            pl.BlockSpec((1, tc, D_MODEL), lambda e, j: (e, j, 0)),
            pl.BlockSpec((1, tc, 1), lambda e, j: (e, j, 0)),
            pl.BlockSpec((1, D_MODEL, D_EXPERT), lambda e, j: (e, 0, 0)),
            pl.BlockSpec((1, D_MODEL, D_EXPERT), lambda e, j: (e, 0, 0)),
            pl.BlockSpec((1, D_EXPERT, D_MODEL), lambda e, j: (e, 0, 0)),
        ],
        out_specs=pl.BlockSpec((1, tc, D_MODEL), lambda e, j: (e, j, 0)),
        out_shape=jax.ShapeDtypeStruct((e, c, D_MODEL), F32),
        scratch_shapes=[pltpu.VMEM((tc, D_EXPERT), BF16)],
        compiler_params=_params("parallel", "parallel"),
        name="experts",
    )(xe, gates, wg, wu, wd)


def _final_kernel(x_ref, y_ref, g_ref, o_ref):
    x = x_ref[...] + y_ref[...]
    o_ref[...] = x * lax.rsqrt(jnp.mean(x * x, axis=-1, keepdims=True) + EPS) * g_ref[...]


def _final(x, y, g, tm):
    m = x.shape[0]
    row = lambda i: (i, 0)
    return pl.pallas_call(
        _final_kernel,
        grid=(m // tm,),
        in_specs=[pl.BlockSpec((tm, D_MODEL), row), pl.BlockSpec((tm, D_MODEL), row),
                  pl.BlockSpec((1, D_MODEL), lambda i: (0, 0))],
        out_specs=pl.BlockSpec((tm, D_MODEL), row),
        out_shape=jax.ShapeDtypeStruct((m, D_MODEL), F32),
        compiler_params=_params("parallel"),
        name="final_norm",
    )(x, y, g)


def _stack_w2(w2, lo):
    w = jnp.pad(w2, ((lo, _LR - GLA_GATE_RANK - lo), (0, 0)))
    hi = w.astype(BF16)
    lo_part = (w - hi.astype(F32)).astype(BF16)
    return jnp.concatenate([hi, hi, lo_part, jnp.zeros_like(hi)], axis=0)


def _trunk(x3, wts):
    n_seq, seq_len, _ = x3.shape
    m = n_seq * seq_len
    x = x3.reshape(m, D_MODEL)
    qkvog, cb, u, gates, lr3 = _inproj(x, wts["norm_mix_g"], wts["w_main"], wts["w_lr3"], wts["b_merge"], tm=256)
    og = _gla(qkvog, lr3, wts["w2f"], wts["b2f"], wts["w2b"], wts["b2b"], wts["gla_norm_g"],
              n_seq, seq_len, tb=256)
    x1, h2, aff = _merge(x, og, cb, u, gates, wts["conv_w"], wts["w_gla_out"], wts["w_conv_out"], wts["w_out"],
                         wts["norm_ffn_g"], wts["w_router"], seq_len, tm=512)
    cap = max(1, EC_CAPACITY_FACTOR * m // N_EXPERTS)
    gate_vals, idx = lax.top_k(aff.T, cap)
    xe = jnp.take(h2, idx, axis=0)
    ye = _ffn(xe, gate_vals[..., None], wts["w_exp_gate"], wts["w_exp_up"], wts["w_exp_down"], tc=512)
    y = jnp.zeros((m, D_MODEL), F32).at[idx.reshape(-1)].add(ye.reshape(-1, D_MODEL))
    return _final(x1, y, wts["norm_final_g"], tm=512).reshape(n_seq, seq_len, D_MODEL)


def kernel(x_prompt, x_sample, norm_mix_g, w_in, w_gk2_fwd, b_gk_fwd, w_gk2_bwd, b_gk_bwd, gla_norm_g, w_gla_out,
           conv_w, w_conv_out, b_merge, w_out, norm_ffn_g, w_router, w_exp_gate, w_exp_up, w_exp_down, norm_final_g):
    w = w_in[0]
    lr0 = _QKVOG
    w_lr = w[:, lr0:lr0 + _LR].astype(BF16)
    wts = {
        "norm_mix_g": norm_mix_g[0][None, :],
        "w_main": jnp.concatenate([w[:, :lr0], w[:, lr0 + _LR:]], axis=1).astype(BF16),
        "w_lr3": jnp.concatenate([w_lr, w_lr, w_lr, jnp.zeros_like(w_lr)], axis=1),
        "w2f": _stack_w2(w_gk2_fwd[0], 0),
        "w2b": _stack_w2(w_gk2_bwd[0], GLA_GATE_RANK),
        "b2f": b_gk_fwd[0][None, :],
        "b2b": b_gk_bwd[0][None, :],
        "gla_norm_g": gla_norm_g[0].reshape(1, GLA_V_WIDTH),
        "w_gla_out": w_gla_out[0].astype(BF16),
        "conv_w": conv_w[0],
        "w_conv_out": w_conv_out[0].astype(BF16),
        "b_merge": b_merge[0][None, :],
        "w_out": w_out[0].astype(BF16),
        "norm_ffn_g": norm_ffn_g[0][None, :],
        "w_router": w_router[0].astype(BF16),
        "w_exp_gate": w_exp_gate[0].astype(BF16),
        "w_exp_up": w_exp_up[0].astype(BF16),
        "w_exp_down": w_exp_down[0].astype(BF16),
        "norm_final_g": norm_final_g[None, :],
    }
    return (_trunk(x_prompt, wts), _trunk(x_sample, wts))
```

```python
import functools

import jax
import jax.numpy as jnp
from jax import lax
from jax.experimental import pallas as pl
from jax.experimental.pallas import tpu as pltpu

D_MODEL = 1024
GLA_HEADS = 4
GLA_DK = 128
GLA_DV = 256
GLA_QK_WIDTH = GLA_HEADS * GLA_DK
GLA_V_WIDTH = GLA_HEADS * GLA_DV
GLA_GATE_RANK = 16
GLA_GATE_NORM = 16.0
GLA_CHUNK = 64
N_EXPERTS = 16
EC_CAPACITY_FACTOR = 2
D_EXPERT = 2 * D_MODEL
EPS = 1e-6

BF16 = jnp.bfloat16
F32 = jnp.float32

VMEM_LIMIT_BYTES = 56 * 1024 * 1024

_QKVOG = 2 * GLA_QK_WIDTH + 2 * GLA_V_WIDTH
_LR = 2 * GLA_GATE_RANK
_LR3 = 128


def _dot(a, b):
    return jnp.dot(a, b, preferred_element_type=F32)


def _dot_nt(a, b):
    return lax.dot_general(a, b, (((1,), (1,)), ((), ())), preferred_element_type=F32)


def _dot_tn(a, b):
    return lax.dot_general(a, b, (((0,), (0,)), ((), ())), preferred_element_type=F32)


def _params(*sem):
    return pltpu.CompilerParams(dimension_semantics=sem, vmem_limit_bytes=VMEM_LIMIT_BYTES)


def _inproj_kernel(x_ref, g_ref, w_ref, wlr_ref, bm_ref, qkvog_ref, cb_ref, u_ref, gates_ref, lr3_ref):
    x = x_ref[...]
    h = x * lax.rsqrt(jnp.mean(x * x, axis=-1, keepdims=True) + EPS) * g_ref[...]
    hb = h.astype(BF16)
    lr = _dot(hb, wlr_ref[...])
    hi = lr.astype(BF16)
    lo = (lr - hi.astype(F32)).astype(BF16)
    lane = lax.broadcasted_iota(jnp.int32, lr.shape, 1)
    lr3_ref[...] = jnp.where((lane >= _LR) & (lane < 2 * _LR), lo, hi)
    q = _dot(hb, w_ref[:, 0:GLA_QK_WIDTH]) * (GLA_DK ** -0.5)
    qkvog_ref[:, 0:GLA_QK_WIDTH] = q.astype(BF16)
    for c0 in range(GLA_QK_WIDTH, _QKVOG, 512):
        qkvog_ref[:, c0:c0 + 512] = _dot(hb, w_ref[:, c0:c0 + 512]).astype(BF16)
    o = _QKVOG
    for c0 in range(0, D_MODEL, 512):
        cb_ref[:, c0:c0 + 512] = _dot(hb, w_ref[:, o + c0:o + c0 + 512]).astype(BF16)
    for c0 in range(0, D_MODEL, 512):
        cc = _dot(hb, w_ref[:, o + D_MODEL + c0:o + D_MODEL + c0 + 512])
        cx = _dot(hb, w_ref[:, o + 2 * D_MODEL + c0:o + 2 * D_MODEL + c0 + 512])
        u_ref[:, c0:c0 + 512] = (cc * cx).astype(BF16)
    o = _QKVOG + 3 * D_MODEL
    for c0 in range(0, 2 * D_MODEL, 512):
        gm = _dot(hb, w_ref[:, o + c0:o + c0 + 512]) + bm_ref[:, c0:c0 + 512]
        gates_ref[:, c0:c0 + 512] = jax.nn.sigmoid(gm).astype(BF16)


def _inproj(x, g, w_main, w_lr3, b_merge, tm):
    m = x.shape[0]
    ncols = w_main.shape[1]
    const = lambda i: (0, 0)
    row = lambda i: (i, 0)
    return pl.pallas_call(
        _inproj_kernel,
        grid=(m // tm,),
        in_specs=[
            pl.BlockSpec((tm, D_MODEL), row),
            pl.BlockSpec((1, D_MODEL), const),
            pl.BlockSpec((D_MODEL, ncols), const, pipeline_mode=pl.Buffered(1)),
            pl.BlockSpec((D_MODEL, _LR3), const),
            pl.BlockSpec((1, 2 * D_MODEL), const),
        ],
        out_specs=[
            pl.BlockSpec((tm, _QKVOG), row),
            pl.BlockSpec((tm, D_MODEL), row),
            pl.BlockSpec((tm, D_MODEL), row),
            pl.BlockSpec((tm, 2 * D_MODEL), row),
            pl.BlockSpec((tm, _LR3), row),
        ],
        out_shape=[
            jax.ShapeDtypeStruct((m, _QKVOG), BF16),
            jax.ShapeDtypeStruct((m, D_MODEL), BF16),
            jax.ShapeDtypeStruct((m, D_MODEL), BF16),
            jax.ShapeDtypeStruct((m, 2 * D_MODEL), BF16),
            jax.ShapeDtypeStruct((m, _LR3), BF16),
        ],
        compiler_params=_params("parallel"),
        name="inproj",
    )(x, g, w_main, w_lr3, b_merge)


def _gla_body(rev, tb, lr3_ref, w2_ref, b2_ref, q_ref, k_ref, v_ref, st_ref):
    L = GLA_CHUNK
    nc = tb // L
    g = _dot(lr3_ref[...], w2_ref[...]) + b2_ref[...]
    la = (jnp.minimum(g, 0.0) - jnp.log(1.0 + jnp.exp(-jnp.abs(g)))) * (1.0 / GLA_GATE_NORM)
    r = lax.broadcasted_iota(jnp.int32, (tb, GLA_QK_WIDTH), 0) % L
    b = la
    s = 1
    while s < L:
        if rev:
            b = b + jnp.where(r < L - s, pltpu.roll(b, tb - s, 0), 0.0)
        else:
            b = b + jnp.where(r >= s, pltpu.roll(b, s, 0), 0.0)
        s *= 2
    ref_row = L // 2 if rev else L // 2 - 1
    end_row = 0 if rev else L - 1
    b_end_rows = [b[c * L + end_row:c * L + end_row + 1, :] for c in range(nc)]
    b_ref = jnp.concatenate(
        [jnp.broadcast_to(b[c * L + ref_row:c * L + ref_row + 1, :], (L, GLA_QK_WIDTH)) for c in range(nc)], axis=0)
    b_end = jnp.concatenate([jnp.broadcast_to(x, (L, GLA_QK_WIDTH)) for x in b_end_rows], axis=0)
    a_end_rows = [jnp.exp(x) for x in b_end_rows]

    q = q_ref[...].astype(F32)
    k = k_ref[...].astype(F32)
    qe = (q * jnp.exp(b - b_ref)).astype(BF16)
    ke = (k * jnp.exp(b_ref - b)).astype(BF16)
    q_in = (q * jnp.exp(b)).astype(BF16)
    k_out = (k * jnp.exp(b_end - b)).astype(BF16)
    ri = lax.broadcasted_iota(jnp.int32, (tb, tb), 0)
    ci = lax.broadcasted_iota(jnp.int32, (tb, tb), 1)
    keep = ((ri // L) == (ci // L)) & ((ci >= ri) if rev else (ci <= ri))
    order = range(nc - 1, -1, -1) if rev else range(nc)
    outs = []
    for h in range(GLA_HEADS):
        ks = slice(h * GLA_DK, (h + 1) * GLA_DK)
        v = v_ref[:, h * GLA_DV:(h + 1) * GLA_DV]
        a = jnp.where(keep, _dot_nt(qe[:, ks], ke[:, ks]), 0.0).astype(BF16)
        o_intra = _dot(a, v)
        st = st_ref[h]
        o_inter = [None] * nc
        for c in order:
            rows = slice(c * L, (c + 1) * L)
            o_inter[c] = _dot_nt(q_in[rows, ks], st.astype(BF16))
            st = st * a_end_rows[c][:, ks] + _dot_tn(v[rows], k_out[rows, ks])
        st_ref[h] = st
        outs.append(o_intra + jnp.concatenate(o_inter, axis=0))
    return outs


def _gla_fwd_kernel(tb, lr3_ref, w2_ref, b2_ref, q_ref, k_ref, v_ref, o_ref, st_ref):
    @pl.when(pl.program_id(1) == 0)
    def _():
        st_ref[...] = jnp.zeros_like(st_ref)
    outs = _gla_body(False, tb, lr3_ref, w2_ref, b2_ref, q_ref, k_ref, v_ref, st_ref)
    for h in range(GLA_HEADS):
        o_ref[:, h * GLA_DV:(h + 1) * GLA_DV] = outs[h]


def _gla_bwd_kernel(tb, lr3_ref, w2_ref, b2_ref, q_ref, k_ref, v_ref, of_ref, og_ref, gn_ref, o_ref, st_ref):
    @pl.when(pl.program_id(1) == 0)
    def _():
        st_ref[...] = jnp.zeros_like(st_ref)
    outs = _gla_body(True, tb, lr3_ref, w2_ref, b2_ref, q_ref, k_ref, v_ref, st_ref)
    for h in range(GLA_HEADS):
        vs = slice(h * GLA_DV, (h + 1) * GLA_DV)
        o = of_ref[:, vs] + outs[h]
        o = o * lax.rsqrt(jnp.mean(o * o, axis=-1, keepdims=True) + EPS) * gn_ref[:, vs]
        og = og_ref[:, vs].astype(F32)
        o_ref[:, vs] = (o * (og * jax.nn.sigmoid(og))).astype(BF16)


def _gla(qkvog, lr3, w2f, b2f, w2b, b2b, gn, n_seq, seq_len, tb):
    m = qkvog.shape[0]
    nb = seq_len // tb
    grid = (n_seq, nb)
    const = lambda b, n: (0, 0)

    def specs(rowf):
        return [
            pl.BlockSpec((tb, _LR3), lambda b, n: (rowf(b, n), 0)),
            pl.BlockSpec((_LR3, GLA_QK_WIDTH), const),
            pl.BlockSpec((1, GLA_QK_WIDTH), const),
            pl.BlockSpec((tb, GLA_QK_WIDTH), lambda b, n: (rowf(b, n), 0)),
            pl.BlockSpec((tb, GLA_QK_WIDTH), lambda b, n: (rowf(b, n), 1)),
            pl.BlockSpec((tb, GLA_V_WIDTH), lambda b, n: (rowf(b, n), 1)),
        ]

    fwd_row = lambda b, n: b * nb + n
    bwd_row = lambda b, n: b * nb + (nb - 1 - n)
    st = pltpu.VMEM((GLA_HEADS, GLA_DV, GLA_DK), F32)
    o_fwd = pl.pallas_call(
        functools.partial(_gla_fwd_kernel, tb),
        grid=grid,
        in_specs=specs(fwd_row),
        out_specs=pl.BlockSpec((tb, GLA_V_WIDTH), lambda b, n: (fwd_row(b, n), 0)),
        out_shape=jax.ShapeDtypeStruct((m, GLA_V_WIDTH), F32),
        scratch_shapes=[st],
        compiler_params=_params("parallel", "arbitrary"),
        name="gla_fwd",
    )(lr3, w2f, b2f, qkvog, qkvog, qkvog)
    return pl.pallas_call(
        functools.partial(_gla_bwd_kernel, tb),
        grid=grid,
        in_specs=specs(bwd_row) + [
            pl.BlockSpec((tb, GLA_V_WIDTH), lambda b, n: (bwd_row(b, n), 0)),
            pl.BlockSpec((tb, GLA_V_WIDTH), lambda b, n: (bwd_row(b, n), 2)),
            pl.BlockSpec((1, GLA_V_WIDTH), const),
        ],
        out_specs=pl.BlockSpec((tb, GLA_V_WIDTH), lambda b, n: (bwd_row(b, n), 0)),
        out_shape=jax.ShapeDtypeStruct((m, GLA_V_WIDTH), BF16),
        scratch_shapes=[st],
        compiler_params=_params("parallel", "arbitrary"),
        name="gla_bwd",
    )(lr3, w2b, b2b, qkvog, qkvog, qkvog, o_fwd, qkvog, gn)


_HALO = 16


def _merge_kernel(tm, seq_len, x_ref, og_ref, cb_ref, u_ref, up_ref, un_ref, gates_ref, cw_ref,
                  wa_ref, wb_ref, wo_ref, g2_ref, wr_ref, x1_ref, h2_ref, aff_ref):
    i = pl.program_id(0)
    u = u_ref[...].astype(F32)
    first = (i * tm) % seq_len == 0
    last = ((i + 1) * tm) % seq_len == 0
    prev_row = jnp.where(first, 0.0, up_ref[_HALO - 1:_HALO, :].astype(F32))
    next_row = jnp.where(last, 0.0, un_ref[0:1, :].astype(F32))
    r = lax.broadcasted_iota(jnp.int32, (tm, D_MODEL), 0)
    um1 = jnp.where(r == 0, prev_row, pltpu.roll(u, 1, 0))
    up1 = jnp.where(r == tm - 1, next_row, pltpu.roll(u, tm - 1, 0))
    hc = cw_ref[0:1, :] * um1 + cw_ref[1:2, :] * u + cw_ref[2:3, :] * up1
    y_b = _dot((cb_ref[...].astype(F32) * hc).astype(BF16), wb_ref[...])
    y_a = _dot(og_ref[...], wa_ref[...])
    mix = gates_ref[:, 0:D_MODEL].astype(F32) * y_a + gates_ref[:, D_MODEL:2 * D_MODEL].astype(F32) * y_b
    x1 = x_ref[...] + _dot(mix.astype(BF16), wo_ref[...])
    x1_ref[...] = x1
    h2 = (x1 * lax.rsqrt(jnp.mean(x1 * x1, axis=-1, keepdims=True) + EPS) * g2_ref[...]).astype(BF16)
    h2_ref[...] = h2
    logits = _dot(h2, wr_ref[...])
    e = jnp.exp(logits - jnp.max(logits, axis=-1, keepdims=True))
    aff_ref[...] = e / jnp.sum(e, axis=-1, keepdims=True)


def _merge(x, og, cb, u, gates, conv_w, wa, wb, wo, g2, wr, seq_len, tm):
    m = x.shape[0]
    const = lambda i: (0, 0)
    row = lambda i: (i, 0)
    hb = tm // _HALO
    nhb = m // _HALO
    sq = lambda: pl.BlockSpec((D_MODEL, D_MODEL), const)
    return pl.pallas_call(
        functools.partial(_merge_kernel, tm, seq_len),
        grid=(m // tm,),
        in_specs=[
            pl.BlockSpec((tm, D_MODEL), row),
            pl.BlockSpec((tm, D_MODEL), row),
            pl.BlockSpec((tm, D_MODEL), row),
            pl.BlockSpec((tm, D_MODEL), row),
            pl.BlockSpec((_HALO, D_MODEL), lambda i: (jnp.maximum(i * hb - 1, 0), 0)),
            pl.BlockSpec((_HALO, D_MODEL), lambda i: (jnp.minimum((i + 1) * hb, nhb - 1), 0)),
            pl.BlockSpec((tm, 2 * D_MODEL), row),
            pl.BlockSpec((3, D_MODEL), const),
            sq(), sq(), sq(),
            pl.BlockSpec((1, D_MODEL), const),
            pl.BlockSpec((D_MODEL, N_EXPERTS), const),
        ],
        out_specs=[
            pl.BlockSpec((tm, D_MODEL), row),
            pl.BlockSpec((tm, D_MODEL), row),
            pl.BlockSpec((tm, N_EXPERTS), row),
        ],
        out_shape=[
            jax.ShapeDtypeStruct((m, D_MODEL), F32),
            jax.ShapeDtypeStruct((m, D_MODEL), BF16),
            jax.ShapeDtypeStruct((m, N_EXPERTS), F32),
        ],
        compiler_params=_params("parallel"),
        name="merge",
    )(x, og, cb, u, u, u, gates, conv_w, wa, wb, wo, g2, wr)


RB = 256
SUB = 64
WIN = SUB + 8
_XW = D_MODEL + 128
_IDX_BITS = 30


def _route_kernel(cap, n_tok_bits, aff_ref, thr_ref, jb_ref):
    bits = lax.bitcast_convert_type(aff_ref[...], jnp.int32)
    idx = lax.broadcasted_iota(jnp.int32, bits.shape, 1)
    capf = jnp.float32(cap)

    def count(mask):
        return jnp.sum(mask.astype(F32), axis=1, keepdims=True)

    def value_step(i, prefix):
        cand = prefix | jnp.left_shift(jnp.int32(1), _IDX_BITS - 1 - i)
        return jnp.where(count(bits >= cand) >= capf, cand, prefix)

    thr = lax.fori_loop(0, _IDX_BITS, value_step, jnp.zeros((N_EXPERTS, 1), jnp.int32))
    need = capf - count(bits > thr)
    tie = bits == thr

    def index_step(i, j):
        cand = j | jnp.left_shift(jnp.int32(1), n_tok_bits - 1 - i)
        return jnp.where(count(tie & (idx < cand)) < need, cand, j)

    thr_ref[...] = thr
    jb_ref[...] = lax.fori_loop(0, n_tok_bits, index_step, jnp.zeros((N_EXPERTS, 1), jnp.int32))


def _route(aff_t, cap):
    n_tok = aff_t.shape[1]
    out = jax.ShapeDtypeStruct((N_EXPERTS, 1), jnp.int32)
    return pl.pallas_call(
        functools.partial(_route_kernel, cap, max(1, (n_tok - 1).bit_length())),
        out_shape=[out, out],
        compiler_params=pltpu.CompilerParams(vmem_limit_bytes=VMEM_LIMIT_BYTES),
        name="route",
    )(aff_t)


def _selected(bits, tok, thr, jb):
    return (bits > thr) | ((bits == thr) & (tok <= jb))


def _offsets_kernel(to, aff_ref, thr_ref, jb_ref, offs_ref, carry_ref):
    i = pl.program_id(0)

    @pl.when(i == 0)
    def _():
        carry_ref[...] = jnp.zeros_like(carry_ref)

    bits = lax.bitcast_convert_type(aff_ref[...], jnp.int32)
    tok = i * to + lax.broadcasted_iota(jnp.int32, bits.shape, 0)
    sel = _selected(bits, tok, thr_ref[...], jb_ref[...]).astype(BF16)
    ns = to // SUB
    grp = (lax.broadcasted_iota(jnp.int32, (ns, to), 1) // SUB == lax.broadcasted_iota(jnp.int32, (ns, to), 0))
    cnt = _dot(grp.astype(BF16), sel)
    before = (lax.broadcasted_iota(jnp.int32, (ns, ns), 1) < lax.broadcasted_iota(jnp.int32, (ns, ns), 0))
    offs_ref[...] = (carry_ref[...] + _dot(before.astype(BF16), cnt.astype(BF16))).astype(jnp.int32)
    carry_ref[...] += jnp.sum(cnt, axis=0, keepdims=True)


def _offsets(aff, thr_row, jb_row, to):
    m = aff.shape[0]
    const = lambda i: (0, 0)
    return pl.pallas_call(
        functools.partial(_offsets_kernel, to),
        grid=(m // to,),
        in_specs=[pl.BlockSpec((to, N_EXPERTS), lambda i: (i, 0)),
                  pl.BlockSpec((1, N_EXPERTS), const), pl.BlockSpec((1, N_EXPERTS), const)],
        out_specs=pl.BlockSpec((to // SUB, N_EXPERTS), lambda i: (i, 0)),
        out_shape=jax.ShapeDtypeStruct((m // SUB, N_EXPERTS), jnp.int32),
        scratch_shapes=[pltpu.VMEM((1, N_EXPERTS), F32)],
        compiler_params=_params("arbitrary"),
        name="offsets",
    )(aff, thr_row, jb_row)


def _block_selection(blk, aff_t_ref, thr_ref, jb_ref):
    bits = lax.bitcast_convert_type(aff_t_ref[...], jnp.int32)
    tok = blk * RB + lax.broadcasted_iota(jnp.int32, bits.shape, 1)
    return _selected(bits, tok, thr_ref[...], jb_ref[...])


def _floor8(x):
    return pl.multiple_of(jnp.left_shift(jnp.right_shift(x, 3), 3), 8)


def _slot_onehot(sel, shift):
    self = sel.astype(F32)
    before = (lax.broadcasted_iota(jnp.int32, (RB, RB), 0) < lax.broadcasted_iota(jnp.int32, (RB, RB), 1))
    rank = _dot(self.astype(BF16), before.astype(BF16)).astype(jnp.int32)
    slot = lax.broadcasted_iota(jnp.int32, (WIN, RB), 0)
    rows = [jnp.where(rank[e:e + 1, :] + shift[e] == slot, self[e:e + 1, :], 0.0) for e in range(N_EXPERTS)]
    return jnp.concatenate(rows, axis=0).astype(BF16)


def _block_counts(offs_ref, blk):
    per = RB // SUB
    start = [offs_ref[blk * per * N_EXPERTS + e] for e in range(N_EXPERTS)]
    end = [offs_ref[(blk + 1) * per * N_EXPERTS + e] for e in range(N_EXPERTS)]
    most = functools.reduce(jnp.maximum, [b - a for a, b in zip(start, end)])
    return most <= SUB


def _sub_mask(sub):
    return lax.broadcasted_iota(jnp.int32, (N_EXPERTS, RB), 1) // SUB == sub


def _gather_kernel(cap, nb, offs_ref, aff_t_ref, aff_ref, thr_ref, jb_ref, h2_ref, xe_ref,
                   stage_ref, carry_ref, sem_ref):
    blk = pl.program_id(0)
    par = blk % 2
    per = RB // SUB

    def copy(parity, e, dst_row):
        return pltpu.make_async_copy(stage_ref.at[parity, pl.ds(e * WIN, WIN)],
                                     xe_ref.at[e, pl.ds(dst_row, WIN)], sem_ref.at[parity, e])

    def wait_all(parity):
        for e in range(N_EXPERTS):
            copy(parity, e, 0).wait()

    @pl.when(blk == 0)
    def _():
        carry_ref[...] = jnp.zeros_like(carry_ref)
        stage_ref[1] = jnp.zeros(stage_ref.shape[1:], F32)
        for e in range(N_EXPERTS):
            copy(1, e, cap).start()
        wait_all(1)

    sel = _block_selection(blk, aff_t_ref, thr_ref, jb_ref)
    aff = aff_ref[...]
    hi = aff.astype(BF16)
    r1 = aff - hi.astype(F32)
    mid = r1.astype(BF16)
    lo = (r1 - mid.astype(F32)).astype(BF16)
    er = lax.broadcasted_iota(jnp.int32, (N_EXPERTS, 128), 0)
    ec = lax.broadcasted_iota(jnp.int32, (N_EXPERTS, 128), 1)
    g3 = (_dot(hi, (ec == 3 * er).astype(BF16)) + _dot(mid, (ec == 3 * er + 1).astype(BF16))
          + _dot(lo, (ec == 3 * er + 2).astype(BF16)))
    src = jnp.concatenate([h2_ref[...], g3.astype(BF16)], axis=1)

    def emit(mask, seg_lo, seg_hi, first):
        lo_rows = [offs_ref[seg_lo * N_EXPERTS + e] for e in range(N_EXPERTS)]
        hi_rows = [offs_ref[seg_hi * N_EXPERTS + e] for e in range(N_EXPERTS)]
        base = [_floor8(r) for r in lo_rows]
        stage_ref[par] = _dot(_slot_onehot(mask, [r - b for r, b in zip(lo_rows, base)]), src)
        for e in range(N_EXPERTS):
            head = pl.ds(e * WIN, 8)
            stage_ref[par, head] = stage_ref[par, head] + carry_ref[e]
            nxt = _floor8(hi_rows[e]) - base[e]
            carry_ref[e] = stage_ref[par, pl.ds(pl.multiple_of(e * WIN + nxt, 8), 8)]
        if first:
            @pl.when(blk > 0)
            def _():
                wait_all(1 - par)
        for e in range(N_EXPERTS):
            copy(par, e, base[e]).start()

    fits = _block_counts(offs_ref, blk)

    @pl.when(fits)
    def _():
        emit(sel, blk * per, (blk + 1) * per, True)

    @pl.when(jnp.logical_not(fits))
    def _():
        for sub in range(per):
            if sub > 0:
                wait_all(par)
            emit(sel & _sub_mask(sub), blk * per + sub, blk * per + sub + 1, sub == 0)

    @pl.when(blk == nb - 1)
    def _():
        wait_all(par)


def _gather(offs, aff_t, aff, thr, jb, h2, cap):
    m = h2.shape[0]
    nb = m // RB
    col = lambda i, o: (0, 0)
    return pl.pallas_call(
        functools.partial(_gather_kernel, cap, nb),
        grid_spec=pltpu.PrefetchScalarGridSpec(
            num_scalar_prefetch=1,
            grid=(nb,),
            in_specs=[
                pl.BlockSpec((N_EXPERTS, RB), lambda i, o: (0, i)),
                pl.BlockSpec((RB, N_EXPERTS), lambda i, o: (i, 0)),
                pl.BlockSpec((N_EXPERTS, 1), col), pl.BlockSpec((N_EXPERTS, 1), col),
                pl.BlockSpec((RB, D_MODEL), lambda i, o: (i, 0)),
            ],
            out_specs=pl.BlockSpec(memory_space=pl.ANY),
            scratch_shapes=[pltpu.VMEM((2, N_EXPERTS * WIN, _XW), F32),
                            pltpu.VMEM((N_EXPERTS, 8, _XW), F32),
                            pltpu.SemaphoreType.DMA((2, N_EXPERTS))],
        ),
        out_shape=jax.ShapeDtypeStruct((N_EXPERTS, cap + WIN, _XW), F32),
        compiler_params=_params("arbitrary"),
        name="gather",
    )(offs, aff_t, aff, thr, jb, h2)


def _ffn_kernel(x_ref, wg_ref, wu_ref, wd_ref, o_ref, hid_ref):
    e = pl.program_id(0)
    x = x_ref[0, :, 0:D_MODEL].astype(BF16)
    gcols = x_ref[0, :, D_MODEL:_XW]
    lane = lax.broadcasted_iota(jnp.int32, gcols.shape, 1)
    gate = jnp.sum(jnp.where((lane >= 3 * e) & (lane < 3 * e + 3), gcols, 0.0), axis=-1, keepdims=True)
    for f0 in range(0, D_EXPERT, 1024):
        g = _dot(x, wg_ref[0, :, f0:f0 + 1024])
        up = _dot(x, wu_ref[0, :, f0:f0 + 1024])
        hid_ref[:, f0:f0 + 1024] = (g * jax.nn.sigmoid(g) * up).astype(BF16)
    o_ref[0] = _dot(hid_ref[...], wd_ref[0]) * gate


def _ffn(xe, cap, wg, wu, wd, tc):
    e = xe.shape[0]
    return pl.pallas_call(
        _ffn_kernel,
        grid=(e, cap // tc),
        in_specs=[
            pl.BlockSpec((1, tc, _XW), lambda e, j: (e, j, 0)),
            pl.BlockSpec((1, D_MODEL, D_EXPERT), lambda e, j: (e, 0, 0)),
            pl.BlockSpec((1, D_MODEL, D_EXPERT), lambda e, j: (e, 0, 0)),
            pl.BlockSpec((1, D_EXPERT, D_MODEL), lambda e, j: (e, 0, 0)),
        ],
        out_specs=pl.BlockSpec((1, tc, D_MODEL), lambda e, j: (e, j, 0)),
        out_shape=jax.ShapeDtypeStruct((e, cap, D_MODEL), F32),
        scratch_shapes=[pltpu.VMEM((tc, D_EXPERT), BF16)],
        compiler_params=_params("parallel", "parallel"),
        name="experts",
    )(xe, wg, wu, wd)


def _combine_kernel(cap, nb, offs_ref, aff_t_ref, thr_ref, jb_ref, x1_ref, g_ref, ye_ref, o_ref, win_ref, sem_ref):
    blk = pl.program_id(0)
    par = blk % 2
    per = RB // SUB

    def window(row):
        first = pl.multiple_of(jnp.minimum(_floor8(row), cap - WIN), 8)
        return first, row - first

    def copy(parity, e, first):
        return pltpu.make_async_copy(ye_ref.at[e, pl.ds(first, WIN)],
                                     win_ref.at[parity, pl.ds(e * WIN, WIN)], sem_ref.at[parity, e])

    def fetch(parity, rows):
        for e in range(N_EXPERTS):
            copy(parity, e, window(rows[e])[0]).start()

    def wait_all(parity):
        for e in range(N_EXPERTS):
            copy(parity, e, 0).wait()

    def block_rows(b, sub):
        return [offs_ref[(b * per + sub) * N_EXPERTS + e] for e in range(N_EXPERTS)]

    @pl.when(blk == 0)
    def _():
        fetch(par, block_rows(blk, 0))

    @pl.when(blk + 1 < nb)
    def _():
        fetch(1 - par, block_rows(blk + 1, 0))

    sel = _block_selection(blk, aff_t_ref, thr_ref, jb_ref)
    fits = _block_counts(offs_ref, blk)
    start = block_rows(blk, 0)

    def contribution(mask, rows):
        onehot = _slot_onehot(mask, [window(r)[1] for r in rows])
        return _dot_tn(onehot, win_ref[par].astype(BF16))

    def finish(y):
        x = x1_ref[...] + y
        o_ref[...] = x * lax.rsqrt(jnp.mean(x * x, axis=-1, keepdims=True) + EPS) * g_ref[...]

    @pl.when(fits)
    def _():
        wait_all(par)
        finish(contribution(sel, start))

    @pl.when(jnp.logical_not(fits))
    def _():
        wait_all(par)
        y = contribution(sel & _sub_mask(0), start)
        for sub in range(1, per):
            rows = block_rows(blk, sub)
            fetch(par, rows)
            wait_all(par)
            y = y + contribution(sel & _sub_mask(sub), rows)
        finish(y)


def _combine(offs, aff_t, thr, jb, x1, g, ye, cap):
    m = x1.shape[0]
    nb = m // RB
    col = lambda i, o: (0, 0)
    return pl.pallas_call(
        functools.partial(_combine_kernel, cap, nb),
        grid_spec=pltpu.PrefetchScalarGridSpec(
            num_scalar_prefetch=1,
            grid=(nb,),
            in_specs=[
                pl.BlockSpec((N_EXPERTS, RB), lambda i, o: (0, i)),
                pl.BlockSpec((N_EXPERTS, 1), col), pl.BlockSpec((N_EXPERTS, 1), col),
                pl.BlockSpec((RB, D_MODEL), lambda i, o: (i, 0)),
                pl.BlockSpec((1, D_MODEL), col),
                pl.BlockSpec(memory_space=pl.ANY),
            ],
            out_specs=pl.BlockSpec((RB, D_MODEL), lambda i, o: (i, 0)),
            scratch_shapes=[pltpu.VMEM((2, N_EXPERTS * WIN, D_MODEL), F32),
                            pltpu.SemaphoreType.DMA((2, N_EXPERTS))],
        ),
        out_shape=jax.ShapeDtypeStruct((m, D_MODEL), F32),
        compiler_params=_params("arbitrary"),
        name="combine",
    )(offs, aff_t, thr, jb, x1, g, ye)


def _stack_w2(w2, lo):
    w = jnp.pad(w2, ((lo, _LR - GLA_GATE_RANK - lo), (0, 0)))
    hi = w.astype(BF16)
    lo_part = (w - hi.astype(F32)).astype(BF16)
    return jnp.concatenate([hi, hi, lo_part, jnp.zeros_like(hi)], axis=0)


def _trunk(x3, wts):
    n_seq, seq_len, _ = x3.shape
    m = n_seq * seq_len
    x = x3.reshape(m, D_MODEL)
    qkvog, cb, u, gates, lr3 = _inproj(x, wts["norm_mix_g"], wts["w_main"], wts["w_lr3"], wts["b_merge"], tm=256)
    og = _gla(qkvog, lr3, wts["w2f"], wts["b2f"], wts["w2b"], wts["b2b"], wts["gla_norm_g"],
              n_seq, seq_len, tb=256)
    x1, h2, aff = _merge(x, og, cb, u, gates, wts["conv_w"], wts["w_gla_out"], wts["w_conv_out"], wts["w_out"],
                         wts["norm_ffn_g"], wts["w_router"], seq_len, tm=512)
    cap = max(1, EC_CAPACITY_FACTOR * m // N_EXPERTS)
    assert cap >= WIN and cap % 512 == 0 and m % 2048 == 0
    aff_t = aff.T
    thr, jb = _route(aff_t, cap)
    offs = _offsets(aff, thr.reshape(1, N_EXPERTS), jb.reshape(1, N_EXPERTS), to=2048)
    offs = jnp.concatenate([offs, jnp.full((1, N_EXPERTS), cap, jnp.int32)], axis=0).reshape(-1)
    xe = _gather(offs, aff_t, aff, thr, jb, h2, cap)
    ye = _ffn(xe, cap, wts["w_exp_gate"], wts["w_exp_up"], wts["w_exp_down"], tc=512)
    out = _combine(offs, aff_t, thr, jb, x1, wts["norm_final_g"], ye, cap)
    return out.reshape(n_seq, seq_len, D_MODEL)


def kernel(x_prompt, x_sample, norm_mix_g, w_in, w_gk2_fwd, b_gk_fwd, w_gk2_bwd, b_gk_bwd, gla_norm_g, w_gla_out,
           conv_w, w_conv_out, b_merge, w_out, norm_ffn_g, w_router, w_exp_gate, w_exp_up, w_exp_down, norm_final_g):
    w = w_in[0]
    lr0 = _QKVOG
    w_lr = w[:, lr0:lr0 + _LR].astype(BF16)
    wts = {
        "norm_mix_g": norm_mix_g[0][None, :],
        "w_main": jnp.concatenate([w[:, :lr0], w[:, lr0 + _LR:]], axis=1).astype(BF16),
        "w_lr3": jnp.concatenate([w_lr, w_lr, w_lr, jnp.zeros_like(w_lr)], axis=1),
        "w2f": _stack_w2(w_gk2_fwd[0], 0),
        "w2b": _stack_w2(w_gk2_bwd[0], GLA_GATE_RANK),
        "b2f": b_gk_fwd[0][None, :],
        "b2b": b_gk_bwd[0][None, :],
        "gla_norm_g": gla_norm_g[0].reshape(1, GLA_V_WIDTH),
        "w_gla_out": w_gla_out[0].astype(BF16),
        "conv_w": conv_w[0],
        "w_conv_out": w_conv_out[0].astype(BF16),
        "b_merge": b_merge[0][None, :],
        "w_out": w_out[0].astype(BF16),
        "norm_ffn_g": norm_ffn_g[0][None, :],
        "w_router": w_router[0].astype(BF16),
        "w_exp_gate": w_exp_gate[0].astype(BF16),
        "w_exp_up": w_exp_up[0].astype(BF16),
        "w_exp_down": w_exp_down[0].astype(BF16),
        "norm_final_g": norm_final_g[None, :],
    }
    return (_trunk(x_prompt, wts), _trunk(x_sample, wts))
```

```python
import functools

import jax
import jax.numpy as jnp
from jax import lax
from jax.experimental import pallas as pl
from jax.experimental.pallas import tpu as pltpu

D_MODEL = 1024
GLA_HEADS = 4
GLA_DK = 128
GLA_DV = 256
GLA_QK_WIDTH = GLA_HEADS * GLA_DK
GLA_V_WIDTH = GLA_HEADS * GLA_DV
GLA_GATE_RANK = 16
GLA_GATE_NORM = 16.0
GLA_CHUNK = 64
N_EXPERTS = 16
EC_CAPACITY_FACTOR = 2
D_EXPERT = 2 * D_MODEL
EPS = 1e-6

BF16 = jnp.bfloat16
F32 = jnp.float32

VMEM_LIMIT_BYTES = 56 * 1024 * 1024

_QKVOG = 2 * GLA_QK_WIDTH + 2 * GLA_V_WIDTH
_LR = 2 * GLA_GATE_RANK
_LR3 = 128


def _dot(a, b):
    return jnp.dot(a, b, preferred_element_type=F32)


def _dot_nt(a, b):
    return lax.dot_general(a, b, (((1,), (1,)), ((), ())), preferred_element_type=F32)


def _dot_tn(a, b):
    return lax.dot_general(a, b, (((0,), (0,)), ((), ())), preferred_element_type=F32)


def _params(*sem):
    return pltpu.CompilerParams(dimension_semantics=sem, vmem_limit_bytes=VMEM_LIMIT_BYTES)


def _inproj_kernel(x_ref, g_ref, w_ref, wlr_ref, bm_ref, qkvog_ref, cb_ref, u_ref, gates_ref, lr3_ref):
    x = x_ref[...]
    h = x * lax.rsqrt(jnp.mean(x * x, axis=-1, keepdims=True) + EPS) * g_ref[...]
    hb = h.astype(BF16)
    lr = _dot(hb, wlr_ref[...])
    hi = lr.astype(BF16)
    lo = (lr - hi.astype(F32)).astype(BF16)
    lane = lax.broadcasted_iota(jnp.int32, lr.shape, 1)
    lr3_ref[...] = jnp.where((lane >= _LR) & (lane < 2 * _LR), lo, hi)
    q = _dot(hb, w_ref[:, 0:GLA_QK_WIDTH]) * (GLA_DK ** -0.5)
    qkvog_ref[:, 0:GLA_QK_WIDTH] = q.astype(BF16)
    for c0 in range(GLA_QK_WIDTH, _QKVOG, 512):
        qkvog_ref[:, c0:c0 + 512] = _dot(hb, w_ref[:, c0:c0 + 512]).astype(BF16)
    o = _QKVOG
    for c0 in range(0, D_MODEL, 512):
        cb_ref[:, c0:c0 + 512] = _dot(hb, w_ref[:, o + c0:o + c0 + 512]).astype(BF16)
    for c0 in range(0, D_MODEL, 512):
        cc = _dot(hb, w_ref[:, o + D_MODEL + c0:o + D_MODEL + c0 + 512])
        cx = _dot(hb, w_ref[:, o + 2 * D_MODEL + c0:o + 2 * D_MODEL + c0 + 512])
        u_ref[:, c0:c0 + 512] = (cc * cx).astype(BF16)
    o = _QKVOG + 3 * D_MODEL
    for c0 in range(0, 2 * D_MODEL, 512):
        gm = _dot(hb, w_ref[:, o + c0:o + c0 + 512]) + bm_ref[:, c0:c0 + 512]
        gates_ref[:, c0:c0 + 512] = jax.nn.sigmoid(gm).astype(BF16)


def _inproj(x, g, w_main, w_lr3, b_merge, tm):
    m = x.shape[0]
    ncols = w_main.shape[1]
    const = lambda i: (0, 0)
    row = lambda i: (i, 0)
    return pl.pallas_call(
        _inproj_kernel,
        grid=(m // tm,),
        in_specs=[
            pl.BlockSpec((tm, D_MODEL), row),
            pl.BlockSpec((1, D_MODEL), const),
            pl.BlockSpec((D_MODEL, ncols), const, pipeline_mode=pl.Buffered(1)),
            pl.BlockSpec((D_MODEL, _LR3), const),
            pl.BlockSpec((1, 2 * D_MODEL), const),
        ],
        out_specs=[
            pl.BlockSpec((tm, _QKVOG), row),
            pl.BlockSpec((tm, D_MODEL), row),
            pl.BlockSpec((tm, D_MODEL), row),
            pl.BlockSpec((tm, 2 * D_MODEL), row),
            pl.BlockSpec((tm, _LR3), row),
        ],
        out_shape=[
            jax.ShapeDtypeStruct((m, _QKVOG), BF16),
            jax.ShapeDtypeStruct((m, D_MODEL), BF16),
            jax.ShapeDtypeStruct((m, D_MODEL), BF16),
            jax.ShapeDtypeStruct((m, 2 * D_MODEL), BF16),
            jax.ShapeDtypeStruct((m, _LR3), BF16),
        ],
        compiler_params=_params("parallel"),
        name="inproj",
    )(x, g, w_main, w_lr3, b_merge)


def _gla_body(rev, tb, lr3_ref, w2_ref, b2_ref, q_ref, k_ref, v_ref, st_ref):
    L = GLA_CHUNK
    nc = tb // L
    g = _dot(lr3_ref[...], w2_ref[...]) + b2_ref[...]
    la = (jnp.minimum(g, 0.0) - jnp.log(1.0 + jnp.exp(-jnp.abs(g)))) * (1.0 / GLA_GATE_NORM)
    r = lax.broadcasted_iota(jnp.int32, (tb, GLA_QK_WIDTH), 0) % L
    b = la
    s = 1
    while s < L:
        if rev:
            b = b + jnp.where(r < L - s, pltpu.roll(b, tb - s, 0), 0.0)
        else:
            b = b + jnp.where(r >= s, pltpu.roll(b, s, 0), 0.0)
        s *= 2
    ref_row = L // 2 if rev else L // 2 - 1
    end_row = 0 if rev else L - 1
    b_end_rows = [b[c * L + end_row:c * L + end_row + 1, :] for c in range(nc)]
    b_ref = jnp.concatenate(
        [jnp.broadcast_to(b[c * L + ref_row:c * L + ref_row + 1, :], (L, GLA_QK_WIDTH)) for c in range(nc)], axis=0)
    b_end = jnp.concatenate([jnp.broadcast_to(x, (L, GLA_QK_WIDTH)) for x in b_end_rows], axis=0)
    a_end_rows = [jnp.exp(x) for x in b_end_rows]

    q = q_ref[...].astype(F32)
    k = k_ref[...].astype(F32)
    qe = (q * jnp.exp(b - b_ref)).astype(BF16)
    ke = (k * jnp.exp(b_ref - b)).astype(BF16)
    q_in = (q * jnp.exp(b)).astype(BF16)
    k_out = (k * jnp.exp(b_end - b)).astype(BF16)
    ri = lax.broadcasted_iota(jnp.int32, (tb, tb), 0)
    ci = lax.broadcasted_iota(jnp.int32, (tb, tb), 1)
    keep = ((ri // L) == (ci // L)) & ((ci >= ri) if rev else (ci <= ri))
    order = range(nc - 1, -1, -1) if rev else range(nc)
    outs = []
    for h in range(GLA_HEADS):
        ks = slice(h * GLA_DK, (h + 1) * GLA_DK)
        v = v_ref[:, h * GLA_DV:(h + 1) * GLA_DV]
        a = jnp.where(keep, _dot_nt(qe[:, ks], ke[:, ks]), 0.0).astype(BF16)
        o_intra = _dot(a, v)
        st = st_ref[h]
        o_inter = [None] * nc
        for c in order:
            rows = slice(c * L, (c + 1) * L)
            o_inter[c] = _dot_nt(q_in[rows, ks], st.astype(BF16))
            st = st * a_end_rows[c][:, ks] + _dot_tn(v[rows], k_out[rows, ks])
        st_ref[h] = st
        outs.append(o_intra + jnp.concatenate(o_inter, axis=0))
    return outs


def _gla_fwd_kernel(tb, lr3_ref, w2_ref, b2_ref, q_ref, k_ref, v_ref, o_ref, st_ref):
    @pl.when(pl.program_id(1) == 0)
    def _():
        st_ref[...] = jnp.zeros_like(st_ref)
    outs = _gla_body(False, tb, lr3_ref, w2_ref, b2_ref, q_ref, k_ref, v_ref, st_ref)
    for h in range(GLA_HEADS):
        o_ref[:, h * GLA_DV:(h + 1) * GLA_DV] = outs[h]


def _gla_bwd_kernel(tb, lr3_ref, w2_ref, b2_ref, q_ref, k_ref, v_ref, of_ref, og_ref, gn_ref, o_ref, st_ref):
    @pl.when(pl.program_id(1) == 0)
    def _():
        st_ref[...] = jnp.zeros_like(st_ref)
    outs = _gla_body(True, tb, lr3_ref, w2_ref, b2_ref, q_ref, k_ref, v_ref, st_ref)
    for h in range(GLA_HEADS):
        vs = slice(h * GLA_DV, (h + 1) * GLA_DV)
        o = of_ref[:, vs] + outs[h]
        o = o * lax.rsqrt(jnp.mean(o * o, axis=-1, keepdims=True) + EPS) * gn_ref[:, vs]
        og = og_ref[:, vs].astype(F32)
        o_ref[:, vs] = (o * (og * jax.nn.sigmoid(og))).astype(BF16)


def _gla(qkvog, lr3, w2f, b2f, w2b, b2b, gn, n_seq, seq_len, tb):
    m = qkvog.shape[0]
    nb = seq_len // tb
    grid = (n_seq, nb)
    const = lambda b, n: (0, 0)

    def specs(rowf):
        return [
            pl.BlockSpec((tb, _LR3), lambda b, n: (rowf(b, n), 0)),
            pl.BlockSpec((_LR3, GLA_QK_WIDTH), const),
            pl.BlockSpec((1, GLA_QK_WIDTH), const),
            pl.BlockSpec((tb, GLA_QK_WIDTH), lambda b, n: (rowf(b, n), 0)),
            pl.BlockSpec((tb, GLA_QK_WIDTH), lambda b, n: (rowf(b, n), 1)),
            pl.BlockSpec((tb, GLA_V_WIDTH), lambda b, n: (rowf(b, n), 1)),
        ]

    fwd_row = lambda b, n: b * nb + n
    bwd_row = lambda b, n: b * nb + (nb - 1 - n)
    st = pltpu.VMEM((GLA_HEADS, GLA_DV, GLA_DK), F32)
    o_fwd = pl.pallas_call(
        functools.partial(_gla_fwd_kernel, tb),
        grid=grid,
        in_specs=specs(fwd_row),
        out_specs=pl.BlockSpec((tb, GLA_V_WIDTH), lambda b, n: (fwd_row(b, n), 0)),
        out_shape=jax.ShapeDtypeStruct((m, GLA_V_WIDTH), F32),
        scratch_shapes=[st],
        compiler_params=_params("parallel", "arbitrary"),
        name="gla_fwd",
    )(lr3, w2f, b2f, qkvog, qkvog, qkvog)
    return pl.pallas_call(
        functools.partial(_gla_bwd_kernel, tb),
        grid=grid,
        in_specs=specs(bwd_row) + [
            pl.BlockSpec((tb, GLA_V_WIDTH), lambda b, n: (bwd_row(b, n), 0)),
            pl.BlockSpec((tb, GLA_V_WIDTH), lambda b, n: (bwd_row(b, n), 2)),
            pl.BlockSpec((1, GLA_V_WIDTH), const),
        ],
        out_specs=pl.BlockSpec((tb, GLA_V_WIDTH), lambda b, n: (bwd_row(b, n), 0)),
        out_shape=jax.ShapeDtypeStruct((m, GLA_V_WIDTH), BF16),
        scratch_shapes=[st],
        compiler_params=_params("parallel", "arbitrary"),
        name="gla_bwd",
    )(lr3, w2b, b2b, qkvog, qkvog, qkvog, o_fwd, qkvog, gn)


_HALO = 16


def _merge_kernel(tm, seq_len, x_ref, og_ref, cb_ref, u_ref, up_ref, un_ref, gates_ref, cw_ref,
                  wa_ref, wb_ref, wo_ref, g2_ref, wr_ref, x1_ref, h2_ref, aff_ref):
    i = pl.program_id(0)
    u = u_ref[...].astype(F32)
    first = (i * tm) % seq_len == 0
    last = ((i + 1) * tm) % seq_len == 0
    prev_row = jnp.where(first, 0.0, up_ref[_HALO - 1:_HALO, :].astype(F32))
    next_row = jnp.where(last, 0.0, un_ref[0:1, :].astype(F32))
    r = lax.broadcasted_iota(jnp.int32, (tm, D_MODEL), 0)
    um1 = jnp.where(r == 0, prev_row, pltpu.roll(u, 1, 0))
    up1 = jnp.where(r == tm - 1, next_row, pltpu.roll(u, tm - 1, 0))
    hc = cw_ref[0:1, :] * um1 + cw_ref[1:2, :] * u + cw_ref[2:3, :] * up1
    y_b = _dot((cb_ref[...].astype(F32) * hc).astype(BF16), wb_ref[...])
    y_a = _dot(og_ref[...], wa_ref[...])
    mix = gates_ref[:, 0:D_MODEL].astype(F32) * y_a + gates_ref[:, D_MODEL:2 * D_MODEL].astype(F32) * y_b
    x1 = x_ref[...] + _dot(mix.astype(BF16), wo_ref[...])
    x1_ref[...] = x1
    h2 = (x1 * lax.rsqrt(jnp.mean(x1 * x1, axis=-1, keepdims=True) + EPS) * g2_ref[...]).astype(BF16)
    h2_ref[...] = h2
    logits = _dot(h2, wr_ref[...])
    e = jnp.exp(logits - jnp.max(logits, axis=-1, keepdims=True))
    aff_ref[...] = e / jnp.sum(e, axis=-1, keepdims=True)


def _merge(x, og, cb, u, gates, conv_w, wa, wb, wo, g2, wr, seq_len, tm):
    m = x.shape[0]
    const = lambda i: (0, 0)
    row = lambda i: (i, 0)
    hb = tm // _HALO
    nhb = m // _HALO
    sq = lambda: pl.BlockSpec((D_MODEL, D_MODEL), const)
    return pl.pallas_call(
        functools.partial(_merge_kernel, tm, seq_len),
        grid=(m // tm,),
        in_specs=[
            pl.BlockSpec((tm, D_MODEL), row),
            pl.BlockSpec((tm, D_MODEL), row),
            pl.BlockSpec((tm, D_MODEL), row),
            pl.BlockSpec((tm, D_MODEL), row),
            pl.BlockSpec((_HALO, D_MODEL), lambda i: (jnp.maximum(i * hb - 1, 0), 0)),
            pl.BlockSpec((_HALO, D_MODEL), lambda i: (jnp.minimum((i + 1) * hb, nhb - 1), 0)),
            pl.BlockSpec((tm, 2 * D_MODEL), row),
            pl.BlockSpec((3, D_MODEL), const),
            sq(), sq(), sq(),
            pl.BlockSpec((1, D_MODEL), const),
            pl.BlockSpec((D_MODEL, N_EXPERTS), const),
        ],
        out_specs=[
            pl.BlockSpec((tm, D_MODEL), row),
            pl.BlockSpec((tm, D_MODEL), row),
            pl.BlockSpec((tm, N_EXPERTS), row),
        ],
        out_shape=[
            jax.ShapeDtypeStruct((m, D_MODEL), F32),
            jax.ShapeDtypeStruct((m, D_MODEL), BF16),
            jax.ShapeDtypeStruct((m, N_EXPERTS), F32),
        ],
        compiler_params=_params("parallel"),
        name="merge",
    )(x, og, cb, u, u, u, gates, conv_w, wa, wb, wo, g2, wr)


RB = 256
SUB = 64
ROW_TILE = 16
WIN = SUB + ROW_TILE
_XW = D_MODEL + 128
_IDX_BITS = 30
_MIN_NORMAL_BITS = 0x00800000


def _route_kernel(cap, n_tok_bits, aff_ref, thr_ref, nxt_ref, jb_ref):
    aff = aff_ref[...]
    idx = lax.broadcasted_iota(jnp.int32, aff.shape, 1)
    capf = jnp.float32(cap)

    def count(mask):
        return jnp.sum(mask.astype(F32), axis=1, keepdims=True)

    def as_f32(bits):
        return lax.bitcast_convert_type(bits, F32)

    def value_step(i, prefix):
        cand = prefix | jnp.left_shift(jnp.int32(1), _IDX_BITS - 1 - i)
        return jnp.where(count(aff >= as_f32(cand)) >= capf, cand, prefix)

    thr_bits = lax.fori_loop(0, _IDX_BITS, value_step, jnp.zeros((N_EXPERTS, 1), jnp.int32))
    thr = as_f32(thr_bits)
    nxt = as_f32(jnp.maximum(thr_bits + 1, _MIN_NORMAL_BITS))
    need = capf - count(aff >= nxt)
    tie = (aff >= thr) & (aff < nxt)

    def index_step(i, j):
        cand = j | jnp.left_shift(jnp.int32(1), n_tok_bits - 1 - i)
        return jnp.where(count(tie & (idx < cand)) < need, cand, j)

    thr_ref[...] = thr
    nxt_ref[...] = nxt
    jb_ref[...] = lax.fori_loop(0, n_tok_bits, index_step, jnp.zeros((N_EXPERTS, 1), jnp.int32))


def _route(aff_t, cap):
    n_tok = aff_t.shape[1]
    val = jax.ShapeDtypeStruct((N_EXPERTS, 1), F32)
    return pl.pallas_call(
        functools.partial(_route_kernel, cap, max(1, (n_tok - 1).bit_length())),
        out_shape=[val, val, jax.ShapeDtypeStruct((N_EXPERTS, 1), jnp.int32)],
        compiler_params=pltpu.CompilerParams(vmem_limit_bytes=VMEM_LIMIT_BYTES),
        name="route",
    )(aff_t)


def _selected(aff, tok, thr, nxt, jb):
    return (aff >= nxt) | ((aff >= thr) & (tok <= jb))


def _offsets_kernel(to, aff_ref, thr_ref, nxt_ref, jb_ref, offs_ref, carry_ref):
    i = pl.program_id(0)

    @pl.when(i == 0)
    def _():
        carry_ref[...] = jnp.zeros_like(carry_ref)

    aff = aff_ref[...]
    tok = i * to + lax.broadcasted_iota(jnp.int32, aff.shape, 0)
    sel = _selected(aff, tok, thr_ref[...], nxt_ref[...], jb_ref[...]).astype(BF16)
    ns = to // SUB
    grp = (lax.broadcasted_iota(jnp.int32, (ns, to), 1) // SUB == lax.broadcasted_iota(jnp.int32, (ns, to), 0))
    cnt = _dot(grp.astype(BF16), sel)
    before = (lax.broadcasted_iota(jnp.int32, (ns, ns), 1) < lax.broadcasted_iota(jnp.int32, (ns, ns), 0))
    offs_ref[...] = (carry_ref[...] + _dot(before.astype(BF16), cnt.astype(BF16))).astype(jnp.int32)
    carry_ref[...] += jnp.sum(cnt, axis=0, keepdims=True)


def _offsets(aff, thr_row, nxt_row, jb_row, to):
    m = aff.shape[0]
    const = lambda i: (0, 0)
    row = lambda: pl.BlockSpec((1, N_EXPERTS), const)
    return pl.pallas_call(
        functools.partial(_offsets_kernel, to),
        grid=(m // to,),
        in_specs=[pl.BlockSpec((to, N_EXPERTS), lambda i: (i, 0)), row(), row(), row()],
        out_specs=pl.BlockSpec((to // SUB, N_EXPERTS), lambda i: (i, 0)),
        out_shape=jax.ShapeDtypeStruct((m // SUB, N_EXPERTS), jnp.int32),
        scratch_shapes=[pltpu.VMEM((1, N_EXPERTS), F32)],
        compiler_params=_params("arbitrary"),
        name="offsets",
    )(aff, thr_row, nxt_row, jb_row)


def _block_selection(blk, aff_t_ref, thr_ref, nxt_ref, jb_ref):
    aff = aff_t_ref[...]
    tok = blk * RB + lax.broadcasted_iota(jnp.int32, aff.shape, 1)
    return _selected(aff, tok, thr_ref[...], nxt_ref[...], jb_ref[...])


def _floor_tile(x):
    return pl.multiple_of((x // ROW_TILE) * ROW_TILE, ROW_TILE)


def _slot_onehot(sel, shift):
    self = sel.astype(F32)
    before = (lax.broadcasted_iota(jnp.int32, (RB, RB), 0) < lax.broadcasted_iota(jnp.int32, (RB, RB), 1))
    rank = _dot(self.astype(BF16), before.astype(BF16)).astype(jnp.int32)
    slot = lax.broadcasted_iota(jnp.int32, (WIN, RB), 0)
    rows = [jnp.where(rank[e:e + 1, :] + shift[e] == slot, self[e:e + 1, :], 0.0) for e in range(N_EXPERTS)]
    return jnp.concatenate(rows, axis=0).astype(BF16)


def _block_counts(offs_ref, blk):
    per = RB // SUB
    start = [offs_ref[blk * per * N_EXPERTS + e] for e in range(N_EXPERTS)]
    end = [offs_ref[(blk + 1) * per * N_EXPERTS + e] for e in range(N_EXPERTS)]
    most = functools.reduce(jnp.maximum, [b - a for a, b in zip(start, end)])
    return most <= SUB


def _sub_mask(sub):
    return lax.broadcasted_iota(jnp.int32, (N_EXPERTS, RB), 1) // SUB == sub


def _gather_kernel(cap, nb, offs_ref, aff_t_ref, aff_ref, thr_ref, nxt_ref, jb_ref, h2_ref, xe_ref,
                   stage_ref, carry_ref, sem_ref):
    blk = pl.program_id(0)
    par = blk % 2
    per = RB // SUB

    def copy(parity, e, dst_row):
        return pltpu.make_async_copy(stage_ref.at[parity, pl.ds(e * WIN, WIN)],
                                     xe_ref.at[e, pl.ds(dst_row, WIN)], sem_ref.at[parity, e])

    def wait_all(parity):
        for e in range(N_EXPERTS):
            copy(parity, e, 0).wait()

    @pl.when(blk == 0)
    def _():
        carry_ref[...] = jnp.zeros_like(carry_ref)
        stage_ref[1] = jnp.zeros(stage_ref.shape[1:], BF16)
        for e in range(N_EXPERTS):
            copy(1, e, cap).start()
        wait_all(1)

    sel = _block_selection(blk, aff_t_ref, thr_ref, nxt_ref, jb_ref)
    aff = aff_ref[...]
    hi = aff.astype(BF16)
    r1 = aff - hi.astype(F32)
    mid = r1.astype(BF16)
    lo = (r1 - mid.astype(F32)).astype(BF16)
    er = lax.broadcasted_iota(jnp.int32, (N_EXPERTS, 128), 0)
    ec = lax.broadcasted_iota(jnp.int32, (N_EXPERTS, 128), 1)
    g3 = (_dot(hi, (ec == 3 * er).astype(BF16)) + _dot(mid, (ec == 3 * er + 1).astype(BF16))
          + _dot(lo, (ec == 3 * er + 2).astype(BF16)))
    src = jnp.concatenate([h2_ref[...], g3.astype(BF16)], axis=1)

    def emit(mask, seg_lo, seg_hi, first):
        lo_rows = [offs_ref[seg_lo * N_EXPERTS + e] for e in range(N_EXPERTS)]
        hi_rows = [offs_ref[seg_hi * N_EXPERTS + e] for e in range(N_EXPERTS)]
        base = [_floor_tile(r) for r in lo_rows]
        stage_ref[par] = _dot(_slot_onehot(mask, [r - b for r, b in zip(lo_rows, base)]), src).astype(BF16)
        for e in range(N_EXPERTS):
            head = pl.ds(e * WIN, ROW_TILE)
            stage_ref[par, head] = stage_ref[par, head] + carry_ref[e]
            nxt = _floor_tile(hi_rows[e]) - base[e]
            carry_ref[e] = stage_ref[par, pl.ds(pl.multiple_of(e * WIN + nxt, ROW_TILE), ROW_TILE)]
        if first:
            @pl.when(blk > 0)
            def _():
                wait_all(1 - par)
        for e in range(N_EXPERTS):
            copy(par, e, base[e]).start()

    fits = _block_counts(offs_ref, blk)

    @pl.when(fits)
    def _():
        emit(sel, blk * per, (blk + 1) * per, True)

    @pl.when(jnp.logical_not(fits))
    def _():
        for sub in range(per):
            if sub > 0:
                wait_all(par)
            emit(sel & _sub_mask(sub), blk * per + sub, blk * per + sub + 1, sub == 0)

    @pl.when(blk == nb - 1)
    def _():
        wait_all(par)


def _gather(offs, aff_t, aff, thr, nxt, jb, h2, cap):
    m = h2.shape[0]
    nb = m // RB
    col = lambda i, o: (0, 0)
    return pl.pallas_call(
        functools.partial(_gather_kernel, cap, nb),
        grid_spec=pltpu.PrefetchScalarGridSpec(
            num_scalar_prefetch=1,
            grid=(nb,),
            in_specs=[
                pl.BlockSpec((N_EXPERTS, RB), lambda i, o: (0, i)),
                pl.BlockSpec((RB, N_EXPERTS), lambda i, o: (i, 0)),
                pl.BlockSpec((N_EXPERTS, 1), col), pl.BlockSpec((N_EXPERTS, 1), col),
                pl.BlockSpec((N_EXPERTS, 1), col),
                pl.BlockSpec((RB, D_MODEL), lambda i, o: (i, 0)),
            ],
            out_specs=pl.BlockSpec(memory_space=pl.ANY),
            scratch_shapes=[pltpu.VMEM((2, N_EXPERTS * WIN, _XW), BF16),
                            pltpu.VMEM((N_EXPERTS, ROW_TILE, _XW), BF16),
                            pltpu.SemaphoreType.DMA((2, N_EXPERTS))],
        ),
        out_shape=jax.ShapeDtypeStruct((N_EXPERTS, cap + WIN, _XW), BF16),
        compiler_params=_params("arbitrary"),
        name="gather",
    )(offs, aff_t, aff, thr, nxt, jb, h2)


def _ffn_kernel(x_ref, wg_ref, wu_ref, wd_ref, o_ref, hid_ref):
    e = pl.program_id(0)
    x = x_ref[0, :, 0:D_MODEL]
    gcols = x_ref[0, :, D_MODEL:_XW].astype(F32)
    lane = lax.broadcasted_iota(jnp.int32, gcols.shape, 1)
    gate = jnp.sum(jnp.where((lane >= 3 * e) & (lane < 3 * e + 3), gcols, 0.0), axis=-1, keepdims=True)
    for f0 in range(0, D_EXPERT, 1024):
        g = _dot(x, wg_ref[0, :, f0:f0 + 1024])
        up = _dot(x, wu_ref[0, :, f0:f0 + 1024])
        hid_ref[:, f0:f0 + 1024] = (g * jax.nn.sigmoid(g) * up).astype(BF16)
    o_ref[0] = (_dot(hid_ref[...], wd_ref[0]) * gate).astype(BF16)


def _ffn(xe, cap, wg, wu, wd, tc):
    e = xe.shape[0]
    return pl.pallas_call(
        _ffn_kernel,
        grid=(e, cap // tc),
        in_specs=[
            pl.BlockSpec((1, tc, _XW), lambda e, j: (e, j, 0)),
            pl.BlockSpec((1, D_MODEL, D_EXPERT), lambda e, j: (e, 0, 0)),
            pl.BlockSpec((1, D_MODEL, D_EXPERT), lambda e, j: (e, 0, 0)),
            pl.BlockSpec((1, D_EXPERT, D_MODEL), lambda e, j: (e, 0, 0)),
        ],
        out_specs=pl.BlockSpec((1, tc, D_MODEL), lambda e, j: (e, j, 0)),
        out_shape=jax.ShapeDtypeStruct((e, cap, D_MODEL), BF16),
        scratch_shapes=[pltpu.VMEM((tc, D_EXPERT), BF16)],
        compiler_params=_params("parallel", "parallel"),
        name="experts",
    )(xe, wg, wu, wd)


def _combine_kernel(cap, nb, offs_ref, aff_t_ref, thr_ref, nxt_ref, jb_ref, x1_ref, g_ref, ye_ref, o_ref,
                    win_ref, sem_ref):
    blk = pl.program_id(0)
    par = blk % 2
    per = RB // SUB

    def window(row):
        first = pl.multiple_of(jnp.minimum(_floor_tile(row), cap - WIN), ROW_TILE)
        return first, row - first

    def copy(parity, e, first):
        return pltpu.make_async_copy(ye_ref.at[e, pl.ds(first, WIN)],
                                     win_ref.at[parity, pl.ds(e * WIN, WIN)], sem_ref.at[parity, e])

    def fetch(parity, rows):
        for e in range(N_EXPERTS):
            copy(parity, e, window(rows[e])[0]).start()

    def wait_all(parity):
        for e in range(N_EXPERTS):
            copy(parity, e, 0).wait()

    def block_rows(b, sub):
        return [offs_ref[(b * per + sub) * N_EXPERTS + e] for e in range(N_EXPERTS)]

    @pl.when(blk == 0)
    def _():
        fetch(par, block_rows(blk, 0))

    @pl.when(blk + 1 < nb)
    def _():
        fetch(1 - par, block_rows(blk + 1, 0))

    sel = _block_selection(blk, aff_t_ref, thr_ref, nxt_ref, jb_ref)
    fits = _block_counts(offs_ref, blk)
    start = block_rows(blk, 0)

    def contribution(mask, rows):
        onehot = _slot_onehot(mask, [window(r)[1] for r in rows])
        return _dot_tn(onehot, win_ref[par])

    def finish(y):
        x = x1_ref[...] + y
        o_ref[...] = x * lax.rsqrt(jnp.mean(x * x, axis=-1, keepdims=True) + EPS) * g_ref[...]

    @pl.when(fits)
    def _():
        wait_all(par)
        finish(contribution(sel, start))

    @pl.when(jnp.logical_not(fits))
    def _():
        wait_all(par)
        y = contribution(sel & _sub_mask(0), start)
        for sub in range(1, per):
            rows = block_rows(blk, sub)
            fetch(par, rows)
            wait_all(par)
            y = y + contribution(sel & _sub_mask(sub), rows)
        finish(y)


def _combine(offs, aff_t, thr, nxt, jb, x1, g, ye, cap):
    m = x1.shape[0]
    nb = m // RB
    col = lambda i, o: (0, 0)
    return pl.pallas_call(
        functools.partial(_combine_kernel, cap, nb),
        grid_spec=pltpu.PrefetchScalarGridSpec(
            num_scalar_prefetch=1,
            grid=(nb,),
            in_specs=[
                pl.BlockSpec((N_EXPERTS, RB), lambda i, o: (0, i)),
                pl.BlockSpec((N_EXPERTS, 1), col), pl.BlockSpec((N_EXPERTS, 1), col),
                pl.BlockSpec((N_EXPERTS, 1), col),
                pl.BlockSpec((RB, D_MODEL), lambda i, o: (i, 0)),
                pl.BlockSpec((1, D_MODEL), col),
                pl.BlockSpec(memory_space=pl.ANY),
            ],
            out_specs=pl.BlockSpec((RB, D_MODEL), lambda i, o: (i, 0)),
            scratch_shapes=[pltpu.VMEM((2, N_EXPERTS * WIN, D_MODEL), BF16),
                            pltpu.SemaphoreType.DMA((2, N_EXPERTS))],
        ),
        out_shape=jax.ShapeDtypeStruct((m, D_MODEL), F32),
        compiler_params=_params("arbitrary"),
        name="combine",
    )(offs, aff_t, thr, nxt, jb, x1, g, ye)


def _stack_w2(w2, lo):
    w = jnp.pad(w2, ((lo, _LR - GLA_GATE_RANK - lo), (0, 0)))
    hi = w.astype(BF16)
    lo_part = (w - hi.astype(F32)).astype(BF16)
    return jnp.concatenate([hi, hi, lo_part, jnp.zeros_like(hi)], axis=0)


def _trunk(x3, wts):
    n_seq, seq_len, _ = x3.shape
    m = n_seq * seq_len
    x = x3.reshape(m, D_MODEL)
    qkvog, cb, u, gates, lr3 = _inproj(x, wts["norm_mix_g"], wts["w_main"], wts["w_lr3"], wts["b_merge"], tm=256)
    og = _gla(qkvog, lr3, wts["w2f"], wts["b2f"], wts["w2b"], wts["b2b"], wts["gla_norm_g"],
              n_seq, seq_len, tb=256)
    x1, h2, aff = _merge(x, og, cb, u, gates, wts["conv_w"], wts["w_gla_out"], wts["w_conv_out"], wts["w_out"],
                         wts["norm_ffn_g"], wts["w_router"], seq_len, tm=512)
    cap = max(1, EC_CAPACITY_FACTOR * m // N_EXPERTS)
    assert cap >= WIN and cap % 512 == 0 and m % 2048 == 0
    aff_t = aff.T
    thr, nxt, jb = _route(aff_t, cap)
    row = lambda a: a.reshape(1, N_EXPERTS)
    offs = _offsets(aff, row(thr), row(nxt), row(jb), to=2048)
    offs = jnp.concatenate([offs, jnp.full((1, N_EXPERTS), cap, jnp.int32)], axis=0).reshape(-1)
    xe = _gather(offs, aff_t, aff, thr, nxt, jb, h2, cap)
    ye = _ffn(xe, cap, wts["w_exp_gate"], wts["w_exp_up"], wts["w_exp_down"], tc=512)
    out = _combine(offs, aff_t, thr, nxt, jb, x1, wts["norm_final_g"], ye, cap)
    return out.reshape(n_seq, seq_len, D_MODEL)


def kernel(x_prompt, x_sample, norm_mix_g, w_in, w_gk2_fwd, b_gk_fwd, w_gk2_bwd, b_gk_bwd, gla_norm_g, w_gla_out,
           conv_w, w_conv_out, b_merge, w_out, norm_ffn_g, w_router, w_exp_gate, w_exp_up, w_exp_down, norm_final_g):
    w = w_in[0]
    lr0 = _QKVOG
    w_lr = w[:, lr0:lr0 + _LR].astype(BF16)
    wts = {
        "norm_mix_g": norm_mix_g[0][None, :],
        "w_main": jnp.concatenate([w[:, :lr0], w[:, lr0 + _LR:]], axis=1).astype(BF16),
        "w_lr3": jnp.concatenate([w_lr, w_lr, w_lr, jnp.zeros_like(w_lr)], axis=1),
        "w2f": _stack_w2(w_gk2_fwd[0], 0),
        "w2b": _stack_w2(w_gk2_bwd[0], GLA_GATE_RANK),
        "b2f": b_gk_fwd[0][None, :],
        "b2b": b_gk_bwd[0][None, :],
        "gla_norm_g": gla_norm_g[0].reshape(1, GLA_V_WIDTH),
        "w_gla_out": w_gla_out[0].astype(BF16),
        "conv_w": conv_w[0],
        "w_conv_out": w_conv_out[0].astype(BF16),
        "b_merge": b_merge[0][None, :],
        "w_out": w_out[0].astype(BF16),
        "norm_ffn_g": norm_ffn_g[0][None, :],
        "w_router": w_router[0].astype(BF16),
        "w_exp_gate": w_exp_gate[0].astype(BF16),
        "w_exp_up": w_exp_up[0].astype(BF16),
        "w_exp_down": w_exp_down[0].astype(BF16),
        "norm_final_g": norm_final_g[None, :],
    }
    return (_trunk(x_prompt, wts), _trunk(x_sample, wts))
```

```python
import functools

import jax
import jax.numpy as jnp
from jax import lax
from jax.experimental import pallas as pl
from jax.experimental.pallas import tpu as pltpu

D_MODEL = 1024
GLA_HEADS = 4
GLA_DK = 128
GLA_DV = 256
GLA_QK_WIDTH = GLA_HEADS * GLA_DK
GLA_V_WIDTH = GLA_HEADS * GLA_DV
GLA_GATE_RANK = 16
GLA_GATE_NORM = 16.0
GLA_CHUNK = 64
GLA_BLOCK = 256
N_EXPERTS = 16
EC_CAPACITY_FACTOR = 2
D_EXPERT = 2 * D_MODEL
EPS = 1e-6

BF16 = jnp.bfloat16
F32 = jnp.float32

VMEM_LIMIT_BYTES = 56 * 1024 * 1024

_QKVOG = 2 * GLA_QK_WIDTH + 2 * GLA_V_WIDTH
_LR = 2 * GLA_GATE_RANK
_LR3 = 128


def _dot(a, b):
    return jnp.dot(a, b, preferred_element_type=F32)


def _dot_nt(a, b):
    return lax.dot_general(a, b, (((1,), (1,)), ((), ())), preferred_element_type=F32)


def _dot_tn(a, b):
    return lax.dot_general(a, b, (((0,), (0,)), ((), ())), preferred_element_type=F32)


def _params(*sem):
    return pltpu.CompilerParams(dimension_semantics=sem, vmem_limit_bytes=VMEM_LIMIT_BYTES)


def _inproj_kernel(x_ref, g_ref, w_ref, wlr_ref, bm_ref, qkvog_ref, cb_ref, u_ref, gates_ref, lr3_ref):
    x = x_ref[...]
    h = x * lax.rsqrt(jnp.mean(x * x, axis=-1, keepdims=True) + EPS) * g_ref[...]
    hb = h.astype(BF16)
    lr = _dot(hb, wlr_ref[...])
    hi = lr.astype(BF16)
    lo = (lr - hi.astype(F32)).astype(BF16)
    lane = lax.broadcasted_iota(jnp.int32, lr.shape, 1)
    lr3_ref[...] = jnp.where((lane >= _LR) & (lane < 2 * _LR), lo, hi)
    q = _dot(hb, w_ref[:, 0:GLA_QK_WIDTH]) * (GLA_DK ** -0.5)
    qkvog_ref[:, 0:GLA_QK_WIDTH] = q.astype(BF16)
    for c0 in range(GLA_QK_WIDTH, _QKVOG, 512):
        qkvog_ref[:, c0:c0 + 512] = _dot(hb, w_ref[:, c0:c0 + 512]).astype(BF16)
    o = _QKVOG
    for c0 in range(0, D_MODEL, 512):
        cb_ref[:, c0:c0 + 512] = _dot(hb, w_ref[:, o + c0:o + c0 + 512]).astype(BF16)
    for c0 in range(0, D_MODEL, 512):
        cc = _dot(hb, w_ref[:, o + D_MODEL + c0:o + D_MODEL + c0 + 512])
        cx = _dot(hb, w_ref[:, o + 2 * D_MODEL + c0:o + 2 * D_MODEL + c0 + 512])
        u_ref[:, c0:c0 + 512] = (cc * cx).astype(BF16)
    o = _QKVOG + 3 * D_MODEL
    for c0 in range(0, 2 * D_MODEL, 512):
        gm = _dot(hb, w_ref[:, o + c0:o + c0 + 512]) + bm_ref[:, c0:c0 + 512]
        gates_ref[:, c0:c0 + 512] = jax.nn.sigmoid(gm).astype(BF16)


def _inproj(x, g, w_main, w_lr3, b_merge, tm):
    m = x.shape[0]
    ncols = w_main.shape[1]
    const = lambda i: (0, 0)
    row = lambda i: (i, 0)
    return pl.pallas_call(
        _inproj_kernel,
        grid=(m // tm,),
        in_specs=[
            pl.BlockSpec((tm, D_MODEL), row),
            pl.BlockSpec((1, D_MODEL), const),
            pl.BlockSpec((D_MODEL, ncols), const, pipeline_mode=pl.Buffered(1)),
            pl.BlockSpec((D_MODEL, _LR3), const),
            pl.BlockSpec((1, 2 * D_MODEL), const),
        ],
        out_specs=[
            pl.BlockSpec((tm, _QKVOG), row),
            pl.BlockSpec((tm, D_MODEL), row),
            pl.BlockSpec((tm, D_MODEL), row),
            pl.BlockSpec((tm, 2 * D_MODEL), row),
            pl.BlockSpec((tm, _LR3), row),
        ],
        out_shape=[
            jax.ShapeDtypeStruct((m, _QKVOG), BF16),
            jax.ShapeDtypeStruct((m, D_MODEL), BF16),
            jax.ShapeDtypeStruct((m, D_MODEL), BF16),
            jax.ShapeDtypeStruct((m, 2 * D_MODEL), BF16),
            jax.ShapeDtypeStruct((m, _LR3), BF16),
        ],
        compiler_params=_params("parallel"),
        name="inproj",
    )(x, g, w_main, w_lr3, b_merge)


def _gla_body(rev, rows, lr3_ref, w2_ref, b2_ref, q_ref, k_ref, v_ref, st_ref):
    L = GLA_CHUNK
    tb = GLA_BLOCK
    nc = tb // L
    g = _dot(lr3_ref[rows, :], w2_ref[...]) + b2_ref[...]
    la = (jnp.minimum(g, 0.0) - jnp.log(1.0 + jnp.exp(-jnp.abs(g)))) * (1.0 / GLA_GATE_NORM)
    ri = lax.broadcasted_iota(jnp.int32, (tb, tb), 0)
    ci = lax.broadcasted_iota(jnp.int32, (tb, tb), 1)
    keep = ((ri // L) == (ci // L)) & ((ci >= ri) if rev else (ci <= ri))
    tri = keep.astype(BF16)
    la_hi = la.astype(BF16)
    la_lo = (la - la_hi.astype(F32)).astype(BF16)
    b = _dot(tri, la_hi) + _dot(tri, la_lo)
    ref_row = L // 2 if rev else L // 2 - 1
    end_row = 0 if rev else L - 1
    b_end_rows = [b[c * L + end_row:c * L + end_row + 1, :] for c in range(nc)]
    b_ref = jnp.concatenate(
        [jnp.broadcast_to(b[c * L + ref_row:c * L + ref_row + 1, :], (L, GLA_QK_WIDTH)) for c in range(nc)], axis=0)
    b_end = jnp.concatenate([jnp.broadcast_to(x, (L, GLA_QK_WIDTH)) for x in b_end_rows], axis=0)
    a_end_rows = [jnp.exp(x) for x in b_end_rows]

    q = q_ref[rows, :].astype(F32)
    k = k_ref[rows, :].astype(F32)
    qe = (q * jnp.exp(b - b_ref)).astype(BF16)
    ke = (k * jnp.exp(b_ref - b)).astype(BF16)
    q_in = (q * jnp.exp(b)).astype(BF16)
    k_out = (k * jnp.exp(b_end - b)).astype(BF16)
    order = range(nc - 1, -1, -1) if rev else range(nc)
    outs = []
    for h in range(GLA_HEADS):
        ks = slice(h * GLA_DK, (h + 1) * GLA_DK)
        v = v_ref[rows, h * GLA_DV:(h + 1) * GLA_DV]
        a = jnp.where(keep, _dot_nt(qe[:, ks], ke[:, ks]), 0.0).astype(BF16)
        o_intra = _dot(a, v)
        st = st_ref[h]
        o_inter = [None] * nc
        for c in order:
            cr = slice(c * L, (c + 1) * L)
            o_inter[c] = _dot_nt(q_in[cr, ks], st.astype(BF16))
            st = st * a_end_rows[c][:, ks] + _dot_tn(v[cr], k_out[cr, ks])
        st_ref[h] = st
        outs.append(o_intra + jnp.concatenate(o_inter, axis=0))
    return outs


def _gla_fwd_kernel(tb, lr3_ref, w2_ref, b2_ref, q_ref, k_ref, v_ref, o_ref, st_ref):
    @pl.when(pl.program_id(1) == 0)
    def _():
        st_ref[...] = jnp.zeros_like(st_ref)
    for sb in range(tb // GLA_BLOCK):
        rows = slice(sb * GLA_BLOCK, (sb + 1) * GLA_BLOCK)
        outs = _gla_body(False, rows, lr3_ref, w2_ref, b2_ref, q_ref, k_ref, v_ref, st_ref)
        for h in range(GLA_HEADS):
            o_ref[rows, h * GLA_DV:(h + 1) * GLA_DV] = outs[h]


def _gla_bwd_kernel(tb, lr3_ref, w2_ref, b2_ref, q_ref, k_ref, v_ref, of_ref, og_ref, gn_ref, o_ref, st_ref):
    @pl.when(pl.program_id(1) == 0)
    def _():
        st_ref[...] = jnp.zeros_like(st_ref)
    for sb in range(tb // GLA_BLOCK - 1, -1, -1):
        rows = slice(sb * GLA_BLOCK, (sb + 1) * GLA_BLOCK)
        outs = _gla_body(True, rows, lr3_ref, w2_ref, b2_ref, q_ref, k_ref, v_ref, st_ref)
        for h in range(GLA_HEADS):
            vs = slice(h * GLA_DV, (h + 1) * GLA_DV)
            o = of_ref[rows, vs] + outs[h]
            o = o * lax.rsqrt(jnp.mean(o * o, axis=-1, keepdims=True) + EPS) * gn_ref[:, vs]
            og = og_ref[rows, vs].astype(F32)
            o_ref[rows, vs] = (o * (og * jax.nn.sigmoid(og))).astype(BF16)


def _gla(qkvog, lr3, w2f, b2f, w2b, b2b, gn, n_seq, seq_len, tb):
    m = qkvog.shape[0]
    nb = seq_len // tb
    grid = (n_seq, nb)
    const = lambda b, n: (0, 0)

    def specs(rowf):
        return [
            pl.BlockSpec((tb, _LR3), lambda b, n: (rowf(b, n), 0)),
            pl.BlockSpec((_LR3, GLA_QK_WIDTH), const),
            pl.BlockSpec((1, GLA_QK_WIDTH), const),
            pl.BlockSpec((tb, GLA_QK_WIDTH), lambda b, n: (rowf(b, n), 0)),
            pl.BlockSpec((tb, GLA_QK_WIDTH), lambda b, n: (rowf(b, n), 1)),
            pl.BlockSpec((tb, GLA_V_WIDTH), lambda b, n: (rowf(b, n), 1)),
        ]

    fwd_row = lambda b, n: b * nb + n
    bwd_row = lambda b, n: b * nb + (nb - 1 - n)
    st = pltpu.VMEM((GLA_HEADS, GLA_DV, GLA_DK), F32)
    o_fwd = pl.pallas_call(
        functools.partial(_gla_fwd_kernel, tb),
        grid=grid,
        in_specs=specs(fwd_row),
        out_specs=pl.BlockSpec((tb, GLA_V_WIDTH), lambda b, n: (fwd_row(b, n), 0)),
        out_shape=jax.ShapeDtypeStruct((m, GLA_V_WIDTH), F32),
        scratch_shapes=[st],
        compiler_params=_params("parallel", "arbitrary"),
        name="gla_fwd",
    )(lr3, w2f, b2f, qkvog, qkvog, qkvog)
    return pl.pallas_call(
        functools.partial(_gla_bwd_kernel, tb),
        grid=grid,
        in_specs=specs(bwd_row) + [
            pl.BlockSpec((tb, GLA_V_WIDTH), lambda b, n: (bwd_row(b, n), 0)),
            pl.BlockSpec((tb, GLA_V_WIDTH), lambda b, n: (bwd_row(b, n), 2)),
            pl.BlockSpec((1, GLA_V_WIDTH), const),
        ],
        out_specs=pl.BlockSpec((tb, GLA_V_WIDTH), lambda b, n: (bwd_row(b, n), 0)),
        out_shape=jax.ShapeDtypeStruct((m, GLA_V_WIDTH), BF16),
        scratch_shapes=[st],
        compiler_params=_params("parallel", "arbitrary"),
        name="gla_bwd",
    )(lr3, w2b, b2b, qkvog, qkvog, qkvog, o_fwd, qkvog, gn)


_HALO = 16
_MERGE_ROWS = 256


def _merge_kernel(tm, seq_len, x_ref, og_ref, cb_ref, u_ref, up_ref, un_ref, gates_ref, cw_ref,
                  wa_ref, wb_ref, wo_ref, g2_ref, wr_ref, x1_ref, h2_ref, aff_ref):
    i = pl.program_id(0)
    u = u_ref[...].astype(F32)
    first = (i * tm) % seq_len == 0
    last = ((i + 1) * tm) % seq_len == 0
    prev_row = jnp.where(first, 0.0, up_ref[_HALO - 1:_HALO, :].astype(F32))
    next_row = jnp.where(last, 0.0, un_ref[0:1, :].astype(F32))
    r = lax.broadcasted_iota(jnp.int32, (tm, D_MODEL), 0)
    um1 = jnp.where(r == 0, prev_row, pltpu.roll(u, 1, 0))
    up1 = jnp.where(r == tm - 1, next_row, pltpu.roll(u, tm - 1, 0))
    hc = cw_ref[0:1, :] * um1 + cw_ref[1:2, :] * u + cw_ref[2:3, :] * up1
    cbh = (cb_ref[...].astype(F32) * hc).astype(BF16)
    for r0 in range(0, tm, _MERGE_ROWS):
        rows = slice(r0, r0 + _MERGE_ROWS)
        y_b = _dot(cbh[rows], wb_ref[...])
        y_a = _dot(og_ref[rows, :], wa_ref[...])
        mix = (gates_ref[rows, 0:D_MODEL].astype(F32) * y_a
               + gates_ref[rows, D_MODEL:2 * D_MODEL].astype(F32) * y_b)
        x1 = x_ref[rows, :] + _dot(mix.astype(BF16), wo_ref[...])
        x1_ref[rows, :] = x1
        h2 = (x1 * lax.rsqrt(jnp.mean(x1 * x1, axis=-1, keepdims=True) + EPS) * g2_ref[...]).astype(BF16)
        h2_ref[rows, :] = h2
        logits = _dot(h2, wr_ref[...])
        e = jnp.exp(logits - jnp.max(logits, axis=-1, keepdims=True))
        aff_ref[rows, :] = e / jnp.sum(e, axis=-1, keepdims=True)


def _merge(x, og, cb, u, gates, conv_w, wa, wb, wo, g2, wr, seq_len, tm):
    m = x.shape[0]
    const = lambda i: (0, 0)
    row = lambda i: (i, 0)
    hb = tm // _HALO
    nhb = m // _HALO
    sq = lambda: pl.BlockSpec((D_MODEL, D_MODEL), const)
    return pl.pallas_call(
        functools.partial(_merge_kernel, tm, seq_len),
        grid=(m // tm,),
        in_specs=[
            pl.BlockSpec((tm, D_MODEL), row),
            pl.BlockSpec((tm, D_MODEL), row),
            pl.BlockSpec((tm, D_MODEL), row),
            pl.BlockSpec((tm, D_MODEL), row),
            pl.BlockSpec((_HALO, D_MODEL), lambda i: (jnp.maximum(i * hb - 1, 0), 0)),
            pl.BlockSpec((_HALO, D_MODEL), lambda i: (jnp.minimum((i + 1) * hb, nhb - 1), 0)),
            pl.BlockSpec((tm, 2 * D_MODEL), row),
            pl.BlockSpec((3, D_MODEL), const),
            sq(), sq(), sq(),
            pl.BlockSpec((1, D_MODEL), const),
            pl.BlockSpec((D_MODEL, N_EXPERTS), const),
        ],
        out_specs=[
            pl.BlockSpec((tm, D_MODEL), row),
            pl.BlockSpec((tm, D_MODEL), row),
            pl.BlockSpec((tm, N_EXPERTS), row),
        ],
        out_shape=[
            jax.ShapeDtypeStruct((m, D_MODEL), F32),
            jax.ShapeDtypeStruct((m, D_MODEL), BF16),
            jax.ShapeDtypeStruct((m, N_EXPERTS), F32),
        ],
        compiler_params=_params("parallel"),
        name="merge",
    )(x, og, cb, u, u, u, gates, conv_w, wa, wb, wo, g2, wr)


RB = 256
SUB = 64
ROW_TILE = 16
WIN = SUB + ROW_TILE
_XW = D_MODEL + 128
_IDX_BITS = 30
_MIN_NORMAL_BITS = 0x00800000


def _route_kernel(cap, n_tok_bits, aff_ref, thr_ref, nxt_ref, jb_ref):
    aff = aff_ref[...]
    idx = lax.broadcasted_iota(jnp.int32, aff.shape, 1)
    capf = jnp.float32(cap)

    def count(mask):
        return jnp.sum(mask.astype(F32), axis=1, keepdims=True)

    def as_f32(bits):
        return lax.bitcast_convert_type(bits, F32)

    def value_step(i, prefix):
        cand = prefix | jnp.left_shift(jnp.int32(1), _IDX_BITS - 1 - i)
        return jnp.where(count(aff >= as_f32(cand)) >= capf, cand, prefix)

    thr_bits = lax.fori_loop(0, _IDX_BITS, value_step, jnp.zeros((N_EXPERTS, 1), jnp.int32))
    thr = as_f32(thr_bits)
    nxt = as_f32(jnp.maximum(thr_bits + 1, _MIN_NORMAL_BITS))
    need = capf - count(aff >= nxt)
    tie = (aff >= thr) & (aff < nxt)

    def index_step(i, j):
        cand = j | jnp.left_shift(jnp.int32(1), n_tok_bits - 1 - i)
        return jnp.where(count(tie & (idx < cand)) < need, cand, j)

    thr_ref[...] = thr
    nxt_ref[...] = nxt
    jb_ref[...] = lax.fori_loop(0, n_tok_bits, index_step, jnp.zeros((N_EXPERTS, 1), jnp.int32))


def _route(aff_t, cap):
    n_tok = aff_t.shape[1]
    val = jax.ShapeDtypeStruct((N_EXPERTS, 1), F32)
    return pl.pallas_call(
        functools.partial(_route_kernel, cap, max(1, (n_tok - 1).bit_length())),
        out_shape=[val, val, jax.ShapeDtypeStruct((N_EXPERTS, 1), jnp.int32)],
        compiler_params=pltpu.CompilerParams(vmem_limit_bytes=VMEM_LIMIT_BYTES),
        name="route",
    )(aff_t)


def _selected(aff, tok, thr, nxt, jb):
    return (aff >= nxt) | ((aff >= thr) & (tok <= jb))


def _offsets_kernel(to, aff_ref, thr_ref, nxt_ref, jb_ref, offs_ref, carry_ref):
    i = pl.program_id(0)

    @pl.when(i == 0)
    def _():
        carry_ref[...] = jnp.zeros_like(carry_ref)

    aff = aff_ref[...]
    tok = i * to + lax.broadcasted_iota(jnp.int32, aff.shape, 0)
    sel = _selected(aff, tok, thr_ref[...], nxt_ref[...], jb_ref[...]).astype(BF16)
    ns = to // SUB
    grp = (lax.broadcasted_iota(jnp.int32, (ns, to), 1) // SUB == lax.broadcasted_iota(jnp.int32, (ns, to), 0))
    cnt = _dot(grp.astype(BF16), sel)
    before = (lax.broadcasted_iota(jnp.int32, (ns, ns), 1) < lax.broadcasted_iota(jnp.int32, (ns, ns), 0))
    offs_ref[...] = (carry_ref[...] + _dot(before.astype(BF16), cnt.astype(BF16))).astype(jnp.int32)
    carry_ref[...] += jnp.sum(cnt, axis=0, keepdims=True)


def _offsets(aff, thr_row, nxt_row, jb_row, to):
    m = aff.shape[0]
    const = lambda i: (0, 0)
    row = lambda: pl.BlockSpec((1, N_EXPERTS), const)
    return pl.pallas_call(
        functools.partial(_offsets_kernel, to),
        grid=(m // to,),
        in_specs=[pl.BlockSpec((to, N_EXPERTS), lambda i: (i, 0)), row(), row(), row()],
        out_specs=pl.BlockSpec((to // SUB, N_EXPERTS), lambda i: (i, 0)),
        out_shape=jax.ShapeDtypeStruct((m // SUB, N_EXPERTS), jnp.int32),
        scratch_shapes=[pltpu.VMEM((1, N_EXPERTS), F32)],
        compiler_params=_params("arbitrary"),
        name="offsets",
    )(aff, thr_row, nxt_row, jb_row)


def _block_selection(blk, aff_t_ref, thr_ref, nxt_ref, jb_ref):
    aff = aff_t_ref[...]
    tok = blk * RB + lax.broadcasted_iota(jnp.int32, aff.shape, 1)
    return _selected(aff, tok, thr_ref[...], nxt_ref[...], jb_ref[...])


def _floor_tile(x):
    return pl.multiple_of((x // ROW_TILE) * ROW_TILE, ROW_TILE)


def _slot_onehot(sel, shift):
    self = sel.astype(F32)
    before = (lax.broadcasted_iota(jnp.int32, (RB, RB), 0) < lax.broadcasted_iota(jnp.int32, (RB, RB), 1))
    rank = _dot(self.astype(BF16), before.astype(BF16)).astype(jnp.int32)
    slot = lax.broadcasted_iota(jnp.int32, (WIN, RB), 0)
    rows = [jnp.where(rank[e:e + 1, :] + shift[e] == slot, self[e:e + 1, :], 0.0) for e in range(N_EXPERTS)]
    return jnp.concatenate(rows, axis=0).astype(BF16)


def _block_counts(offs_ref, blk):
    per = RB // SUB
    start = [offs_ref[blk * per * N_EXPERTS + e] for e in range(N_EXPERTS)]
    end = [offs_ref[(blk + 1) * per * N_EXPERTS + e] for e in range(N_EXPERTS)]
    most = functools.reduce(jnp.maximum, [b - a for a, b in zip(start, end)])
    return most <= SUB


def _sub_mask(sub):
    return lax.broadcasted_iota(jnp.int32, (N_EXPERTS, RB), 1) // SUB == sub


def _gather_kernel(cap, nb, offs_ref, aff_t_ref, aff_ref, thr_ref, nxt_ref, jb_ref, h2_ref, xe_ref,
                   stage_ref, carry_ref, sem_ref):
    blk = pl.program_id(0)
    par = blk % 2
    per = RB // SUB

    def copy(parity, e, dst_row):
        return pltpu.make_async_copy(stage_ref.at[parity, pl.ds(e * WIN, WIN)],
                                     xe_ref.at[e, pl.ds(dst_row, WIN)], sem_ref.at[parity, e])

    def wait_all(parity):
        for e in range(N_EXPERTS):
            copy(parity, e, 0).wait()

    @pl.when(blk == 0)
    def _():
        carry_ref[...] = jnp.zeros_like(carry_ref)
        stage_ref[1] = jnp.zeros(stage_ref.shape[1:], BF16)
        for e in range(N_EXPERTS):
            copy(1, e, cap).start()
        wait_all(1)

    sel = _block_selection(blk, aff_t_ref, thr_ref, nxt_ref, jb_ref)
    aff = aff_ref[...]
    hi = aff.astype(BF16)
    r1 = aff - hi.astype(F32)
    mid = r1.astype(BF16)
    lo = (r1 - mid.astype(F32)).astype(BF16)
    er = lax.broadcasted_iota(jnp.int32, (N_EXPERTS, 128), 0)
    ec = lax.broadcasted_iota(jnp.int32, (N_EXPERTS, 128), 1)
    g3 = (_dot(hi, (ec == 3 * er).astype(BF16)) + _dot(mid, (ec == 3 * er + 1).astype(BF16))
          + _dot(lo, (ec == 3 * er + 2).astype(BF16)))
    src = jnp.concatenate([h2_ref[...], g3.astype(BF16)], axis=1)

    def emit(mask, seg_lo, seg_hi, first):
        lo_rows = [offs_ref[seg_lo * N_EXPERTS + e] for e in range(N_EXPERTS)]
        hi_rows = [offs_ref[seg_hi * N_EXPERTS + e] for e in range(N_EXPERTS)]
        base = [_floor_tile(r) for r in lo_rows]
        stage_ref[par] = _dot(_slot_onehot(mask, [r - b for r, b in zip(lo_rows, base)]), src).astype(BF16)
        for e in range(N_EXPERTS):
            head = pl.ds(e * WIN, ROW_TILE)
            stage_ref[par, head] = stage_ref[par, head] + carry_ref[e]
            nxt = _floor_tile(hi_rows[e]) - base[e]
            carry_ref[e] = stage_ref[par, pl.ds(pl.multiple_of(e * WIN + nxt, ROW_TILE), ROW_TILE)]
        if first:
            @pl.when(blk > 0)
            def _():
                wait_all(1 - par)
        for e in range(N_EXPERTS):
            copy(par, e, base[e]).start()

    fits = _block_counts(offs_ref, blk)

    @pl.when(fits)
    def _():
        emit(sel, blk * per, (blk + 1) * per, True)

    @pl.when(jnp.logical_not(fits))
    def _():
        for sub in range(per):
            if sub > 0:
                wait_all(par)
            emit(sel & _sub_mask(sub), blk * per + sub, blk * per + sub + 1, sub == 0)

    @pl.when(blk == nb - 1)
    def _():
        wait_all(par)


def _gather(offs, aff_t, aff, thr, nxt, jb, h2, cap):
    m = h2.shape[0]
    nb = m // RB
    col = lambda i, o: (0, 0)
    return pl.pallas_call(
        functools.partial(_gather_kernel, cap, nb),
        grid_spec=pltpu.PrefetchScalarGridSpec(
            num_scalar_prefetch=1,
            grid=(nb,),
            in_specs=[
                pl.BlockSpec((N_EXPERTS, RB), lambda i, o: (0, i)),
                pl.BlockSpec((RB, N_EXPERTS), lambda i, o: (i, 0)),
                pl.BlockSpec((N_EXPERTS, 1), col), pl.BlockSpec((N_EXPERTS, 1), col),
                pl.BlockSpec((N_EXPERTS, 1), col),
                pl.BlockSpec((RB, D_MODEL), lambda i, o: (i, 0)),
            ],
            out_specs=pl.BlockSpec(memory_space=pl.ANY),
            scratch_shapes=[pltpu.VMEM((2, N_EXPERTS * WIN, _XW), BF16),
                            pltpu.VMEM((N_EXPERTS, ROW_TILE, _XW), BF16),
                            pltpu.SemaphoreType.DMA((2, N_EXPERTS))],
        ),
        out_shape=jax.ShapeDtypeStruct((N_EXPERTS, cap + WIN, _XW), BF16),
        compiler_params=_params("arbitrary"),
        name="gather",
    )(offs, aff_t, aff, thr, nxt, jb, h2)


def _ffn_kernel(x_ref, wg_ref, wu_ref, wd_ref, o_ref, hid_ref):
    e = pl.program_id(0)
    x = x_ref[0, :, 0:D_MODEL]
    gcols = x_ref[0, :, D_MODEL:_XW].astype(F32)
    lane = lax.broadcasted_iota(jnp.int32, gcols.shape, 1)
    gate = jnp.sum(jnp.where((lane >= 3 * e) & (lane < 3 * e + 3), gcols, 0.0), axis=-1, keepdims=True)
    for f0 in range(0, D_EXPERT, 1024):
        g = _dot(x, wg_ref[0, :, f0:f0 + 1024])
        up = _dot(x, wu_ref[0, :, f0:f0 + 1024])
        hid_ref[:, f0:f0 + 1024] = (g * jax.nn.sigmoid(g) * up).astype(BF16)
    o_ref[0] = (_dot(hid_ref[...], wd_ref[0]) * gate).astype(BF16)


def _ffn(xe, cap, wg, wu, wd, tc):
    e = xe.shape[0]
    return pl.pallas_call(
        _ffn_kernel,
        grid=(e, cap // tc),
        in_specs=[
            pl.BlockSpec((1, tc, _XW), lambda e, j: (e, j, 0)),
            pl.BlockSpec((1, D_MODEL, D_EXPERT), lambda e, j: (e, 0, 0)),
            pl.BlockSpec((1, D_MODEL, D_EXPERT), lambda e, j: (e, 0, 0)),
            pl.BlockSpec((1, D_EXPERT, D_MODEL), lambda e, j: (e, 0, 0)),
        ],
        out_specs=pl.BlockSpec((1, tc, D_MODEL), lambda e, j: (e, j, 0)),
        out_shape=jax.ShapeDtypeStruct((e, cap, D_MODEL), BF16),
        scratch_shapes=[pltpu.VMEM((tc, D_EXPERT), BF16)],
        compiler_params=_params("parallel", "parallel"),
        name="experts",
    )(xe, wg, wu, wd)


def _combine_kernel(cap, nb, offs_ref, aff_t_ref, thr_ref, nxt_ref, jb_ref, x1_ref, g_ref, ye_ref, o_ref,
                    win_ref, sem_ref):
    blk = pl.program_id(0)
    par = blk % 2
    per = RB // SUB

    def window(row):
        first = pl.multiple_of(jnp.minimum(_floor_tile(row), cap - WIN), ROW_TILE)
        return first, row - first

    def copy(parity, e, first):
        return pltpu.make_async_copy(ye_ref.at[e, pl.ds(first, WIN)],
                                     win_ref.at[parity, pl.ds(e * WIN, WIN)], sem_ref.at[parity, e])

    def fetch(parity, rows):
        for e in range(N_EXPERTS):
            copy(parity, e, window(rows[e])[0]).start()

    def wait_all(parity):
        for e in range(N_EXPERTS):
            copy(parity, e, 0).wait()

    def block_rows(b, sub):
        return [offs_ref[(b * per + sub) * N_EXPERTS + e] for e in range(N_EXPERTS)]

    @pl.when(blk == 0)
    def _():
        fetch(par, block_rows(blk, 0))

    @pl.when(blk + 1 < nb)
    def _():
        fetch(1 - par, block_rows(blk + 1, 0))

    sel = _block_selection(blk, aff_t_ref, thr_ref, nxt_ref, jb_ref)
    fits = _block_counts(offs_ref, blk)
    start = block_rows(blk, 0)

    def contribution(mask, rows):
        onehot = _slot_onehot(mask, [window(r)[1] for r in rows])
        return _dot_tn(onehot, win_ref[par])

    def finish(y):
        x = x1_ref[...] + y
        o_ref[...] = x * lax.rsqrt(jnp.mean(x * x, axis=-1, keepdims=True) + EPS) * g_ref[...]

    @pl.when(fits)
    def _():
        wait_all(par)
        finish(contribution(sel, start))

    @pl.when(jnp.logical_not(fits))
    def _():
        wait_all(par)
        y = contribution(sel & _sub_mask(0), start)
        for sub in range(1, per):
            rows = block_rows(blk, sub)
            fetch(par, rows)
            wait_all(par)
            y = y + contribution(sel & _sub_mask(sub), rows)
        finish(y)


def _combine(offs, aff_t, thr, nxt, jb, x1, g, ye, cap):
    m = x1.shape[0]
    nb = m // RB
    col = lambda i, o: (0, 0)
    return pl.pallas_call(
        functools.partial(_combine_kernel, cap, nb),
        grid_spec=pltpu.PrefetchScalarGridSpec(
            num_scalar_prefetch=1,
            grid=(nb,),
            in_specs=[
                pl.BlockSpec((N_EXPERTS, RB), lambda i, o: (0, i)),
                pl.BlockSpec((N_EXPERTS, 1), col), pl.BlockSpec((N_EXPERTS, 1), col),
                pl.BlockSpec((N_EXPERTS, 1), col),
                pl.BlockSpec((RB, D_MODEL), lambda i, o: (i, 0)),
                pl.BlockSpec((1, D_MODEL), col),
                pl.BlockSpec(memory_space=pl.ANY),
            ],
            out_specs=pl.BlockSpec((RB, D_MODEL), lambda i, o: (i, 0)),
            scratch_shapes=[pltpu.VMEM((2, N_EXPERTS * WIN, D_MODEL), BF16),
                            pltpu.SemaphoreType.DMA((2, N_EXPERTS))],
        ),
        out_shape=jax.ShapeDtypeStruct((m, D_MODEL), F32),
        compiler_params=_params("arbitrary"),
        name="combine",
    )(offs, aff_t, thr, nxt, jb, x1, g, ye)


def _stack_w2(w2, lo):
    w = jnp.pad(w2, ((lo, _LR - GLA_GATE_RANK - lo), (0, 0)))
    hi = w.astype(BF16)
    lo_part = (w - hi.astype(F32)).astype(BF16)
    return jnp.concatenate([hi, hi, lo_part, jnp.zeros_like(hi)], axis=0)


_TILE_INPROJ = 256
_TILE_GLA = 512
_TILE_MERGE = 512
_TILE_OFFSETS = 2048
_TILE_EXPERTS = 1024


def _trunk(x3, wts):
    n_seq, seq_len, _ = x3.shape
    m = n_seq * seq_len
    cap = max(1, EC_CAPACITY_FACTOR * m // N_EXPERTS)
    assert seq_len % _TILE_GLA == 0 and seq_len % _TILE_MERGE == 0 and m % _TILE_OFFSETS == 0
    assert cap >= WIN and cap % _TILE_EXPERTS == 0
    x = x3.reshape(m, D_MODEL)
    qkvog, cb, u, gates, lr3 = _inproj(x, wts["norm_mix_g"], wts["w_main"], wts["w_lr3"], wts["b_merge"],
                                       tm=_TILE_INPROJ)
    og = _gla(qkvog, lr3, wts["w2f"], wts["b2f"], wts["w2b"], wts["b2b"], wts["gla_norm_g"],
              n_seq, seq_len, tb=_TILE_GLA)
    x1, h2, aff = _merge(x, og, cb, u, gates, wts["conv_w"], wts["w_gla_out"], wts["w_conv_out"], wts["w_out"],
                         wts["norm_ffn_g"], wts["w_router"], seq_len, tm=_TILE_MERGE)
    aff_t = aff.T
    thr, nxt, jb = _route(aff_t, cap)
    row = lambda a: a.reshape(1, N_EXPERTS)
    offs = _offsets(aff, row(thr), row(nxt), row(jb), to=_TILE_OFFSETS)
    offs = jnp.concatenate([offs, jnp.full((1, N_EXPERTS), cap, jnp.int32)], axis=0).reshape(-1)
    xe = _gather(offs, aff_t, aff, thr, nxt, jb, h2, cap)
    ye = _ffn(xe, cap, wts["w_exp_gate"], wts["w_exp_up"], wts["w_exp_down"], tc=_TILE_EXPERTS)
    out = _combine(offs, aff_t, thr, nxt, jb, x1, wts["norm_final_g"], ye, cap)
    return out.reshape(n_seq, seq_len, D_MODEL)


def kernel(x_prompt, x_sample, norm_mix_g, w_in, w_gk2_fwd, b_gk_fwd, w_gk2_bwd, b_gk_bwd, gla_norm_g, w_gla_out,
           conv_w, w_conv_out, b_merge, w_out, norm_ffn_g, w_router, w_exp_gate, w_exp_up, w_exp_down, norm_final_g):
    w = w_in[0]
    lr0 = _QKVOG
    w_lr = w[:, lr0:lr0 + _LR].astype(BF16)
    wts = {
        "norm_mix_g": norm_mix_g[0][None, :],
        "w_main": jnp.concatenate([w[:, :lr0], w[:, lr0 + _LR:]], axis=1).astype(BF16),
        "w_lr3": jnp.concatenate([w_lr, w_lr, w_lr, jnp.zeros_like(w_lr)], axis=1),
        "w2f": _stack_w2(w_gk2_fwd[0], 0),
        "w2b": _stack_w2(w_gk2_bwd[0], GLA_GATE_RANK),
        "b2f": b_gk_fwd[0][None, :],
        "b2b": b_gk_bwd[0][None, :],
        "gla_norm_g": gla_norm_g[0].reshape(1, GLA_V_WIDTH),
        "w_gla_out": w_gla_out[0].astype(BF16),
        "conv_w": conv_w[0],
        "w_conv_out": w_conv_out[0].astype(BF16),
        "b_merge": b_merge[0][None, :],
        "w_out": w_out[0].astype(BF16),
        "norm_ffn_g": norm_ffn_g[0][None, :],
        "w_router": w_router[0].astype(BF16),
        "w_exp_gate": w_exp_gate[0].astype(BF16),
        "w_exp_up": w_exp_up[0].astype(BF16),
        "w_exp_down": w_exp_down[0].astype(BF16),
        "norm_final_g": norm_final_g[None, :],
    }
    return (_trunk(x_prompt, wts), _trunk(x_sample, wts))
```

```python
import functools

import jax
import jax.numpy as jnp
from jax import lax
from jax.experimental import pallas as pl
from jax.experimental.pallas import tpu as pltpu

D_MODEL = 1024
GLA_HEADS = 4
GLA_DK = 128
GLA_DV = 256
GLA_QK_WIDTH = GLA_HEADS * GLA_DK
GLA_V_WIDTH = GLA_HEADS * GLA_DV
GLA_GATE_RANK = 16
GLA_GATE_NORM = 16.0
GLA_CHUNK = 64
GLA_BLOCK = 256
N_EXPERTS = 16
EC_CAPACITY_FACTOR = 2
D_EXPERT = 2 * D_MODEL
EPS = 1e-6

BF16 = jnp.bfloat16
F32 = jnp.float32

VMEM_LIMIT_BYTES = 56 * 1024 * 1024

_QKVOG = 2 * GLA_QK_WIDTH + 2 * GLA_V_WIDTH
_LR = 2 * GLA_GATE_RANK
_LR3 = 128


def _dot(a, b):
    return jnp.dot(a, b, preferred_element_type=F32)


def _dot_nt(a, b):
    return lax.dot_general(a, b, (((1,), (1,)), ((), ())), preferred_element_type=F32)


def _dot_tn(a, b):
    return lax.dot_general(a, b, (((0,), (0,)), ((), ())), preferred_element_type=F32)


def _params(*sem):
    return pltpu.CompilerParams(dimension_semantics=sem, vmem_limit_bytes=VMEM_LIMIT_BYTES)


def _inproj_kernel(x_ref, g_ref, w_ref, wlr_ref, bm_ref, qkvog_ref, cb_ref, u_ref, gates_ref, lr3_ref):
    x = x_ref[...]
    h = x * lax.rsqrt(jnp.mean(x * x, axis=-1, keepdims=True) + EPS) * g_ref[...]
    hb = h.astype(BF16)
    lr = _dot(hb, wlr_ref[...])
    hi = lr.astype(BF16)
    lo = (lr - hi.astype(F32)).astype(BF16)
    lane = lax.broadcasted_iota(jnp.int32, lr.shape, 1)
    lr3_ref[...] = jnp.where((lane >= _LR) & (lane < 2 * _LR), lo, hi)
    q = _dot(hb, w_ref[:, 0:GLA_QK_WIDTH]) * (GLA_DK ** -0.5)
    qkvog_ref[:, 0:GLA_QK_WIDTH] = q.astype(BF16)
    for c0 in range(GLA_QK_WIDTH, _QKVOG, 512):
        qkvog_ref[:, c0:c0 + 512] = _dot(hb, w_ref[:, c0:c0 + 512]).astype(BF16)
    o = _QKVOG
    for c0 in range(0, D_MODEL, 512):
        cb_ref[:, c0:c0 + 512] = _dot(hb, w_ref[:, o + c0:o + c0 + 512]).astype(BF16)
    for c0 in range(0, D_MODEL, 512):
        cc = _dot(hb, w_ref[:, o + D_MODEL + c0:o + D_MODEL + c0 + 512])
        cx = _dot(hb, w_ref[:, o + 2 * D_MODEL + c0:o + 2 * D_MODEL + c0 + 512])
        u_ref[:, c0:c0 + 512] = (cc * cx).astype(BF16)
    o = _QKVOG + 3 * D_MODEL
    for c0 in range(0, 2 * D_MODEL, 512):
        gm = _dot(hb, w_ref[:, o + c0:o + c0 + 512]) + bm_ref[:, c0:c0 + 512]
        gates_ref[:, c0:c0 + 512] = jax.nn.sigmoid(gm).astype(BF16)


def _inproj(x, g, w_main, w_lr3, b_merge, tm):
    m = x.shape[0]
    ncols = w_main.shape[1]
    const = lambda i: (0, 0)
    row = lambda i: (i, 0)
    return pl.pallas_call(
        _inproj_kernel,
        grid=(m // tm,),
        in_specs=[
            pl.BlockSpec((tm, D_MODEL), row),
            pl.BlockSpec((1, D_MODEL), const),
            pl.BlockSpec((D_MODEL, ncols), const, pipeline_mode=pl.Buffered(1)),
            pl.BlockSpec((D_MODEL, _LR3), const),
            pl.BlockSpec((1, 2 * D_MODEL), const),
        ],
        out_specs=[
            pl.BlockSpec((tm, _QKVOG), row),
            pl.BlockSpec((tm, D_MODEL), row),
            pl.BlockSpec((tm, D_MODEL), row),
            pl.BlockSpec((tm, 2 * D_MODEL), row),
            pl.BlockSpec((tm, _LR3), row),
        ],
        out_shape=[
            jax.ShapeDtypeStruct((m, _QKVOG), BF16),
            jax.ShapeDtypeStruct((m, D_MODEL), BF16),
            jax.ShapeDtypeStruct((m, D_MODEL), BF16),
            jax.ShapeDtypeStruct((m, 2 * D_MODEL), BF16),
            jax.ShapeDtypeStruct((m, _LR3), BF16),
        ],
        compiler_params=_params("parallel"),
        name="inproj",
    )(x, g, w_main, w_lr3, b_merge)


def _gla_body(rev, rows, lr3_ref, w2_ref, b2_ref, q_ref, k_ref, v_ref, st_ref):
    L = GLA_CHUNK
    tb = GLA_BLOCK
    nc = tb // L
    g = _dot(lr3_ref[rows, :], w2_ref[...]) + b2_ref[...]
    la = (jnp.minimum(g, 0.0) - jnp.log(1.0 + jnp.exp(-jnp.abs(g)))) * (1.0 / GLA_GATE_NORM)
    ri = lax.broadcasted_iota(jnp.int32, (tb, tb), 0)
    ci = lax.broadcasted_iota(jnp.int32, (tb, tb), 1)
    keep = ((ri // L) == (ci // L)) & ((ci >= ri) if rev else (ci <= ri))
    tri = keep.astype(BF16)
    la_hi = la.astype(BF16)
    la_lo = (la - la_hi.astype(F32)).astype(BF16)
    b = _dot(tri, la_hi) + _dot(tri, la_lo)
    ref_row = L // 2 if rev else L // 2 - 1
    end_row = 0 if rev else L - 1
    b_end_rows = [b[c * L + end_row:c * L + end_row + 1, :] for c in range(nc)]
    b_ref = jnp.concatenate(
        [jnp.broadcast_to(b[c * L + ref_row:c * L + ref_row + 1, :], (L, GLA_QK_WIDTH)) for c in range(nc)], axis=0)
    b_end = jnp.concatenate([jnp.broadcast_to(x, (L, GLA_QK_WIDTH)) for x in b_end_rows], axis=0)
    a_end_rows = [jnp.exp(x) for x in b_end_rows]

    q = q_ref[rows, :].astype(F32)
    k = k_ref[rows, :].astype(F32)
    qe = (q * jnp.exp(b - b_ref)).astype(BF16)
    ke = (k * jnp.exp(b_ref - b)).astype(BF16)
    q_in = (q * jnp.exp(b)).astype(BF16)
    k_out = (k * jnp.exp(b_end - b)).astype(BF16)
    order = range(nc - 1, -1, -1) if rev else range(nc)
    outs = []
    for h in range(GLA_HEADS):
        ks = slice(h * GLA_DK, (h + 1) * GLA_DK)
        v = v_ref[rows, h * GLA_DV:(h + 1) * GLA_DV]
        a = jnp.where(keep, _dot_nt(qe[:, ks], ke[:, ks]), 0.0).astype(BF16)
        o_intra = _dot(a, v)
        st = st_ref[h]
        o_inter = [None] * nc
        for c in order:
            cr = slice(c * L, (c + 1) * L)
            o_inter[c] = _dot_nt(q_in[cr, ks], st.astype(BF16))
            st = st * a_end_rows[c][:, ks] + _dot_tn(v[cr], k_out[cr, ks])
        st_ref[h] = st
        outs.append(o_intra + jnp.concatenate(o_inter, axis=0))
    return outs


def _gla_fwd_kernel(tb, lr3_ref, w2_ref, b2_ref, q_ref, k_ref, v_ref, o_ref, st_ref):
    @pl.when(pl.program_id(1) == 0)
    def _():
        st_ref[...] = jnp.zeros_like(st_ref)
    for sb in range(tb // GLA_BLOCK):
        rows = slice(sb * GLA_BLOCK, (sb + 1) * GLA_BLOCK)
        outs = _gla_body(False, rows, lr3_ref, w2_ref, b2_ref, q_ref, k_ref, v_ref, st_ref)
        for h in range(GLA_HEADS):
            o_ref[rows, h * GLA_DV:(h + 1) * GLA_DV] = outs[h]


def _gla_bwd_kernel(tb, lr3_ref, w2_ref, b2_ref, q_ref, k_ref, v_ref, of_ref, og_ref, gn_ref, o_ref, st_ref):
    @pl.when(pl.program_id(1) == 0)
    def _():
        st_ref[...] = jnp.zeros_like(st_ref)
    for sb in range(tb // GLA_BLOCK - 1, -1, -1):
        rows = slice(sb * GLA_BLOCK, (sb + 1) * GLA_BLOCK)
        outs = _gla_body(True, rows, lr3_ref, w2_ref, b2_ref, q_ref, k_ref, v_ref, st_ref)
        for h in range(GLA_HEADS):
            vs = slice(h * GLA_DV, (h + 1) * GLA_DV)
            o = of_ref[rows, vs] + outs[h]
            o = o * lax.rsqrt(jnp.mean(o * o, axis=-1, keepdims=True) + EPS) * gn_ref[:, vs]
            og = og_ref[rows, vs].astype(F32)
            o_ref[rows, vs] = (o * (og * jax.nn.sigmoid(og))).astype(BF16)


def _gla(qkvog, lr3, w2f, b2f, w2b, b2b, gn, n_seq, seq_len, tb):
    m = qkvog.shape[0]
    nb = seq_len // tb
    grid = (n_seq, nb)
    const = lambda b, n: (0, 0)

    def specs(rowf):
        return [
            pl.BlockSpec((tb, _LR3), lambda b, n: (rowf(b, n), 0)),
            pl.BlockSpec((_LR3, GLA_QK_WIDTH), const),
            pl.BlockSpec((1, GLA_QK_WIDTH), const),
            pl.BlockSpec((tb, GLA_QK_WIDTH), lambda b, n: (rowf(b, n), 0)),
            pl.BlockSpec((tb, GLA_QK_WIDTH), lambda b, n: (rowf(b, n), 1)),
            pl.BlockSpec((tb, GLA_V_WIDTH), lambda b, n: (rowf(b, n), 1)),
        ]

    fwd_row = lambda b, n: b * nb + n
    bwd_row = lambda b, n: b * nb + (nb - 1 - n)
    st = pltpu.VMEM((GLA_HEADS, GLA_DV, GLA_DK), F32)
    o_fwd = pl.pallas_call(
        functools.partial(_gla_fwd_kernel, tb),
        grid=grid,
        in_specs=specs(fwd_row),
        out_specs=pl.BlockSpec((tb, GLA_V_WIDTH), lambda b, n: (fwd_row(b, n), 0)),
        out_shape=jax.ShapeDtypeStruct((m, GLA_V_WIDTH), F32),
        scratch_shapes=[st],
        compiler_params=_params("parallel", "arbitrary"),
        name="gla_fwd",
    )(lr3, w2f, b2f, qkvog, qkvog, qkvog)
    return pl.pallas_call(
        functools.partial(_gla_bwd_kernel, tb),
        grid=grid,
        in_specs=specs(bwd_row) + [
            pl.BlockSpec((tb, GLA_V_WIDTH), lambda b, n: (bwd_row(b, n), 0)),
            pl.BlockSpec((tb, GLA_V_WIDTH), lambda b, n: (bwd_row(b, n), 2)),
            pl.BlockSpec((1, GLA_V_WIDTH), const),
        ],
        out_specs=pl.BlockSpec((tb, GLA_V_WIDTH), lambda b, n: (bwd_row(b, n), 0)),
        out_shape=jax.ShapeDtypeStruct((m, GLA_V_WIDTH), BF16),
        scratch_shapes=[st],
        compiler_params=_params("parallel", "arbitrary"),
        name="gla_bwd",
    )(lr3, w2b, b2b, qkvog, qkvog, qkvog, o_fwd, qkvog, gn)


_HALO = 16
_MERGE_ROWS = 256


def _merge_kernel(tm, seq_len, x_ref, og_ref, cb_ref, u_ref, up_ref, un_ref, gates_ref, cw_ref,
                  wa_ref, wb_ref, wo_ref, g2_ref, wr_ref, x1_ref, h2_ref, aff_ref):
    i = pl.program_id(0)
    u = u_ref[...].astype(F32)
    first = (i * tm) % seq_len == 0
    last = ((i + 1) * tm) % seq_len == 0
    prev_row = jnp.where(first, 0.0, up_ref[_HALO - 1:_HALO, :].astype(F32))
    next_row = jnp.where(last, 0.0, un_ref[0:1, :].astype(F32))
    r = lax.broadcasted_iota(jnp.int32, (tm, D_MODEL), 0)
    um1 = jnp.where(r == 0, prev_row, pltpu.roll(u, 1, 0))
    up1 = jnp.where(r == tm - 1, next_row, pltpu.roll(u, tm - 1, 0))
    hc = cw_ref[0:1, :] * um1 + cw_ref[1:2, :] * u + cw_ref[2:3, :] * up1
    cbh = (cb_ref[...].astype(F32) * hc).astype(BF16)
    for r0 in range(0, tm, _MERGE_ROWS):
        rows = slice(r0, r0 + _MERGE_ROWS)
        y_b = _dot(cbh[rows], wb_ref[...])
        y_a = _dot(og_ref[rows, :], wa_ref[...])
        mix = (gates_ref[rows, 0:D_MODEL].astype(F32) * y_a
               + gates_ref[rows, D_MODEL:2 * D_MODEL].astype(F32) * y_b)
        x1 = x_ref[rows, :] + _dot(mix.astype(BF16), wo_ref[...])
        x1_ref[rows, :] = x1
        h2 = (x1 * lax.rsqrt(jnp.mean(x1 * x1, axis=-1, keepdims=True) + EPS) * g2_ref[...]).astype(BF16)
        h2_ref[rows, :] = h2
        logits = _dot(h2, wr_ref[...])
        e = jnp.exp(logits - jnp.max(logits, axis=-1, keepdims=True))
        aff_ref[rows, :] = e / jnp.sum(e, axis=-1, keepdims=True)


def _merge(x, og, cb, u, gates, conv_w, wa, wb, wo, g2, wr, seq_len, tm):
    m = x.shape[0]
    const = lambda i: (0, 0)
    row = lambda i: (i, 0)
    hb = tm // _HALO
    nhb = m // _HALO
    sq = lambda: pl.BlockSpec((D_MODEL, D_MODEL), const)
    return pl.pallas_call(
        functools.partial(_merge_kernel, tm, seq_len),
        grid=(m // tm,),
        in_specs=[
            pl.BlockSpec((tm, D_MODEL), row),
            pl.BlockSpec((tm, D_MODEL), row),
            pl.BlockSpec((tm, D_MODEL), row),
            pl.BlockSpec((tm, D_MODEL), row),
            pl.BlockSpec((_HALO, D_MODEL), lambda i: (jnp.maximum(i * hb - 1, 0), 0)),
            pl.BlockSpec((_HALO, D_MODEL), lambda i: (jnp.minimum((i + 1) * hb, nhb - 1), 0)),
            pl.BlockSpec((tm, 2 * D_MODEL), row),
            pl.BlockSpec((3, D_MODEL), const),
            sq(), sq(), sq(),
            pl.BlockSpec((1, D_MODEL), const),
            pl.BlockSpec((D_MODEL, N_EXPERTS), const),
        ],
        out_specs=[
            pl.BlockSpec((tm, D_MODEL), row),
            pl.BlockSpec((tm, D_MODEL), row),
            pl.BlockSpec((tm, N_EXPERTS), row),
        ],
        out_shape=[
            jax.ShapeDtypeStruct((m, D_MODEL), F32),
            jax.ShapeDtypeStruct((m, D_MODEL), BF16),
            jax.ShapeDtypeStruct((m, N_EXPERTS), F32),
        ],
        compiler_params=_params("parallel"),
        name="merge",
    )(x, og, cb, u, u, u, gates, conv_w, wa, wb, wo, g2, wr)


RB = 256
SUB = 64
ROW_TILE = 16
WIN = SUB + ROW_TILE
NARROW = SUB
_XW = D_MODEL + 128
_IDX_BITS = 30
_MIN_NORMAL_BITS = 0x00800000


def _route_kernel(cap, n_tok_bits, aff_ref, thr_ref, nxt_ref, jb_ref):
    aff = aff_ref[...]
    idx = lax.broadcasted_iota(jnp.int32, aff.shape, 1)
    capf = jnp.float32(cap)

    def count(mask):
        return jnp.sum(mask.astype(F32), axis=1, keepdims=True)

    def as_f32(bits):
        return lax.bitcast_convert_type(bits, F32)

    def value_step(i, prefix):
        cand = prefix | jnp.left_shift(jnp.int32(1), _IDX_BITS - 1 - i)
        return jnp.where(count(aff >= as_f32(cand)) >= capf, cand, prefix)

    thr_bits = lax.fori_loop(0, _IDX_BITS, value_step, jnp.zeros((N_EXPERTS, 1), jnp.int32))
    thr = as_f32(thr_bits)
    nxt = as_f32(jnp.maximum(thr_bits + 1, _MIN_NORMAL_BITS))
    need = capf - count(aff >= nxt)
    tie = (aff >= thr) & (aff < nxt)

    def index_step(i, j):
        cand = j | jnp.left_shift(jnp.int32(1), n_tok_bits - 1 - i)
        return jnp.where(count(tie & (idx < cand)) < need, cand, j)

    thr_ref[...] = thr
    nxt_ref[...] = nxt
    jb_ref[...] = lax.fori_loop(0, n_tok_bits, index_step, jnp.zeros((N_EXPERTS, 1), jnp.int32))


def _route(aff_t, cap):
    n_tok = aff_t.shape[1]
    val = jax.ShapeDtypeStruct((N_EXPERTS, 1), F32)
    return pl.pallas_call(
        functools.partial(_route_kernel, cap, max(1, (n_tok - 1).bit_length())),
        out_shape=[val, val, jax.ShapeDtypeStruct((N_EXPERTS, 1), jnp.int32)],
        compiler_params=pltpu.CompilerParams(vmem_limit_bytes=VMEM_LIMIT_BYTES),
        name="route",
    )(aff_t)


def _selected(aff, tok, thr, nxt, jb):
    return (aff >= nxt) | ((aff >= thr) & (tok <= jb))


def _offsets_kernel(to, aff_ref, thr_ref, nxt_ref, jb_ref, offs_ref, carry_ref):
    i = pl.program_id(0)

    @pl.when(i == 0)
    def _():
        carry_ref[...] = jnp.zeros_like(carry_ref)

    aff = aff_ref[...]
    tok = i * to + lax.broadcasted_iota(jnp.int32, aff.shape, 0)
    sel = _selected(aff, tok, thr_ref[...], nxt_ref[...], jb_ref[...]).astype(BF16)
    ns = to // SUB
    grp = (lax.broadcasted_iota(jnp.int32, (ns, to), 1) // SUB == lax.broadcasted_iota(jnp.int32, (ns, to), 0))
    cnt = _dot(grp.astype(BF16), sel)
    before = (lax.broadcasted_iota(jnp.int32, (ns, ns), 1) < lax.broadcasted_iota(jnp.int32, (ns, ns), 0))
    offs_ref[...] = (carry_ref[...] + _dot(before.astype(BF16), cnt.astype(BF16))).astype(jnp.int32)
    carry_ref[...] += jnp.sum(cnt, axis=0, keepdims=True)


def _offsets(aff, thr_row, nxt_row, jb_row, to):
    m = aff.shape[0]
    const = lambda i: (0, 0)
    row = lambda: pl.BlockSpec((1, N_EXPERTS), const)
    return pl.pallas_call(
        functools.partial(_offsets_kernel, to),
        grid=(m // to,),
        in_specs=[pl.BlockSpec((to, N_EXPERTS), lambda i: (i, 0)), row(), row(), row()],
        out_specs=pl.BlockSpec((to // SUB, N_EXPERTS), lambda i: (i, 0)),
        out_shape=jax.ShapeDtypeStruct((m // SUB, N_EXPERTS), jnp.int32),
        scratch_shapes=[pltpu.VMEM((1, N_EXPERTS), F32)],
        compiler_params=_params("arbitrary"),
        name="offsets",
    )(aff, thr_row, nxt_row, jb_row)


def _block_selection(blk, aff_t_ref, thr_ref, nxt_ref, jb_ref):
    aff = aff_t_ref[...]
    tok = blk * RB + lax.broadcasted_iota(jnp.int32, aff.shape, 1)
    return _selected(aff, tok, thr_ref[...], nxt_ref[...], jb_ref[...])


def _floor_tile(x):
    return pl.multiple_of((x // ROW_TILE) * ROW_TILE, ROW_TILE)


def _slot_onehot(sel, shift, width):
    self = sel.astype(F32)
    before = (lax.broadcasted_iota(jnp.int32, (RB, RB), 0) < lax.broadcasted_iota(jnp.int32, (RB, RB), 1))
    rank = _dot(self.astype(BF16), before.astype(BF16)).astype(jnp.int32)
    slot = lax.broadcasted_iota(jnp.int32, (width, RB), 0)
    rows = [jnp.where(rank[e:e + 1, :] + shift[e] == slot, self[e:e + 1, :], 0.0) for e in range(N_EXPERTS)]
    return jnp.concatenate(rows, axis=0).astype(BF16)


def _block_counts(offs_ref, blk):
    per = RB // SUB
    start = [offs_ref[blk * per * N_EXPERTS + e] for e in range(N_EXPERTS)]
    end = [offs_ref[(blk + 1) * per * N_EXPERTS + e] for e in range(N_EXPERTS)]
    most = functools.reduce(jnp.maximum, [b - a for a, b in zip(start, end)])
    span = functools.reduce(jnp.maximum, [b - _floor_tile(a) for a, b in zip(start, end)])
    return most <= SUB, span <= NARROW


def _sub_mask(sub):
    return lax.broadcasted_iota(jnp.int32, (N_EXPERTS, RB), 1) // SUB == sub


def _gather_kernel(cap, nb, offs_ref, aff_t_ref, aff_ref, thr_ref, nxt_ref, jb_ref, h2_ref, xe_ref,
                   stage_ref, carry_ref, sem_ref):
    blk = pl.program_id(0)
    par = blk % 2
    per = RB // SUB

    def copy(parity, e, dst_row, width):
        return pltpu.make_async_copy(stage_ref.at[parity, pl.ds(e * width, width)],
                                     xe_ref.at[e, pl.ds(dst_row, width)], sem_ref.at[parity, e])

    def wait_all(parity, width):
        for e in range(N_EXPERTS):
            copy(parity, e, 0, width).wait()

    @pl.when(blk == 0)
    def _():
        carry_ref[...] = jnp.zeros_like(carry_ref)
        stage_ref[1] = jnp.zeros(stage_ref.shape[1:], BF16)
        for e in range(N_EXPERTS):
            copy(1, e, cap, WIN).start()
        wait_all(1, WIN)

    sel = _block_selection(blk, aff_t_ref, thr_ref, nxt_ref, jb_ref)
    aff = aff_ref[...]
    hi = aff.astype(BF16)
    r1 = aff - hi.astype(F32)
    mid = r1.astype(BF16)
    lo = (r1 - mid.astype(F32)).astype(BF16)
    er = lax.broadcasted_iota(jnp.int32, (N_EXPERTS, 128), 0)
    ec = lax.broadcasted_iota(jnp.int32, (N_EXPERTS, 128), 1)
    g3 = (_dot(hi, (ec == 3 * er).astype(BF16)) + _dot(mid, (ec == 3 * er + 1).astype(BF16))
          + _dot(lo, (ec == 3 * er + 2).astype(BF16)))
    src = jnp.concatenate([h2_ref[...], g3.astype(BF16)], axis=1)

    fits, narrow = _block_counts(offs_ref, blk)
    wide = fits & jnp.logical_not(narrow)
    _, prev_narrow = _block_counts(offs_ref, jnp.maximum(blk - 1, 0))

    def emit(mask, seg_lo, seg_hi, first, width):
        lo_rows = [offs_ref[seg_lo * N_EXPERTS + e] for e in range(N_EXPERTS)]
        hi_rows = [offs_ref[seg_hi * N_EXPERTS + e] for e in range(N_EXPERTS)]
        base = [_floor_tile(r) for r in lo_rows]
        onehot = _slot_onehot(mask, [r - b for r, b in zip(lo_rows, base)], width)
        stage_ref[par, pl.ds(0, N_EXPERTS * width)] = _dot(onehot, src).astype(BF16)
        for e in range(N_EXPERTS):
            head = pl.ds(e * width, ROW_TILE)
            stage_ref[par, head] = stage_ref[par, head] + carry_ref[e]
            nxt = _floor_tile(hi_rows[e]) - base[e]
            carry_ref[e] = stage_ref[par, pl.ds(pl.multiple_of(e * width + nxt, ROW_TILE), ROW_TILE)]
        if first:
            @pl.when((blk > 0) & prev_narrow)
            def _():
                wait_all(1 - par, NARROW)

            @pl.when((blk > 0) & jnp.logical_not(prev_narrow))
            def _():
                wait_all(1 - par, WIN)
        for e in range(N_EXPERTS):
            copy(par, e, base[e], width).start()

    @pl.when(narrow)
    def _():
        emit(sel, blk * per, (blk + 1) * per, True, NARROW)

    @pl.when(wide)
    def _():
        emit(sel, blk * per, (blk + 1) * per, True, WIN)

    @pl.when(jnp.logical_not(fits))
    def _():
        for sub in range(per):
            if sub > 0:
                wait_all(par, WIN)
            emit(sel & _sub_mask(sub), blk * per + sub, blk * per + sub + 1, sub == 0, WIN)

    @pl.when((blk == nb - 1) & narrow)
    def _():
        wait_all(par, NARROW)

    @pl.when((blk == nb - 1) & jnp.logical_not(narrow))
    def _():
        wait_all(par, WIN)


def _gather(offs, aff_t, aff, thr, nxt, jb, h2, cap):
    m = h2.shape[0]
    nb = m // RB
    col = lambda i, o: (0, 0)
    return pl.pallas_call(
        functools.partial(_gather_kernel, cap, nb),
        grid_spec=pltpu.PrefetchScalarGridSpec(
            num_scalar_prefetch=1,
            grid=(nb,),
            in_specs=[
                pl.BlockSpec((N_EXPERTS, RB), lambda i, o: (0, i)),
                pl.BlockSpec((RB, N_EXPERTS), lambda i, o: (i, 0)),
                pl.BlockSpec((N_EXPERTS, 1), col), pl.BlockSpec((N_EXPERTS, 1), col),
                pl.BlockSpec((N_EXPERTS, 1), col),
                pl.BlockSpec((RB, D_MODEL), lambda i, o: (i, 0)),
            ],
            out_specs=pl.BlockSpec(memory_space=pl.ANY),
            scratch_shapes=[pltpu.VMEM((2, N_EXPERTS * WIN, _XW), BF16),
                            pltpu.VMEM((N_EXPERTS, ROW_TILE, _XW), BF16),
                            pltpu.SemaphoreType.DMA((2, N_EXPERTS))],
        ),
        out_shape=jax.ShapeDtypeStruct((N_EXPERTS, cap + WIN, _XW), BF16),
        compiler_params=_params("arbitrary"),
        name="gather",
    )(offs, aff_t, aff, thr, nxt, jb, h2)


def _ffn_kernel(x_ref, wg_ref, wu_ref, wd_ref, o_ref, hid_ref):
    e = pl.program_id(0)
    x = x_ref[0, :, 0:D_MODEL]
    gcols = x_ref[0, :, D_MODEL:_XW].astype(F32)
    lane = lax.broadcasted_iota(jnp.int32, gcols.shape, 1)
    gate = jnp.sum(jnp.where((lane >= 3 * e) & (lane < 3 * e + 3), gcols, 0.0), axis=-1, keepdims=True)
    for f0 in range(0, D_EXPERT, 1024):
        g = _dot(x, wg_ref[0, :, f0:f0 + 1024])
        up = _dot(x, wu_ref[0, :, f0:f0 + 1024])
        hid_ref[:, f0:f0 + 1024] = (g * jax.nn.sigmoid(g) * up).astype(BF16)
    o_ref[0] = (_dot(hid_ref[...], wd_ref[0]) * gate).astype(BF16)


def _ffn(xe, cap, wg, wu, wd, tc):
    e = xe.shape[0]
    return pl.pallas_call(
        _ffn_kernel,
        grid=(e, cap // tc),
        in_specs=[
            pl.BlockSpec((1, tc, _XW), lambda e, j: (e, j, 0)),
            pl.BlockSpec((1, D_MODEL, D_EXPERT), lambda e, j: (e, 0, 0)),
            pl.BlockSpec((1, D_MODEL, D_EXPERT), lambda e, j: (e, 0, 0)),
            pl.BlockSpec((1, D_EXPERT, D_MODEL), lambda e, j: (e, 0, 0)),
        ],
        out_specs=pl.BlockSpec((1, tc, D_MODEL), lambda e, j: (e, j, 0)),
        out_shape=jax.ShapeDtypeStruct((e, cap, D_MODEL), BF16),
        scratch_shapes=[pltpu.VMEM((tc, D_EXPERT), BF16)],
        compiler_params=_params("parallel", "parallel"),
        name="experts",
    )(xe, wg, wu, wd)


def _combine_kernel(cap, nb, offs_ref, aff_t_ref, thr_ref, nxt_ref, jb_ref, x1_ref, g_ref, ye_ref, o_ref,
                    win_ref, sem_ref):
    blk = pl.program_id(0)
    par = blk % 2
    per = RB // SUB

    def window(row, width):
        first = pl.multiple_of(jnp.minimum(_floor_tile(row), cap - width), ROW_TILE)
        return first, row - first

    def copy(parity, e, first, width):
        return pltpu.make_async_copy(ye_ref.at[e, pl.ds(first, width)],
                                     win_ref.at[parity, pl.ds(e * width, width)], sem_ref.at[parity, e])

    def fetch(parity, rows, width):
        for e in range(N_EXPERTS):
            copy(parity, e, window(rows[e], width)[0], width).start()

    def wait_all(parity, width):
        for e in range(N_EXPERTS):
            copy(parity, e, 0, width).wait()

    def block_rows(b, sub):
        return [offs_ref[(b * per + sub) * N_EXPERTS + e] for e in range(N_EXPERTS)]

    def fetch_block(parity, b):
        _, b_narrow = _block_counts(offs_ref, b)

        @pl.when(b_narrow)
        def _():
            fetch(parity, block_rows(b, 0), NARROW)

        @pl.when(jnp.logical_not(b_narrow))
        def _():
            fetch(parity, block_rows(b, 0), WIN)

    @pl.when(blk == 0)
    def _():
        fetch_block(par, blk)

    @pl.when(blk + 1 < nb)
    def _():
        fetch_block(1 - par, jnp.minimum(blk + 1, nb - 1))

    sel = _block_selection(blk, aff_t_ref, thr_ref, nxt_ref, jb_ref)
    fits, narrow = _block_counts(offs_ref, blk)
    start = block_rows(blk, 0)

    def contribution(mask, rows, width):
        onehot = _slot_onehot(mask, [window(r, width)[1] for r in rows], width)
        return _dot_tn(onehot, win_ref[par, pl.ds(0, N_EXPERTS * width)])

    def finish(y):
        x = x1_ref[...] + y
        o_ref[...] = x * lax.rsqrt(jnp.mean(x * x, axis=-1, keepdims=True) + EPS) * g_ref[...]

    @pl.when(narrow)
    def _():
        wait_all(par, NARROW)
        finish(contribution(sel, start, NARROW))

    @pl.when(fits & jnp.logical_not(narrow))
    def _():
        wait_all(par, WIN)
        finish(contribution(sel, start, WIN))

    @pl.when(jnp.logical_not(fits))
    def _():
        wait_all(par, WIN)
        y = contribution(sel & _sub_mask(0), start, WIN)
        for sub in range(1, per):
            rows = block_rows(blk, sub)
            fetch(par, rows, WIN)
            wait_all(par, WIN)
            y = y + contribution(sel & _sub_mask(sub), rows, WIN)
        finish(y)


def _combine(offs, aff_t, thr, nxt, jb, x1, g, ye, cap):
    m = x1.shape[0]
    nb = m // RB
    col = lambda i, o: (0, 0)
    return pl.pallas_call(
        functools.partial(_combine_kernel, cap, nb),
        grid_spec=pltpu.PrefetchScalarGridSpec(
            num_scalar_prefetch=1,
            grid=(nb,),
            in_specs=[
                pl.BlockSpec((N_EXPERTS, RB), lambda i, o: (0, i)),
                pl.BlockSpec((N_EXPERTS, 1), col), pl.BlockSpec((N_EXPERTS, 1), col),
                pl.BlockSpec((N_EXPERTS, 1), col),
                pl.BlockSpec((RB, D_MODEL), lambda i, o: (i, 0)),
                pl.BlockSpec((1, D_MODEL), col),
                pl.BlockSpec(memory_space=pl.ANY),
            ],
            out_specs=pl.BlockSpec((RB, D_MODEL), lambda i, o: (i, 0)),
            scratch_shapes=[pltpu.VMEM((2, N_EXPERTS * WIN, D_MODEL), BF16),
                            pltpu.SemaphoreType.DMA((2, N_EXPERTS))],
        ),
        out_shape=jax.ShapeDtypeStruct((m, D_MODEL), F32),
        compiler_params=_params("arbitrary"),
        name="combine",
    )(offs, aff_t, thr, nxt, jb, x1, g, ye)


def _stack_w2(w2, lo):
    w = jnp.pad(w2, ((lo, _LR - GLA_GATE_RANK - lo), (0, 0)))
    hi = w.astype(BF16)
    lo_part = (w - hi.astype(F32)).astype(BF16)
    return jnp.concatenate([hi, hi, lo_part, jnp.zeros_like(hi)], axis=0)


_TILE_INPROJ = 512
_TILE_GLA = 512
_TILE_MERGE = 512
_TILE_OFFSETS = 2048
_TILE_EXPERTS = 1024


def _trunk(x3, wts):
    n_seq, seq_len, _ = x3.shape
    m = n_seq * seq_len
    cap = max(1, EC_CAPACITY_FACTOR * m // N_EXPERTS)
    assert seq_len % _TILE_GLA == 0 and seq_len % _TILE_MERGE == 0 and m % _TILE_OFFSETS == 0
    assert cap >= WIN and cap % _TILE_EXPERTS == 0
    x = x3.reshape(m, D_MODEL)
    qkvog, cb, u, gates, lr3 = _inproj(x, wts["norm_mix_g"], wts["w_main"], wts["w_lr3"], wts["b_merge"],
                                       tm=_TILE_INPROJ)
    og = _gla(qkvog, lr3, wts["w2f"], wts["b2f"], wts["w2b"], wts["b2b"], wts["gla_norm_g"],
              n_seq, seq_len, tb=_TILE_GLA)
    x1, h2, aff = _merge(x, og, cb, u, gates, wts["conv_w"], wts["w_gla_out"], wts["w_conv_out"], wts["w_out"],
                         wts["norm_ffn_g"], wts["w_router"], seq_len, tm=_TILE_MERGE)
    aff_t = aff.T
    thr, nxt, jb = _route(aff_t, cap)
    row = lambda a: a.reshape(1, N_EXPERTS)
    offs = _offsets(aff, row(thr), row(nxt), row(jb), to=_TILE_OFFSETS)
    offs = jnp.concatenate([offs, jnp.full((1, N_EXPERTS), cap, jnp.int32)], axis=0).reshape(-1)
    xe = _gather(offs, aff_t, aff, thr, nxt, jb, h2, cap)
    ye = _ffn(xe, cap, wts["w_exp_gate"], wts["w_exp_up"], wts["w_exp_down"], tc=_TILE_EXPERTS)
    out = _combine(offs, aff_t, thr, nxt, jb, x1, wts["norm_final_g"], ye, cap)
    return out.reshape(n_seq, seq_len, D_MODEL)


def kernel(x_prompt, x_sample, norm_mix_g, w_in, w_gk2_fwd, b_gk_fwd, w_gk2_bwd, b_gk_bwd, gla_norm_g, w_gla_out,
           conv_w, w_conv_out, b_merge, w_out, norm_ffn_g, w_router, w_exp_gate, w_exp_up, w_exp_down, norm_final_g):
    w = w_in[0]
    lr0 = _QKVOG
    w_lr = w[:, lr0:lr0 + _LR].astype(BF16)
    wts = {
        "norm_mix_g": norm_mix_g[0][None, :],
        "w_main": jnp.concatenate([w[:, :lr0], w[:, lr0 + _LR:]], axis=1).astype(BF16),
        "w_lr3": jnp.concatenate([w_lr, w_lr, w_lr, jnp.zeros_like(w_lr)], axis=1),
        "w2f": _stack_w2(w_gk2_fwd[0], 0),
        "w2b": _stack_w2(w_gk2_bwd[0], GLA_GATE_RANK),
        "b2f": b_gk_fwd[0][None, :],
        "b2b": b_gk_bwd[0][None, :],
        "gla_norm_g": gla_norm_g[0].reshape(1, GLA_V_WIDTH),
        "w_gla_out": w_gla_out[0].astype(BF16),
        "conv_w": conv_w[0],
        "w_conv_out": w_conv_out[0].astype(BF16),
        "b_merge": b_merge[0][None, :],
        "w_out": w_out[0].astype(BF16),
        "norm_ffn_g": norm_ffn_g[0][None, :],
        "w_router": w_router[0].astype(BF16),
        "w_exp_gate": w_exp_gate[0].astype(BF16),
        "w_exp_up": w_exp_up[0].astype(BF16),
        "w_exp_down": w_exp_down[0].astype(BF16),
        "norm_final_g": norm_final_g[None, :],
    }
    return (_trunk(x_prompt, wts), _trunk(x_sample, wts))
```

```python
import functools

import jax
import jax.numpy as jnp
from jax import lax
from jax.experimental import pallas as pl
from jax.experimental.pallas import tpu as pltpu

D_MODEL = 1024
GLA_HEADS = 4
GLA_DK = 128
GLA_DV = 256
GLA_QK_WIDTH = GLA_HEADS * GLA_DK
GLA_V_WIDTH = GLA_HEADS * GLA_DV
GLA_GATE_RANK = 16
GLA_GATE_NORM = 16.0
GLA_CHUNK = 64
GLA_BLOCK = 256
N_EXPERTS = 16
EC_CAPACITY_FACTOR = 2
D_EXPERT = 2 * D_MODEL
EPS = 1e-6

BF16 = jnp.bfloat16
F32 = jnp.float32

VMEM_LIMIT_BYTES = 56 * 1024 * 1024

_QKVOG = 2 * GLA_QK_WIDTH + 2 * GLA_V_WIDTH
_LR = 2 * GLA_GATE_RANK
_LR3 = 128


def _dot(a, b):
    return jnp.dot(a, b, preferred_element_type=F32)


def _dot_nt(a, b):
    return lax.dot_general(a, b, (((1,), (1,)), ((), ())), preferred_element_type=F32)


def _dot_tn(a, b):
    return lax.dot_general(a, b, (((0,), (0,)), ((), ())), preferred_element_type=F32)


def _params(*sem):
    return pltpu.CompilerParams(dimension_semantics=sem, vmem_limit_bytes=VMEM_LIMIT_BYTES)


def _inproj_kernel(x_ref, g_ref, w_ref, wlr_ref, bm_ref, qkvog_ref, cb_ref, u_ref, gates_ref, lr3_ref):
    x = x_ref[...]
    h = x * lax.rsqrt(jnp.mean(x * x, axis=-1, keepdims=True) + EPS) * g_ref[...]
    hb = h.astype(BF16)
    lr = _dot(hb, wlr_ref[...])
    hi = lr.astype(BF16)
    lo = (lr - hi.astype(F32)).astype(BF16)
    lane = lax.broadcasted_iota(jnp.int32, lr.shape, 1)
    lr3_ref[...] = jnp.where((lane >= _LR) & (lane < 2 * _LR), lo, hi)
    q = _dot(hb, w_ref[:, 0:GLA_QK_WIDTH]) * (GLA_DK ** -0.5)
    qkvog_ref[:, 0:GLA_QK_WIDTH] = q.astype(BF16)
    for c0 in range(GLA_QK_WIDTH, _QKVOG, 512):
        qkvog_ref[:, c0:c0 + 512] = _dot(hb, w_ref[:, c0:c0 + 512]).astype(BF16)
    o = _QKVOG
    for c0 in range(0, D_MODEL, 512):
        cb_ref[:, c0:c0 + 512] = _dot(hb, w_ref[:, o + c0:o + c0 + 512]).astype(BF16)
    for c0 in range(0, D_MODEL, 512):
        cc = _dot(hb, w_ref[:, o + D_MODEL + c0:o + D_MODEL + c0 + 512])
        cx = _dot(hb, w_ref[:, o + 2 * D_MODEL + c0:o + 2 * D_MODEL + c0 + 512])
        u_ref[:, c0:c0 + 512] = (cc * cx).astype(BF16)
    o = _QKVOG + 3 * D_MODEL
    for c0 in range(0, 2 * D_MODEL, 512):
        gm = _dot(hb, w_ref[:, o + c0:o + c0 + 512]) + bm_ref[:, c0:c0 + 512]
        gates_ref[:, c0:c0 + 512] = jax.nn.sigmoid(gm).astype(BF16)


def _inproj(x, g, w_main, w_lr3, b_merge, tm):
    m = x.shape[0]
    ncols = w_main.shape[1]
    const = lambda i: (0, 0)
    row = lambda i: (i, 0)
    return pl.pallas_call(
        _inproj_kernel,
        grid=(m // tm,),
        in_specs=[
            pl.BlockSpec((tm, D_MODEL), row),
            pl.BlockSpec((1, D_MODEL), const),
            pl.BlockSpec((D_MODEL, ncols), const, pipeline_mode=pl.Buffered(1)),
            pl.BlockSpec((D_MODEL, _LR3), const),
            pl.BlockSpec((1, 2 * D_MODEL), const),
        ],
        out_specs=[
            pl.BlockSpec((tm, _QKVOG), row),
            pl.BlockSpec((tm, D_MODEL), row),
            pl.BlockSpec((tm, D_MODEL), row),
            pl.BlockSpec((tm, 2 * D_MODEL), row),
            pl.BlockSpec((tm, _LR3), row),
        ],
        out_shape=[
            jax.ShapeDtypeStruct((m, _QKVOG), BF16),
            jax.ShapeDtypeStruct((m, D_MODEL), BF16),
            jax.ShapeDtypeStruct((m, D_MODEL), BF16),
            jax.ShapeDtypeStruct((m, 2 * D_MODEL), BF16),
            jax.ShapeDtypeStruct((m, _LR3), BF16),
        ],
        compiler_params=_params("parallel"),
        name="inproj",
    )(x, g, w_main, w_lr3, b_merge)


def _gla_body(rev, rows, lr3_ref, w2_ref, b2_ref, q_ref, k_ref, v_ref, st_ref):
    L = GLA_CHUNK
    tb = GLA_BLOCK
    nc = tb // L
    g = _dot(lr3_ref[rows, :], w2_ref[...]) + b2_ref[...]
    la = (jnp.minimum(g, 0.0) - jnp.log(1.0 + jnp.exp(-jnp.abs(g)))) * (1.0 / GLA_GATE_NORM)
    ri = lax.broadcasted_iota(jnp.int32, (tb, tb), 0)
    ci = lax.broadcasted_iota(jnp.int32, (tb, tb), 1)
    keep = ((ri // L) == (ci // L)) & ((ci >= ri) if rev else (ci <= ri))
    tri = keep.astype(BF16)
    la_hi = la.astype(BF16)
    la_lo = (la - la_hi.astype(F32)).astype(BF16)
    b = _dot(tri, la_hi) + _dot(tri, la_lo)
    ref_row = L // 2 if rev else L // 2 - 1
    end_row = 0 if rev else L - 1
    b_end_rows = [b[c * L + end_row:c * L + end_row + 1, :] for c in range(nc)]
    b_ref = jnp.concatenate(
        [jnp.broadcast_to(b[c * L + ref_row:c * L + ref_row + 1, :], (L, GLA_QK_WIDTH)) for c in range(nc)], axis=0)
    b_end = jnp.concatenate([jnp.broadcast_to(x, (L, GLA_QK_WIDTH)) for x in b_end_rows], axis=0)
    a_end_rows = [jnp.exp(x) for x in b_end_rows]

    q = q_ref[rows, :].astype(F32)
    k = k_ref[rows, :].astype(F32)
    qe = (q * jnp.exp(b - b_ref)).astype(BF16)
    ke = (k * jnp.exp(b_ref - b)).astype(BF16)
    q_in = (q * jnp.exp(b)).astype(BF16)
    k_out = (k * jnp.exp(b_end - b)).astype(BF16)
    order = range(nc - 1, -1, -1) if rev else range(nc)
    outs = []
    for h in range(GLA_HEADS):
        ks = slice(h * GLA_DK, (h + 1) * GLA_DK)
        v = v_ref[rows, h * GLA_DV:(h + 1) * GLA_DV]
        a = jnp.where(keep, _dot_nt(qe[:, ks], ke[:, ks]), 0.0).astype(BF16)
        o_intra = _dot(a, v)
        st = st_ref[h]
        o_inter = [None] * nc
        for c in order:
            cr = slice(c * L, (c + 1) * L)
            o_inter[c] = _dot_nt(q_in[cr, ks], st.astype(BF16))
            st = st * a_end_rows[c][:, ks] + _dot_tn(v[cr], k_out[cr, ks])
        st_ref[h] = st
        outs.append(o_intra + jnp.concatenate(o_inter, axis=0))
    return outs


def _gla_fwd_kernel(tb, lr3_ref, w2_ref, b2_ref, q_ref, k_ref, v_ref, o_ref, st_ref):
    @pl.when(pl.program_id(1) == 0)
    def _():
        st_ref[...] = jnp.zeros_like(st_ref)
    for sb in range(tb // GLA_BLOCK):
        rows = slice(sb * GLA_BLOCK, (sb + 1) * GLA_BLOCK)
        outs = _gla_body(False, rows, lr3_ref, w2_ref, b2_ref, q_ref, k_ref, v_ref, st_ref)
        for h in range(GLA_HEADS):
            o_ref[rows, h * GLA_DV:(h + 1) * GLA_DV] = outs[h]


def _gla_bwd_kernel(tb, lr3_ref, w2_ref, b2_ref, q_ref, k_ref, v_ref, of_ref, og_ref, gn_ref, o_ref, st_ref):
    @pl.when(pl.program_id(1) == 0)
    def _():
        st_ref[...] = jnp.zeros_like(st_ref)
    for sb in range(tb // GLA_BLOCK - 1, -1, -1):
        rows = slice(sb * GLA_BLOCK, (sb + 1) * GLA_BLOCK)
        outs = _gla_body(True, rows, lr3_ref, w2_ref, b2_ref, q_ref, k_ref, v_ref, st_ref)
        for h in range(GLA_HEADS):
            vs = slice(h * GLA_DV, (h + 1) * GLA_DV)
            o = of_ref[rows, vs] + outs[h]
            o = o * lax.rsqrt(jnp.mean(o * o, axis=-1, keepdims=True) + EPS) * gn_ref[:, vs]
            og = og_ref[rows, vs].astype(F32)
            o_ref[rows, vs] = (o * (og * jax.nn.sigmoid(og))).astype(BF16)


def _gla(qkvog, lr3, w2f, b2f, w2b, b2b, gn, n_seq, seq_len, tb):
    m = qkvog.shape[0]
    nb = seq_len // tb
    grid = (n_seq, nb)
    const = lambda b, n: (0, 0)

    def specs(rowf):
        return [
            pl.BlockSpec((tb, _LR3), lambda b, n: (rowf(b, n), 0)),
            pl.BlockSpec((_LR3, GLA_QK_WIDTH), const),
            pl.BlockSpec((1, GLA_QK_WIDTH), const),
            pl.BlockSpec((tb, GLA_QK_WIDTH), lambda b, n: (rowf(b, n), 0)),
            pl.BlockSpec((tb, GLA_QK_WIDTH), lambda b, n: (rowf(b, n), 1)),
            pl.BlockSpec((tb, GLA_V_WIDTH), lambda b, n: (rowf(b, n), 1)),
        ]

    fwd_row = lambda b, n: b * nb + n
    bwd_row = lambda b, n: b * nb + (nb - 1 - n)
    st = pltpu.VMEM((GLA_HEADS, GLA_DV, GLA_DK), F32)
    o_fwd = pl.pallas_call(
        functools.partial(_gla_fwd_kernel, tb),
        grid=grid,
        in_specs=specs(fwd_row),
        out_specs=pl.BlockSpec((tb, GLA_V_WIDTH), lambda b, n: (fwd_row(b, n), 0)),
        out_shape=jax.ShapeDtypeStruct((m, GLA_V_WIDTH), F32),
        scratch_shapes=[st],
        compiler_params=_params("parallel", "arbitrary"),
        name="gla_fwd",
    )(lr3, w2f, b2f, qkvog, qkvog, qkvog)
    return pl.pallas_call(
        functools.partial(_gla_bwd_kernel, tb),
        grid=grid,
        in_specs=specs(bwd_row) + [
            pl.BlockSpec((tb, GLA_V_WIDTH), lambda b, n: (bwd_row(b, n), 0)),
            pl.BlockSpec((tb, GLA_V_WIDTH), lambda b, n: (bwd_row(b, n), 2)),
            pl.BlockSpec((1, GLA_V_WIDTH), const),
        ],
        out_specs=pl.BlockSpec((tb, GLA_V_WIDTH), lambda b, n: (bwd_row(b, n), 0)),
        out_shape=jax.ShapeDtypeStruct((m, GLA_V_WIDTH), BF16),
        scratch_shapes=[st],
        compiler_params=_params("parallel", "arbitrary"),
        name="gla_bwd",
    )(lr3, w2b, b2b, qkvog, qkvog, qkvog, o_fwd, qkvog, gn)


_HALO = 16
_MERGE_ROWS = 256


def _merge_kernel(tm, seq_len, x_ref, og_ref, cb_ref, u_ref, up_ref, un_ref, gates_ref, cw_ref,
                  wa_ref, wb_ref, wo_ref, g2_ref, wr_ref, x1_ref, h2_ref, aff_ref):
    i = pl.program_id(0)
    u = u_ref[...].astype(F32)
    first = (i * tm) % seq_len == 0
    last = ((i + 1) * tm) % seq_len == 0
    prev_row = jnp.where(first, 0.0, up_ref[_HALO - 1:_HALO, :].astype(F32))
    next_row = jnp.where(last, 0.0, un_ref[0:1, :].astype(F32))
    r = lax.broadcasted_iota(jnp.int32, (tm, D_MODEL), 0)
    um1 = jnp.where(r == 0, prev_row, pltpu.roll(u, 1, 0))
    up1 = jnp.where(r == tm - 1, next_row, pltpu.roll(u, tm - 1, 0))
    hc = cw_ref[0:1, :] * um1 + cw_ref[1:2, :] * u + cw_ref[2:3, :] * up1
    cbh = (cb_ref[...].astype(F32) * hc).astype(BF16)
    for r0 in range(0, tm, _MERGE_ROWS):
        rows = slice(r0, r0 + _MERGE_ROWS)
        y_b = _dot(cbh[rows], wb_ref[...])
        y_a = _dot(og_ref[rows, :], wa_ref[...])
        mix = (gates_ref[rows, 0:D_MODEL].astype(F32) * y_a
               + gates_ref[rows, D_MODEL:2 * D_MODEL].astype(F32) * y_b)
        x1 = x_ref[rows, :] + _dot(mix.astype(BF16), wo_ref[...])
        x1_ref[rows, :] = x1
        h2 = (x1 * lax.rsqrt(jnp.mean(x1 * x1, axis=-1, keepdims=True) + EPS) * g2_ref[...]).astype(BF16)
        h2_ref[rows, :] = h2
        logits = _dot(h2, wr_ref[...])
        e = jnp.exp(logits - jnp.max(logits, axis=-1, keepdims=True))
        aff_ref[rows, :] = e / jnp.sum(e, axis=-1, keepdims=True)


def _merge(x, og, cb, u, gates, conv_w, wa, wb, wo, g2, wr, seq_len, tm):
    m = x.shape[0]
    const = lambda i: (0, 0)
    row = lambda i: (i, 0)
    hb = tm // _HALO
    nhb = m // _HALO
    sq = lambda: pl.BlockSpec((D_MODEL, D_MODEL), const)
    return pl.pallas_call(
        functools.partial(_merge_kernel, tm, seq_len),
        grid=(m // tm,),
        in_specs=[
            pl.BlockSpec((tm, D_MODEL), row),
            pl.BlockSpec((tm, D_MODEL), row),
            pl.BlockSpec((tm, D_MODEL), row),
            pl.BlockSpec((tm, D_MODEL), row),
            pl.BlockSpec((_HALO, D_MODEL), lambda i: (jnp.maximum(i * hb - 1, 0), 0)),
            pl.BlockSpec((_HALO, D_MODEL), lambda i: (jnp.minimum((i + 1) * hb, nhb - 1), 0)),
            pl.BlockSpec((tm, 2 * D_MODEL), row),
            pl.BlockSpec((3, D_MODEL), const),
            sq(), sq(), sq(),
            pl.BlockSpec((1, D_MODEL), const),
            pl.BlockSpec((D_MODEL, N_EXPERTS), const),
        ],
        out_specs=[
            pl.BlockSpec((tm, D_MODEL), row),
            pl.BlockSpec((tm, D_MODEL), row),
            pl.BlockSpec((tm, N_EXPERTS), row),
        ],
        out_shape=[
            jax.ShapeDtypeStruct((m, D_MODEL), F32),
            jax.ShapeDtypeStruct((m, D_MODEL), BF16),
            jax.ShapeDtypeStruct((m, N_EXPERTS), F32),
        ],
        compiler_params=_params("parallel"),
        name="merge",
    )(x, og, cb, u, u, u, gates, conv_w, wa, wb, wo, g2, wr)


RB = 256
SUB = 64
ROW_TILE = 16
WIN = SUB + ROW_TILE
NARROW = SUB
_XW = D_MODEL + 128
_IDX_BITS = 30
_MIN_NORMAL_BITS = 0x00800000


def _route_kernel(cap, n_tok_bits, aff_ref, thr_ref, nxt_ref, jb_ref):
    aff = aff_ref[...]
    idx = lax.broadcasted_iota(jnp.int32, aff.shape, 1)
    capf = jnp.float32(cap)

    def count(mask):
        return jnp.sum(mask.astype(F32), axis=1, keepdims=True)

    def as_f32(bits):
        return lax.bitcast_convert_type(bits, F32)

    def value_step(i, prefix):
        cand = prefix | jnp.left_shift(jnp.int32(1), _IDX_BITS - 1 - i)
        return jnp.where(count(aff >= as_f32(cand)) >= capf, cand, prefix)

    thr_bits = lax.fori_loop(0, _IDX_BITS, value_step, jnp.zeros((N_EXPERTS, 1), jnp.int32))
    thr = as_f32(thr_bits)
    nxt = as_f32(jnp.maximum(thr_bits + 1, _MIN_NORMAL_BITS))
    need = capf - count(aff >= nxt)
    tie = (aff >= thr) & (aff < nxt)

    def index_step(i, j):
        cand = j | jnp.left_shift(jnp.int32(1), n_tok_bits - 1 - i)
        return jnp.where(count(tie & (idx < cand)) < need, cand, j)

    thr_ref[...] = thr
    nxt_ref[...] = nxt
    jb_ref[...] = lax.fori_loop(0, n_tok_bits, index_step, jnp.zeros((N_EXPERTS, 1), jnp.int32))


def _route(aff_t, cap):
    n_tok = aff_t.shape[1]
    val = jax.ShapeDtypeStruct((N_EXPERTS, 1), F32)
    return pl.pallas_call(
        functools.partial(_route_kernel, cap, max(1, (n_tok - 1).bit_length())),
        out_shape=[val, val, jax.ShapeDtypeStruct((N_EXPERTS, 1), jnp.int32)],
        compiler_params=pltpu.CompilerParams(vmem_limit_bytes=VMEM_LIMIT_BYTES),
        name="route",
    )(aff_t)


def _selected(aff, tok, thr, nxt, jb):
    return (aff >= nxt) | ((aff >= thr) & (tok <= jb))


def _offsets_kernel(to, aff_ref, thr_ref, nxt_ref, jb_ref, offs_ref, carry_ref):
    i = pl.program_id(0)

    @pl.when(i == 0)
    def _():
        carry_ref[...] = jnp.zeros_like(carry_ref)

    aff = aff_ref[...]
    tok = i * to + lax.broadcasted_iota(jnp.int32, aff.shape, 0)
    sel = _selected(aff, tok, thr_ref[...], nxt_ref[...], jb_ref[...]).astype(BF16)
    ns = to // SUB
    grp = (lax.broadcasted_iota(jnp.int32, (ns, to), 1) // SUB == lax.broadcasted_iota(jnp.int32, (ns, to), 0))
    cnt = _dot(grp.astype(BF16), sel)
    before = (lax.broadcasted_iota(jnp.int32, (ns, ns), 1) < lax.broadcasted_iota(jnp.int32, (ns, ns), 0))
    offs_ref[...] = (carry_ref[...] + _dot(before.astype(BF16), cnt.astype(BF16))).astype(jnp.int32)
    carry_ref[...] += jnp.sum(cnt, axis=0, keepdims=True)


def _offsets(aff, thr_row, nxt_row, jb_row, to):
    m = aff.shape[0]
    const = lambda i: (0, 0)
    row = lambda: pl.BlockSpec((1, N_EXPERTS), const)
    return pl.pallas_call(
        functools.partial(_offsets_kernel, to),
        grid=(m // to,),
        in_specs=[pl.BlockSpec((to, N_EXPERTS), lambda i: (i, 0)), row(), row(), row()],
        out_specs=pl.BlockSpec((to // SUB, N_EXPERTS), lambda i: (i, 0)),
        out_shape=jax.ShapeDtypeStruct((m // SUB, N_EXPERTS), jnp.int32),
        scratch_shapes=[pltpu.VMEM((1, N_EXPERTS), F32)],
        compiler_params=_params("arbitrary"),
        name="offsets",
    )(aff, thr_row, nxt_row, jb_row)


def _block_selection(blk, aff_t_ref, thr_ref, nxt_ref, jb_ref):
    aff = aff_t_ref[...]
    tok = blk * RB + lax.broadcasted_iota(jnp.int32, aff.shape, 1)
    return _selected(aff, tok, thr_ref[...], nxt_ref[...], jb_ref[...])


def _floor_tile(x):
    return pl.multiple_of((x // ROW_TILE) * ROW_TILE, ROW_TILE)


def _slot_onehot(sel, shift, width):
    self = sel.astype(F32)
    before = (lax.broadcasted_iota(jnp.int32, (RB, RB), 0) < lax.broadcasted_iota(jnp.int32, (RB, RB), 1))
    rank = _dot(self.astype(BF16), before.astype(BF16)).astype(jnp.int32)
    slot = lax.broadcasted_iota(jnp.int32, (width, RB), 0)
    rows = [jnp.where(rank[e:e + 1, :] + shift[e] == slot, self[e:e + 1, :], 0.0) for e in range(N_EXPERTS)]
    return jnp.concatenate(rows, axis=0).astype(BF16)


def _block_counts(offs_ref, blk):
    per = RB // SUB
    start = [offs_ref[blk * per * N_EXPERTS + e] for e in range(N_EXPERTS)]
    end = [offs_ref[(blk + 1) * per * N_EXPERTS + e] for e in range(N_EXPERTS)]
    most = functools.reduce(jnp.maximum, [b - a for a, b in zip(start, end)])
    span = functools.reduce(jnp.maximum, [b - _floor_tile(a) for a, b in zip(start, end)])
    return most <= SUB, span <= NARROW


def _sub_mask(sub):
    return lax.broadcasted_iota(jnp.int32, (N_EXPERTS, RB), 1) // SUB == sub


_STEP_BLOCKS = 2


def _gather_kernel(cap, nb, offs_ref, aff_t_ref, aff_ref, thr_ref, nxt_ref, jb_ref, h2_ref, xe_ref,
                   stage_ref, carry_ref, sem_ref):
    for j in range(_STEP_BLOCKS):
        rows = pl.ds(j * RB, RB)
        _gather_block(cap, nb, pl.program_id(0) * _STEP_BLOCKS + j, j % 2, offs_ref, aff_t_ref.at[:, rows],
                      aff_ref.at[rows], thr_ref, nxt_ref, jb_ref, h2_ref.at[rows], xe_ref,
                      stage_ref, carry_ref, sem_ref)


def _gather_block(cap, nb, blk, par, offs_ref, aff_t_ref, aff_ref, thr_ref, nxt_ref, jb_ref, h2_ref, xe_ref,
                  stage_ref, carry_ref, sem_ref):
    per = RB // SUB

    def copy(parity, e, dst_row, width):
        return pltpu.make_async_copy(stage_ref.at[parity, pl.ds(e * width, width)],
                                     xe_ref.at[e, pl.ds(dst_row, width)], sem_ref.at[parity, e])

    def wait_all(parity, width):
        for e in range(N_EXPERTS):
            copy(parity, e, 0, width).wait()

    @pl.when(blk == 0)
    def _():
        carry_ref[...] = jnp.zeros_like(carry_ref)
        stage_ref[1] = jnp.zeros(stage_ref.shape[1:], BF16)
        for e in range(N_EXPERTS):
            copy(1, e, cap, WIN).start()
        wait_all(1, WIN)

    sel = _block_selection(blk, aff_t_ref, thr_ref, nxt_ref, jb_ref)
    aff = aff_ref[...]
    hi = aff.astype(BF16)
    r1 = aff - hi.astype(F32)
    mid = r1.astype(BF16)
    lo = (r1 - mid.astype(F32)).astype(BF16)
    er = lax.broadcasted_iota(jnp.int32, (N_EXPERTS, 128), 0)
    ec = lax.broadcasted_iota(jnp.int32, (N_EXPERTS, 128), 1)
    g3 = (_dot(hi, (ec == 3 * er).astype(BF16)) + _dot(mid, (ec == 3 * er + 1).astype(BF16))
          + _dot(lo, (ec == 3 * er + 2).astype(BF16)))
    src = jnp.concatenate([h2_ref[...], g3.astype(BF16)], axis=1)

    fits, narrow = _block_counts(offs_ref, blk)
    wide = fits & jnp.logical_not(narrow)
    _, prev_narrow = _block_counts(offs_ref, jnp.maximum(blk - 1, 0))

    def emit(mask, seg_lo, seg_hi, first, width):
        lo_rows = [offs_ref[seg_lo * N_EXPERTS + e] for e in range(N_EXPERTS)]
        hi_rows = [offs_ref[seg_hi * N_EXPERTS + e] for e in range(N_EXPERTS)]
        base = [_floor_tile(r) for r in lo_rows]
        onehot = _slot_onehot(mask, [r - b for r, b in zip(lo_rows, base)], width)
        stage_ref[par, pl.ds(0, N_EXPERTS * width)] = _dot(onehot, src).astype(BF16)
        for e in range(N_EXPERTS):
            head = pl.ds(e * width, ROW_TILE)
            stage_ref[par, head] = stage_ref[par, head] + carry_ref[e]
            nxt = _floor_tile(hi_rows[e]) - base[e]
            carry_ref[e] = stage_ref[par, pl.ds(pl.multiple_of(e * width + nxt, ROW_TILE), ROW_TILE)]
        if first:
            @pl.when((blk > 0) & prev_narrow)
            def _():
                wait_all(1 - par, NARROW)

            @pl.when((blk > 0) & jnp.logical_not(prev_narrow))
            def _():
                wait_all(1 - par, WIN)
        for e in range(N_EXPERTS):
            copy(par, e, base[e], width).start()

    @pl.when(narrow)
    def _():
        emit(sel, blk * per, (blk + 1) * per, True, NARROW)

    @pl.when(wide)
    def _():
        emit(sel, blk * per, (blk + 1) * per, True, WIN)

    @pl.when(jnp.logical_not(fits))
    def _():
        for sub in range(per):
            if sub > 0:
                wait_all(par, WIN)
            emit(sel & _sub_mask(sub), blk * per + sub, blk * per + sub + 1, sub == 0, WIN)

    @pl.when((blk == nb - 1) & narrow)
    def _():
        wait_all(par, NARROW)

    @pl.when((blk == nb - 1) & jnp.logical_not(narrow))
    def _():
        wait_all(par, WIN)


def _gather(offs, aff_t, aff, thr, nxt, jb, h2, cap):
    m = h2.shape[0]
    nb = m // RB
    step = _STEP_BLOCKS * RB
    col = lambda i, o: (0, 0)
    return pl.pallas_call(
        functools.partial(_gather_kernel, cap, nb),
        grid_spec=pltpu.PrefetchScalarGridSpec(
            num_scalar_prefetch=1,
            grid=(m // step,),
            in_specs=[
                pl.BlockSpec((N_EXPERTS, step), lambda i, o: (0, i)),
                pl.BlockSpec((step, N_EXPERTS), lambda i, o: (i, 0)),
                pl.BlockSpec((N_EXPERTS, 1), col), pl.BlockSpec((N_EXPERTS, 1), col),
                pl.BlockSpec((N_EXPERTS, 1), col),
                pl.BlockSpec((step, D_MODEL), lambda i, o: (i, 0)),
            ],
            out_specs=pl.BlockSpec(memory_space=pl.ANY),
            scratch_shapes=[pltpu.VMEM((2, N_EXPERTS * WIN, _XW), BF16),
                            pltpu.VMEM((N_EXPERTS, ROW_TILE, _XW), BF16),
                            pltpu.SemaphoreType.DMA((2, N_EXPERTS))],
        ),
        out_shape=jax.ShapeDtypeStruct((N_EXPERTS, cap + WIN, _XW), BF16),
        compiler_params=_params("arbitrary"),
        name="gather",
    )(offs, aff_t, aff, thr, nxt, jb, h2)


def _ffn_kernel(x_ref, wg_ref, wu_ref, wd_ref, o_ref, hid_ref):
    e = pl.program_id(0)
    x = x_ref[0, :, 0:D_MODEL]
    gcols = x_ref[0, :, D_MODEL:_XW].astype(F32)
    lane = lax.broadcasted_iota(jnp.int32, gcols.shape, 1)
    gate = jnp.sum(jnp.where((lane >= 3 * e) & (lane < 3 * e + 3), gcols, 0.0), axis=-1, keepdims=True)
    for f0 in range(0, D_EXPERT, 1024):
        g = _dot(x, wg_ref[0, :, f0:f0 + 1024])
        up = _dot(x, wu_ref[0, :, f0:f0 + 1024])
        hid_ref[:, f0:f0 + 1024] = (g * jax.nn.sigmoid(g) * up).astype(BF16)
    o_ref[0] = (_dot(hid_ref[...], wd_ref[0]) * gate).astype(BF16)


def _ffn(xe, cap, wg, wu, wd, tc):
    e = xe.shape[0]
    return pl.pallas_call(
        _ffn_kernel,
        grid=(e, cap // tc),
        in_specs=[
            pl.BlockSpec((1, tc, _XW), lambda e, j: (e, j, 0)),
            pl.BlockSpec((1, D_MODEL, D_EXPERT), lambda e, j: (e, 0, 0)),
            pl.BlockSpec((1, D_MODEL, D_EXPERT), lambda e, j: (e, 0, 0)),
            pl.BlockSpec((1, D_EXPERT, D_MODEL), lambda e, j: (e, 0, 0)),
        ],
        out_specs=pl.BlockSpec((1, tc, D_MODEL), lambda e, j: (e, j, 0)),
        out_shape=jax.ShapeDtypeStruct((e, cap, D_MODEL), BF16),
        scratch_shapes=[pltpu.VMEM((tc, D_EXPERT), BF16)],
        compiler_params=_params("parallel", "parallel"),
        name="experts",
    )(xe, wg, wu, wd)


def _combine_kernel(cap, nb, offs_ref, aff_t_ref, thr_ref, nxt_ref, jb_ref, x1_ref, g_ref, ye_ref, o_ref,
                    win_ref, sem_ref):
    for j in range(_STEP_BLOCKS):
        rows = pl.ds(j * RB, RB)
        _combine_block(cap, nb, pl.program_id(0) * _STEP_BLOCKS + j, j % 2, offs_ref, aff_t_ref.at[:, rows],
                       thr_ref, nxt_ref, jb_ref, x1_ref.at[rows], g_ref, ye_ref, o_ref.at[rows], win_ref, sem_ref)


def _combine_block(cap, nb, blk, par, offs_ref, aff_t_ref, thr_ref, nxt_ref, jb_ref, x1_ref, g_ref, ye_ref, o_ref,
                   win_ref, sem_ref):
    per = RB // SUB

    def window(row, width):
        first = pl.multiple_of(jnp.minimum(_floor_tile(row), cap - width), ROW_TILE)
        return first, row - first

    def copy(parity, e, first, width):
        return pltpu.make_async_copy(ye_ref.at[e, pl.ds(first, width)],
                                     win_ref.at[parity, pl.ds(e * width, width)], sem_ref.at[parity, e])

    def fetch(parity, rows, width):
        for e in range(N_EXPERTS):
            copy(parity, e, window(rows[e], width)[0], width).start()

    def wait_all(parity, width):
        for e in range(N_EXPERTS):
            copy(parity, e, 0, width).wait()

    def block_rows(b, sub):
        return [offs_ref[(b * per + sub) * N_EXPERTS + e] for e in range(N_EXPERTS)]

    def fetch_block(parity, b):
        _, b_narrow = _block_counts(offs_ref, b)

        @pl.when(b_narrow)
        def _():
            fetch(parity, block_rows(b, 0), NARROW)

        @pl.when(jnp.logical_not(b_narrow))
        def _():
            fetch(parity, block_rows(b, 0), WIN)

    @pl.when(blk == 0)
    def _():
        fetch_block(par, blk)

    @pl.when(blk + 1 < nb)
    def _():
        fetch_block(1 - par, jnp.minimum(blk + 1, nb - 1))

    sel = _block_selection(blk, aff_t_ref, thr_ref, nxt_ref, jb_ref)
    fits, narrow = _block_counts(offs_ref, blk)
    start = block_rows(blk, 0)

    def contribution(mask, rows, width):
        onehot = _slot_onehot(mask, [window(r, width)[1] for r in rows], width)
        return _dot_tn(onehot, win_ref[par, pl.ds(0, N_EXPERTS * width)])

    def finish(y):
        x = x1_ref[...] + y
        o_ref[...] = x * lax.rsqrt(jnp.mean(x * x, axis=-1, keepdims=True) + EPS) * g_ref[...]

    @pl.when(narrow)
    def _():
        wait_all(par, NARROW)
        finish(contribution(sel, start, NARROW))

    @pl.when(fits & jnp.logical_not(narrow))
    def _():
        wait_all(par, WIN)
        finish(contribution(sel, start, WIN))

    @pl.when(jnp.logical_not(fits))
    def _():
        wait_all(par, WIN)
        y = contribution(sel & _sub_mask(0), start, WIN)
        for sub in range(1, per):
            rows = block_rows(blk, sub)
            fetch(par, rows, WIN)
            wait_all(par, WIN)
            y = y + contribution(sel & _sub_mask(sub), rows, WIN)
        finish(y)


def _combine(offs, aff_t, thr, nxt, jb, x1, g, ye, cap):
    m = x1.shape[0]
    nb = m // RB
    step = _STEP_BLOCKS * RB
    col = lambda i, o: (0, 0)
    return pl.pallas_call(
        functools.partial(_combine_kernel, cap, nb),
        grid_spec=pltpu.PrefetchScalarGridSpec(
            num_scalar_prefetch=1,
            grid=(m // step,),
            in_specs=[
                pl.BlockSpec((N_EXPERTS, step), lambda i, o: (0, i)),
                pl.BlockSpec((N_EXPERTS, 1), col), pl.BlockSpec((N_EXPERTS, 1), col),
                pl.BlockSpec((N_EXPERTS, 1), col),
                pl.BlockSpec((step, D_MODEL), lambda i, o: (i, 0)),
                pl.BlockSpec((1, D_MODEL), col),
                pl.BlockSpec(memory_space=pl.ANY),
            ],
            out_specs=pl.BlockSpec((step, D_MODEL), lambda i, o: (i, 0)),
            scratch_shapes=[pltpu.VMEM((2, N_EXPERTS * WIN, D_MODEL), BF16),
                            pltpu.SemaphoreType.DMA((2, N_EXPERTS))],
        ),
        out_shape=jax.ShapeDtypeStruct((m, D_MODEL), F32),
        compiler_params=_params("arbitrary"),
        name="combine",
    )(offs, aff_t, thr, nxt, jb, x1, g, ye)


def _stack_w2(w2, lo):
    w = jnp.pad(w2, ((lo, _LR - GLA_GATE_RANK - lo), (0, 0)))
    hi = w.astype(BF16)
    lo_part = (w - hi.astype(F32)).astype(BF16)
    return jnp.concatenate([hi, hi, lo_part, jnp.zeros_like(hi)], axis=0)


_TILE_INPROJ = 256
_TILE_GLA = 1024
_TILE_MERGE = 512
_TILE_OFFSETS = 2048
_TILE_EXPERTS = 1024


def _trunk(x3, wts):
    n_seq, seq_len, _ = x3.shape
    m = n_seq * seq_len
    cap = max(1, EC_CAPACITY_FACTOR * m // N_EXPERTS)
    assert seq_len % _TILE_GLA == 0 and seq_len % _TILE_MERGE == 0 and m % _TILE_OFFSETS == 0
    assert cap >= WIN and cap % _TILE_EXPERTS == 0 and m % (_STEP_BLOCKS * RB) == 0
    x = x3.reshape(m, D_MODEL)
    qkvog, cb, u, gates, lr3 = _inproj(x, wts["norm_mix_g"], wts["w_main"], wts["w_lr3"], wts["b_merge"],
                                       tm=_TILE_INPROJ)
    og = _gla(qkvog, lr3, wts["w2f"], wts["b2f"], wts["w2b"], wts["b2b"], wts["gla_norm_g"],
              n_seq, seq_len, tb=_TILE_GLA)
    x1, h2, aff = _merge(x, og, cb, u, gates, wts["conv_w"], wts["w_gla_out"], wts["w_conv_out"], wts["w_out"],
                         wts["norm_ffn_g"], wts["w_router"], seq_len, tm=_TILE_MERGE)
    aff_t = aff.T
    thr, nxt, jb = _route(aff_t, cap)
    row = lambda a: a.reshape(1, N_EXPERTS)
    offs = _offsets(aff, row(thr), row(nxt), row(jb), to=_TILE_OFFSETS)
    offs = jnp.concatenate([offs, jnp.full((1, N_EXPERTS), cap, jnp.int32)], axis=0).reshape(-1)
    xe = _gather(offs, aff_t, aff, thr, nxt, jb, h2, cap)
    ye = _ffn(xe, cap, wts["w_exp_gate"], wts["w_exp_up"], wts["w_exp_down"], tc=_TILE_EXPERTS)
    out = _combine(offs, aff_t, thr, nxt, jb, x1, wts["norm_final_g"], ye, cap)
    return out.reshape(n_seq, seq_len, D_MODEL)


def kernel(x_prompt, x_sample, norm_mix_g, w_in, w_gk2_fwd, b_gk_fwd, w_gk2_bwd, b_gk_bwd, gla_norm_g, w_gla_out,
           conv_w, w_conv_out, b_merge, w_out, norm_ffn_g, w_router, w_exp_gate, w_exp_up, w_exp_down, norm_final_g):
    w = w_in[0]
    lr0 = _QKVOG
    w_lr = w[:, lr0:lr0 + _LR].astype(BF16)
    wts = {
        "norm_mix_g": norm_mix_g[0][None, :],
        "w_main": jnp.concatenate([w[:, :lr0], w[:, lr0 + _LR:]], axis=1).astype(BF16),
        "w_lr3": jnp.concatenate([w_lr, w_lr, w_lr, jnp.zeros_like(w_lr)], axis=1),
        "w2f": _stack_w2(w_gk2_fwd[0], 0),
        "w2b": _stack_w2(w_gk2_bwd[0], GLA_GATE_RANK),
        "b2f": b_gk_fwd[0][None, :],
        "b2b": b_gk_bwd[0][None, :],
        "gla_norm_g": gla_norm_g[0].reshape(1, GLA_V_WIDTH),
        "w_gla_out": w_gla_out[0].astype(BF16),
        "conv_w": conv_w[0],
        "w_conv_out": w_conv_out[0].astype(BF16),
        "b_merge": b_merge[0][None, :],
        "w_out": w_out[0].astype(BF16),
        "norm_ffn_g": norm_ffn_g[0][None, :],
        "w_router": w_router[0].astype(BF16),
        "w_exp_gate": w_exp_gate[0].astype(BF16),
        "w_exp_up": w_exp_up[0].astype(BF16),
        "w_exp_down": w_exp_down[0].astype(BF16),
        "norm_final_g": norm_final_g[None, :],
    }
    return (_trunk(x_prompt, wts), _trunk(x_sample, wts))
```

```python
import functools

import jax
import jax.numpy as jnp
from jax import lax
from jax.experimental import pallas as pl
from jax.experimental.pallas import tpu as pltpu

D_MODEL = 1024
GLA_HEADS = 4
GLA_DK = 128
GLA_DV = 256
GLA_QK_WIDTH = GLA_HEADS * GLA_DK
GLA_V_WIDTH = GLA_HEADS * GLA_DV
GLA_GATE_RANK = 16
GLA_GATE_NORM = 16.0
GLA_CHUNK = 64
GLA_BLOCK = 256
N_EXPERTS = 16
EC_CAPACITY_FACTOR = 2
D_EXPERT = 2 * D_MODEL
EPS = 1e-6

BF16 = jnp.bfloat16
F32 = jnp.float32

VMEM_LIMIT_BYTES = 56 * 1024 * 1024

_QKVOG = 2 * GLA_QK_WIDTH + 2 * GLA_V_WIDTH
_LR = 2 * GLA_GATE_RANK
_LR3 = 128


def _dot(a, b):
    return jnp.dot(a, b, preferred_element_type=F32)


def _dot_nt(a, b):
    return lax.dot_general(a, b, (((1,), (1,)), ((), ())), preferred_element_type=F32)


def _dot_tn(a, b):
    return lax.dot_general(a, b, (((0,), (0,)), ((), ())), preferred_element_type=F32)


def _params(*sem):
    return pltpu.CompilerParams(dimension_semantics=sem, vmem_limit_bytes=VMEM_LIMIT_BYTES)


def _inproj_kernel(x_ref, g_ref, w_ref, wlr_ref, bm_ref, qkvog_ref, cb_ref, u_ref, gates_ref, lr3_ref):
    x = x_ref[...]
    h = x * lax.rsqrt(jnp.mean(x * x, axis=-1, keepdims=True) + EPS) * g_ref[...]
    hb = h.astype(BF16)
    lr = _dot(hb, wlr_ref[...])
    hi = lr.astype(BF16)
    lo = (lr - hi.astype(F32)).astype(BF16)
    lane = lax.broadcasted_iota(jnp.int32, lr.shape, 1)
    lr3_ref[...] = jnp.where((lane >= _LR) & (lane < 2 * _LR), lo, hi)
    q = _dot(hb, w_ref[:, 0:GLA_QK_WIDTH]) * (GLA_DK ** -0.5)
    qkvog_ref[:, 0:GLA_QK_WIDTH] = q.astype(BF16)
    for c0 in range(GLA_QK_WIDTH, _QKVOG, 512):
        qkvog_ref[:, c0:c0 + 512] = _dot(hb, w_ref[:, c0:c0 + 512]).astype(BF16)
    o = _QKVOG
    for c0 in range(0, D_MODEL, 512):
        cb_ref[:, c0:c0 + 512] = _dot(hb, w_ref[:, o + c0:o + c0 + 512]).astype(BF16)
    for c0 in range(0, D_MODEL, 512):
        cc = _dot(hb, w_ref[:, o + D_MODEL + c0:o + D_MODEL + c0 + 512])
        cx = _dot(hb, w_ref[:, o + 2 * D_MODEL + c0:o + 2 * D_MODEL + c0 + 512])
        u_ref[:, c0:c0 + 512] = (cc * cx).astype(BF16)
    o = _QKVOG + 3 * D_MODEL
    for c0 in range(0, 2 * D_MODEL, 512):
        gm = _dot(hb, w_ref[:, o + c0:o + c0 + 512]) + bm_ref[:, c0:c0 + 512]
        gates_ref[:, c0:c0 + 512] = jax.nn.sigmoid(gm).astype(BF16)


def _inproj(x, g, w_main, w_lr3, b_merge, tm):
    m = x.shape[0]
    ncols = w_main.shape[1]
    const = lambda i: (0, 0)
    row = lambda i: (i, 0)
    return pl.pallas_call(
        _inproj_kernel,
        grid=(m // tm,),
        in_specs=[
            pl.BlockSpec((tm, D_MODEL), row),
            pl.BlockSpec((1, D_MODEL), const),
            pl.BlockSpec((D_MODEL, ncols), const, pipeline_mode=pl.Buffered(1)),
            pl.BlockSpec((D_MODEL, _LR3), const),
            pl.BlockSpec((1, 2 * D_MODEL), const),
        ],
        out_specs=[
            pl.BlockSpec((tm, _QKVOG), row),
            pl.BlockSpec((tm, D_MODEL), row),
            pl.BlockSpec((tm, D_MODEL), row),
            pl.BlockSpec((tm, 2 * D_MODEL), row),
            pl.BlockSpec((tm, _LR3), row),
        ],
        out_shape=[
            jax.ShapeDtypeStruct((m, _QKVOG), BF16),
            jax.ShapeDtypeStruct((m, D_MODEL), BF16),
            jax.ShapeDtypeStruct((m, D_MODEL), BF16),
            jax.ShapeDtypeStruct((m, 2 * D_MODEL), BF16),
            jax.ShapeDtypeStruct((m, _LR3), BF16),
        ],
        compiler_params=_params("parallel"),
        name="inproj",
    )(x, g, w_main, w_lr3, b_merge)


def _gla_body(rev, rows, lr3_ref, w2_ref, b2_ref, q_ref, k_ref, v_ref, st_ref):
    L = GLA_CHUNK
    tb = GLA_BLOCK
    nc = tb // L
    g = _dot(lr3_ref[rows, :], w2_ref[...]) + b2_ref[...]
    la = (jnp.minimum(g, 0.0) - jnp.log(1.0 + jnp.exp(-jnp.abs(g)))) * (1.0 / GLA_GATE_NORM)
    ri = lax.broadcasted_iota(jnp.int32, (tb, tb), 0)
    ci = lax.broadcasted_iota(jnp.int32, (tb, tb), 1)
    keep = ((ri // L) == (ci // L)) & ((ci >= ri) if rev else (ci <= ri))
    tri = keep.astype(BF16)
    la_hi = la.astype(BF16)
    la_lo = (la - la_hi.astype(F32)).astype(BF16)
    b = _dot(tri, la_hi) + _dot(tri, la_lo)
    ref_row = L // 2 if rev else L // 2 - 1
    end_row = 0 if rev else L - 1
    b_end_rows = [b[c * L + end_row:c * L + end_row + 1, :] for c in range(nc)]
    b_ref = jnp.concatenate(
        [jnp.broadcast_to(b[c * L + ref_row:c * L + ref_row + 1, :], (L, GLA_QK_WIDTH)) for c in range(nc)], axis=0)
    b_end = jnp.concatenate([jnp.broadcast_to(x, (L, GLA_QK_WIDTH)) for x in b_end_rows], axis=0)
    a_end_rows = [jnp.exp(x) for x in b_end_rows]

    q = q_ref[rows, :].astype(F32)
    k = k_ref[rows, :].astype(F32)
    qe = (q * jnp.exp(b - b_ref)).astype(BF16)
    ke = (k * jnp.exp(b_ref - b)).astype(BF16)
    q_in = (q * jnp.exp(b)).astype(BF16)
    k_out = (k * jnp.exp(b_end - b)).astype(BF16)
    order = range(nc - 1, -1, -1) if rev else range(nc)
    outs = []
    for h in range(GLA_HEADS):
        ks = slice(h * GLA_DK, (h + 1) * GLA_DK)
        v = v_ref[rows, h * GLA_DV:(h + 1) * GLA_DV]
        a = jnp.where(keep, _dot_nt(qe[:, ks], ke[:, ks]), 0.0).astype(BF16)
        o_intra = _dot(a, v)
        st = st_ref[h]
        o_inter = [None] * nc
        for c in order:
            cr = slice(c * L, (c + 1) * L)
            o_inter[c] = _dot_nt(q_in[cr, ks], st.astype(BF16))
            st = st * a_end_rows[c][:, ks] + _dot_tn(v[cr], k_out[cr, ks])
        st_ref[h] = st
        outs.append(o_intra + jnp.concatenate(o_inter, axis=0))
    return outs


def _gla_fwd_kernel(tb, lr3_ref, w2_ref, b2_ref, q_ref, k_ref, v_ref, o_ref, st_ref):
    @pl.when(pl.program_id(1) == 0)
    def _():
        st_ref[...] = jnp.zeros_like(st_ref)
    for sb in range(tb // GLA_BLOCK):
        rows = slice(sb * GLA_BLOCK, (sb + 1) * GLA_BLOCK)
        outs = _gla_body(False, rows, lr3_ref, w2_ref, b2_ref, q_ref, k_ref, v_ref, st_ref)
        for h in range(GLA_HEADS):
            o_ref[rows, h * GLA_DV:(h + 1) * GLA_DV] = outs[h]


def _gla_bwd_kernel(tb, lr3_ref, w2_ref, b2_ref, q_ref, k_ref, v_ref, of_ref, og_ref, gn_ref, o_ref, st_ref):
    @pl.when(pl.program_id(1) == 0)
    def _():
        st_ref[...] = jnp.zeros_like(st_ref)
    for sb in range(tb // GLA_BLOCK - 1, -1, -1):
        rows = slice(sb * GLA_BLOCK, (sb + 1) * GLA_BLOCK)
        outs = _gla_body(True, rows, lr3_ref, w2_ref, b2_ref, q_ref, k_ref, v_ref, st_ref)
        for h in range(GLA_HEADS):
            vs = slice(h * GLA_DV, (h + 1) * GLA_DV)
            o = of_ref[rows, vs] + outs[h]
            o = o * lax.rsqrt(jnp.mean(o * o, axis=-1, keepdims=True) + EPS) * gn_ref[:, vs]
            og = og_ref[rows, vs].astype(F32)
            o_ref[rows, vs] = (o * (og * jax.nn.sigmoid(og))).astype(BF16)


def _gla(qkvog, lr3, w2f, b2f, w2b, b2b, gn, n_seq, seq_len, tb):
    m = qkvog.shape[0]
    nb = seq_len // tb
    grid = (n_seq, nb)
    const = lambda b, n: (0, 0)

    def specs(rowf):
        return [
            pl.BlockSpec((tb, _LR3), lambda b, n: (rowf(b, n), 0)),
            pl.BlockSpec((_LR3, GLA_QK_WIDTH), const),
            pl.BlockSpec((1, GLA_QK_WIDTH), const),
            pl.BlockSpec((tb, GLA_QK_WIDTH), lambda b, n: (rowf(b, n), 0)),
            pl.BlockSpec((tb, GLA_QK_WIDTH), lambda b, n: (rowf(b, n), 1)),
            pl.BlockSpec((tb, GLA_V_WIDTH), lambda b, n: (rowf(b, n), 1)),
        ]

    fwd_row = lambda b, n: b * nb + n
    bwd_row = lambda b, n: b * nb + (nb - 1 - n)
    st = pltpu.VMEM((GLA_HEADS, GLA_DV, GLA_DK), F32)
    o_fwd = pl.pallas_call(
        functools.partial(_gla_fwd_kernel, tb),
        grid=grid,
        in_specs=specs(fwd_row),
        out_specs=pl.BlockSpec((tb, GLA_V_WIDTH), lambda b, n: (fwd_row(b, n), 0)),
        out_shape=jax.ShapeDtypeStruct((m, GLA_V_WIDTH), F32),
        scratch_shapes=[st],
        compiler_params=_params("parallel", "arbitrary"),
        name="gla_fwd",
    )(lr3, w2f, b2f, qkvog, qkvog, qkvog)
    return pl.pallas_call(
        functools.partial(_gla_bwd_kernel, tb),
        grid=grid,
        in_specs=specs(bwd_row) + [
            pl.BlockSpec((tb, GLA_V_WIDTH), lambda b, n: (bwd_row(b, n), 0)),
            pl.BlockSpec((tb, GLA_V_WIDTH), lambda b, n: (bwd_row(b, n), 2)),
            pl.BlockSpec((1, GLA_V_WIDTH), const),
        ],
        out_specs=pl.BlockSpec((tb, GLA_V_WIDTH), lambda b, n: (bwd_row(b, n), 0)),
        out_shape=jax.ShapeDtypeStruct((m, GLA_V_WIDTH), BF16),
        scratch_shapes=[st],
        compiler_params=_params("parallel", "arbitrary"),
        name="gla_bwd",
    )(lr3, w2b, b2b, qkvog, qkvog, qkvog, o_fwd, qkvog, gn)


_HALO = 16
_MERGE_ROWS = 256


def _merge_kernel(tm, seq_len, x_ref, og_ref, cb_ref, u_ref, up_ref, un_ref, gates_ref, cw_ref,
                  wa_ref, wb_ref, wo_ref, g2_ref, wr_ref, x1_ref, h2_ref, aff_ref):
    i = pl.program_id(0)
    u = u_ref[...].astype(F32)
    first = (i * tm) % seq_len == 0
    last = ((i + 1) * tm) % seq_len == 0
    prev_row = jnp.where(first, 0.0, up_ref[_HALO - 1:_HALO, :].astype(F32))
    next_row = jnp.where(last, 0.0, un_ref[0:1, :].astype(F32))
    r = lax.broadcasted_iota(jnp.int32, (tm, D_MODEL), 0)
    um1 = jnp.where(r == 0, prev_row, pltpu.roll(u, 1, 0))
    up1 = jnp.where(r == tm - 1, next_row, pltpu.roll(u, tm - 1, 0))
    hc = cw_ref[0:1, :] * um1 + cw_ref[1:2, :] * u + cw_ref[2:3, :] * up1
    cbh = (cb_ref[...].astype(F32) * hc).astype(BF16)
    for r0 in range(0, tm, _MERGE_ROWS):
        rows = slice(r0, r0 + _MERGE_ROWS)
        y_b = _dot(cbh[rows], wb_ref[...])
        y_a = _dot(og_ref[rows, :], wa_ref[...])
        mix = (gates_ref[rows, 0:D_MODEL].astype(F32) * y_a
               + gates_ref[rows, D_MODEL:2 * D_MODEL].astype(F32) * y_b)
        x1 = x_ref[rows, :] + _dot(mix.astype(BF16), wo_ref[...])
        x1_ref[rows, :] = x1
        h2 = (x1 * lax.rsqrt(jnp.mean(x1 * x1, axis=-1, keepdims=True) + EPS) * g2_ref[...]).astype(BF16)
        h2_ref[rows, :] = h2
        logits = _dot(h2, wr_ref[...])
        e = jnp.exp(logits - jnp.max(logits, axis=-1, keepdims=True))
        aff_ref[rows, :] = e / jnp.sum(e, axis=-1, keepdims=True)


def _merge(x, og, cb, u, gates, conv_w, wa, wb, wo, g2, wr, seq_len, tm):
    m = x.shape[0]
    const = lambda i: (0, 0)
    row = lambda i: (i, 0)
    hb = tm // _HALO
    nhb = m // _HALO
    sq = lambda: pl.BlockSpec((D_MODEL, D_MODEL), const)
    return pl.pallas_call(
        functools.partial(_merge_kernel, tm, seq_len),
        grid=(m // tm,),
        in_specs=[
            pl.BlockSpec((tm, D_MODEL), row),
            pl.BlockSpec((tm, D_MODEL), row),
            pl.BlockSpec((tm, D_MODEL), row),
            pl.BlockSpec((tm, D_MODEL), row),
            pl.BlockSpec((_HALO, D_MODEL), lambda i: (jnp.maximum(i * hb - 1, 0), 0)),
            pl.BlockSpec((_HALO, D_MODEL), lambda i: (jnp.minimum((i + 1) * hb, nhb - 1), 0)),
            pl.BlockSpec((tm, 2 * D_MODEL), row),
            pl.BlockSpec((3, D_MODEL), const),
            sq(), sq(), sq(),
            pl.BlockSpec((1, D_MODEL), const),
            pl.BlockSpec((D_MODEL, N_EXPERTS), const),
        ],
        out_specs=[
            pl.BlockSpec((tm, D_MODEL), row),
            pl.BlockSpec((tm, D_MODEL), row),
            pl.BlockSpec((tm, N_EXPERTS), row),
        ],
        out_shape=[
            jax.ShapeDtypeStruct((m, D_MODEL), F32),
            jax.ShapeDtypeStruct((m, D_MODEL), BF16),
            jax.ShapeDtypeStruct((m, N_EXPERTS), F32),
        ],
        compiler_params=_params("parallel"),
        name="merge",
    )(x, og, cb, u, u, u, gates, conv_w, wa, wb, wo, g2, wr)


RB = 256
SUB = 64
ROW_TILE = 16
WIN = SUB + ROW_TILE
NARROW = SUB
_XW = D_MODEL + 128
_IDX_BITS = 30
_MIN_NORMAL_BITS = 0x00800000


def _route_kernel(cap, n_tok_bits, aff_ref, thr_ref, nxt_ref, jb_ref):
    aff = aff_ref[...]
    idx = lax.broadcasted_iota(jnp.int32, aff.shape, 1)
    capf = jnp.float32(cap)

    def count(mask):
        return jnp.sum(mask.astype(F32), axis=1, keepdims=True)

    def as_f32(bits):
        return lax.bitcast_convert_type(bits, F32)

    def value_step(i, prefix):
        cand = prefix | jnp.left_shift(jnp.int32(1), _IDX_BITS - 1 - i)
        return jnp.where(count(aff >= as_f32(cand)) >= capf, cand, prefix)

    thr_bits = lax.fori_loop(0, _IDX_BITS, value_step, jnp.zeros((N_EXPERTS, 1), jnp.int32))
    thr = as_f32(thr_bits)
    nxt = as_f32(jnp.maximum(thr_bits + 1, _MIN_NORMAL_BITS))
    need = capf - count(aff >= nxt)
    tie = (aff >= thr) & (aff < nxt)

    def index_step(i, j):
        cand = j | jnp.left_shift(jnp.int32(1), n_tok_bits - 1 - i)
        return jnp.where(count(tie & (idx < cand)) < need, cand, j)

    thr_ref[...] = thr
    nxt_ref[...] = nxt
    jb_ref[...] = lax.fori_loop(0, n_tok_bits, index_step, jnp.zeros((N_EXPERTS, 1), jnp.int32))


def _route(aff_t, cap):
    n_tok = aff_t.shape[1]
    val = jax.ShapeDtypeStruct((N_EXPERTS, 1), F32)
    return pl.pallas_call(
        functools.partial(_route_kernel, cap, max(1, (n_tok - 1).bit_length())),
        out_shape=[val, val, jax.ShapeDtypeStruct((N_EXPERTS, 1), jnp.int32)],
        compiler_params=pltpu.CompilerParams(vmem_limit_bytes=VMEM_LIMIT_BYTES),
        name="route",
    )(aff_t)


def _selected(aff, tok, thr, nxt, jb):
    return (aff >= nxt) | ((aff >= thr) & (tok <= jb))


def _offsets_kernel(to, aff_ref, thr_ref, nxt_ref, jb_ref, offs_ref, carry_ref):
    i = pl.program_id(0)

    @pl.when(i == 0)
    def _():
        carry_ref[...] = jnp.zeros_like(carry_ref)

    aff = aff_ref[...]
    tok = i * to + lax.broadcasted_iota(jnp.int32, aff.shape, 0)
    sel = _selected(aff, tok, thr_ref[...], nxt_ref[...], jb_ref[...]).astype(BF16)
    ns = to // SUB
    grp = (lax.broadcasted_iota(jnp.int32, (ns, to), 1) // SUB == lax.broadcasted_iota(jnp.int32, (ns, to), 0))
    cnt = _dot(grp.astype(BF16), sel)
    before = (lax.broadcasted_iota(jnp.int32, (ns, ns), 1) < lax.broadcasted_iota(jnp.int32, (ns, ns), 0))
    offs_ref[...] = (carry_ref[...] + _dot(before.astype(BF16), cnt.astype(BF16))).astype(jnp.int32)
    carry_ref[...] += jnp.sum(cnt, axis=0, keepdims=True)


def _offsets(aff, thr_row, nxt_row, jb_row, to):
    m = aff.shape[0]
    const = lambda i: (0, 0)
    row = lambda: pl.BlockSpec((1, N_EXPERTS), const)
    return pl.pallas_call(
        functools.partial(_offsets_kernel, to),
        grid=(m // to,),
        in_specs=[pl.BlockSpec((to, N_EXPERTS), lambda i: (i, 0)), row(), row(), row()],
        out_specs=pl.BlockSpec((to // SUB, N_EXPERTS), lambda i: (i, 0)),
        out_shape=jax.ShapeDtypeStruct((m // SUB, N_EXPERTS), jnp.int32),
        scratch_shapes=[pltpu.VMEM((1, N_EXPERTS), F32)],
        compiler_params=_params("arbitrary"),
        name="offsets",
    )(aff, thr_row, nxt_row, jb_row)


def _block_selection(blk, aff_t_ref, thr_ref, nxt_ref, jb_ref):
    aff = aff_t_ref[...]
    tok = blk * RB + lax.broadcasted_iota(jnp.int32, aff.shape, 1)
    return _selected(aff, tok, thr_ref[...], nxt_ref[...], jb_ref[...])


def _floor_tile(x):
    return pl.multiple_of((x // ROW_TILE) * ROW_TILE, ROW_TILE)


def _slot_onehot(sel, shift, width):
    self = sel.astype(F32)
    before = (lax.broadcasted_iota(jnp.int32, (RB, RB), 0) < lax.broadcasted_iota(jnp.int32, (RB, RB), 1))
    rank = _dot(self.astype(BF16), before.astype(BF16)).astype(jnp.int32)
    slot = lax.broadcasted_iota(jnp.int32, (width, RB), 0)
    rows = [jnp.where(rank[e:e + 1, :] + shift[e] == slot, self[e:e + 1, :], 0.0) for e in range(N_EXPERTS)]
    return jnp.concatenate(rows, axis=0).astype(BF16)


def _block_counts(offs_ref, blk):
    per = RB // SUB
    start = [offs_ref[blk * per * N_EXPERTS + e] for e in range(N_EXPERTS)]
    end = [offs_ref[(blk + 1) * per * N_EXPERTS + e] for e in range(N_EXPERTS)]
    most = functools.reduce(jnp.maximum, [b - a for a, b in zip(start, end)])
    span = functools.reduce(jnp.maximum, [b - _floor_tile(a) for a, b in zip(start, end)])
    return most <= SUB, span <= NARROW


def _sub_mask(sub):
    return lax.broadcasted_iota(jnp.int32, (N_EXPERTS, RB), 1) // SUB == sub


_STEP_BLOCKS = 2


def _gather_kernel(cap, nb, offs_ref, aff_t_ref, aff_ref, thr_ref, nxt_ref, jb_ref, h2_ref, xe_ref,
                   stage_ref, carry_ref, sem_ref):
    for j in range(_STEP_BLOCKS):
        rows = pl.ds(j * RB, RB)
        _gather_block(cap, nb, pl.program_id(0) * _STEP_BLOCKS + j, j % 2, offs_ref, aff_t_ref.at[:, rows],
                      aff_ref.at[rows], thr_ref, nxt_ref, jb_ref, h2_ref.at[rows], xe_ref,
                      stage_ref, carry_ref, sem_ref)


def _gather_block(cap, nb, blk, par, offs_ref, aff_t_ref, aff_ref, thr_ref, nxt_ref, jb_ref, h2_ref, xe_ref,
                  stage_ref, carry_ref, sem_ref):
    per = RB // SUB

    def copy(parity, e, dst_row, width):
        return pltpu.make_async_copy(stage_ref.at[parity, pl.ds(e * width, width)],
                                     xe_ref.at[e, pl.ds(dst_row, width)], sem_ref.at[parity, e])

    def wait_all(parity, width):
        for e in range(N_EXPERTS):
            copy(parity, e, 0, width).wait()

    @pl.when(blk == 0)
    def _():
        carry_ref[...] = jnp.zeros_like(carry_ref)
        stage_ref[1] = jnp.zeros(stage_ref.shape[1:], BF16)
        for e in range(N_EXPERTS):
            copy(1, e, cap, WIN).start()
        wait_all(1, WIN)

    sel = _block_selection(blk, aff_t_ref, thr_ref, nxt_ref, jb_ref)
    aff = aff_ref[...]
    hi = aff.astype(BF16)
    r1 = aff - hi.astype(F32)
    mid = r1.astype(BF16)
    lo = (r1 - mid.astype(F32)).astype(BF16)
    er = lax.broadcasted_iota(jnp.int32, (N_EXPERTS, 128), 0)
    ec = lax.broadcasted_iota(jnp.int32, (N_EXPERTS, 128), 1)
    g3 = (_dot(hi, (ec == 3 * er).astype(BF16)) + _dot(mid, (ec == 3 * er + 1).astype(BF16))
          + _dot(lo, (ec == 3 * er + 2).astype(BF16)))
    src = jnp.concatenate([h2_ref[...], g3.astype(BF16)], axis=1)

    fits, narrow = _block_counts(offs_ref, blk)
    wide = fits & jnp.logical_not(narrow)
    _, prev_narrow = _block_counts(offs_ref, jnp.maximum(blk - 1, 0))

    def emit(mask, seg_lo, seg_hi, first, width):
        lo_rows = [offs_ref[seg_lo * N_EXPERTS + e] for e in range(N_EXPERTS)]
        hi_rows = [offs_ref[seg_hi * N_EXPERTS + e] for e in range(N_EXPERTS)]
        base = [_floor_tile(r) for r in lo_rows]
        onehot = _slot_onehot(mask, [r - b for r, b in zip(lo_rows, base)], width)
        stage_ref[par, pl.ds(0, N_EXPERTS * width)] = _dot(onehot, src).astype(BF16)
        for e in range(N_EXPERTS):
            head = pl.ds(e * width, ROW_TILE)
            stage_ref[par, head] = stage_ref[par, head] + carry_ref[e]
            nxt = _floor_tile(hi_rows[e]) - base[e]
            carry_ref[e] = stage_ref[par, pl.ds(pl.multiple_of(e * width + nxt, ROW_TILE), ROW_TILE)]
        if first:
            @pl.when((blk > 0) & prev_narrow)
            def _():
                wait_all(1 - par, NARROW)

            @pl.when((blk > 0) & jnp.logical_not(prev_narrow))
            def _():
                wait_all(1 - par, WIN)
        for e in range(N_EXPERTS):
            copy(par, e, base[e], width).start()

    @pl.when(narrow)
    def _():
        emit(sel, blk * per, (blk + 1) * per, True, NARROW)

    @pl.when(wide)
    def _():
        emit(sel, blk * per, (blk + 1) * per, True, WIN)

    @pl.when(jnp.logical_not(fits))
    def _():
        for sub in range(per):
            if sub > 0:
                wait_all(par, WIN)
            emit(sel & _sub_mask(sub), blk * per + sub, blk * per + sub + 1, sub == 0, WIN)

    @pl.when((blk == nb - 1) & narrow)
    def _():
        wait_all(par, NARROW)

    @pl.when((blk == nb - 1) & jnp.logical_not(narrow))
    def _():
        wait_all(par, WIN)


def _gather(offs, aff_t, aff, thr, nxt, jb, h2, cap):
    m = h2.shape[0]
    nb = m // RB
    step = _STEP_BLOCKS * RB
    col = lambda i, o: (0, 0)
    return pl.pallas_call(
        functools.partial(_gather_kernel, cap, nb),
        grid_spec=pltpu.PrefetchScalarGridSpec(
            num_scalar_prefetch=1,
            grid=(m // step,),
            in_specs=[
                pl.BlockSpec((N_EXPERTS, step), lambda i, o: (0, i)),
                pl.BlockSpec((step, N_EXPERTS), lambda i, o: (i, 0)),
                pl.BlockSpec((N_EXPERTS, 1), col), pl.BlockSpec((N_EXPERTS, 1), col),
                pl.BlockSpec((N_EXPERTS, 1), col),
                pl.BlockSpec((step, D_MODEL), lambda i, o: (i, 0)),
            ],
            out_specs=pl.BlockSpec(memory_space=pl.ANY),
            scratch_shapes=[pltpu.VMEM((2, N_EXPERTS * WIN, _XW), BF16),
                            pltpu.VMEM((N_EXPERTS, ROW_TILE, _XW), BF16),
                            pltpu.SemaphoreType.DMA((2, N_EXPERTS))],
        ),
        out_shape=jax.ShapeDtypeStruct((N_EXPERTS, cap + WIN, _XW), BF16),
        compiler_params=_params("arbitrary"),
        name="gather",
    )(offs, aff_t, aff, thr, nxt, jb, h2)


def _ffn_kernel(x_ref, wg_ref, wu_ref, wd_ref, o_ref, hid_ref):
    e = pl.program_id(0)
    x = x_ref[0, :, 0:D_MODEL]
    gcols = x_ref[0, :, D_MODEL:_XW].astype(F32)
    lane = lax.broadcasted_iota(jnp.int32, gcols.shape, 1)
    gate = jnp.sum(jnp.where((lane >= 3 * e) & (lane < 3 * e + 3), gcols, 0.0), axis=-1, keepdims=True)
    for f0 in range(0, D_EXPERT, 1024):
        g = _dot(x, wg_ref[0, :, f0:f0 + 1024])
        up = _dot(x, wu_ref[0, :, f0:f0 + 1024])
        hid_ref[:, f0:f0 + 1024] = (g * jax.nn.sigmoid(g) * up).astype(BF16)
    o_ref[0] = (_dot(hid_ref[...], wd_ref[0]) * gate).astype(BF16)


def _ffn(xe, cap, wg, wu, wd, tc):
    e = xe.shape[0]
    return pl.pallas_call(
        _ffn_kernel,
        grid=(e, cap // tc),
        in_specs=[
            pl.BlockSpec((1, tc, _XW), lambda e, j: (e, j, 0)),
            pl.BlockSpec((1, D_MODEL, D_EXPERT), lambda e, j: (e, 0, 0)),
            pl.BlockSpec((1, D_MODEL, D_EXPERT), lambda e, j: (e, 0, 0)),
            pl.BlockSpec((1, D_EXPERT, D_MODEL), lambda e, j: (e, 0, 0)),
        ],
        out_specs=pl.BlockSpec((1, tc, D_MODEL), lambda e, j: (e, j, 0)),
        out_shape=jax.ShapeDtypeStruct((e, cap, D_MODEL), BF16),
        scratch_shapes=[pltpu.VMEM((tc, D_EXPERT), BF16)],
        compiler_params=_params("parallel", "parallel"),
        name="experts",
    )(xe, wg, wu, wd)


def _combine_kernel(cap, nb, offs_ref, aff_t_ref, thr_ref, nxt_ref, jb_ref, x1_ref, g_ref, ye_ref, o_ref,
                    win_ref, sem_ref):
    blk = pl.program_id(0)
    par = blk % 2
    per = RB // SUB

    def window(row, width):
        first = pl.multiple_of(jnp.minimum(_floor_tile(row), cap - width), ROW_TILE)
        return first, row - first

    def copy(parity, e, first, width):
        return pltpu.make_async_copy(ye_ref.at[e, pl.ds(first, width)],
                                     win_ref.at[parity, pl.ds(e * width, width)], sem_ref.at[parity, e])

    def fetch(parity, rows, width):
        for e in range(N_EXPERTS):
            copy(parity, e, window(rows[e], width)[0], width).start()

    def wait_all(parity, width):
        for e in range(N_EXPERTS):
            copy(parity, e, 0, width).wait()

    def block_rows(b, sub):
        return [offs_ref[(b * per + sub) * N_EXPERTS + e] for e in range(N_EXPERTS)]

    def fetch_block(parity, b):
        _, b_narrow = _block_counts(offs_ref, b)

        @pl.when(b_narrow)
        def _():
            fetch(parity, block_rows(b, 0), NARROW)

        @pl.when(jnp.logical_not(b_narrow))
        def _():
            fetch(parity, block_rows(b, 0), WIN)

    @pl.when(blk == 0)
    def _():
        fetch_block(par, blk)

    @pl.when(blk + 1 < nb)
    def _():
        fetch_block(1 - par, jnp.minimum(blk + 1, nb - 1))

    sel = _block_selection(blk, aff_t_ref, thr_ref, nxt_ref, jb_ref)
    fits, narrow = _block_counts(offs_ref, blk)
    start = block_rows(blk, 0)

    def contribution(mask, rows, width):
        onehot = _slot_onehot(mask, [window(r, width)[1] for r in rows], width)
        return _dot_tn(onehot, win_ref[par, pl.ds(0, N_EXPERTS * width)])

    def finish(y):
        x = x1_ref[...] + y
        o_ref[...] = x * lax.rsqrt(jnp.mean(x * x, axis=-1, keepdims=True) + EPS) * g_ref[...]

    @pl.when(narrow)
    def _():
        wait_all(par, NARROW)
        finish(contribution(sel, start, NARROW))

    @pl.when(fits & jnp.logical_not(narrow))
    def _():
        wait_all(par, WIN)
        finish(contribution(sel, start, WIN))

    @pl.when(jnp.logical_not(fits))
    def _():
        wait_all(par, WIN)
        y = contribution(sel & _sub_mask(0), start, WIN)
        for sub in range(1, per):
            rows = block_rows(blk, sub)
            fetch(par, rows, WIN)
            wait_all(par, WIN)
            y = y + contribution(sel & _sub_mask(sub), rows, WIN)
        finish(y)


def _combine(offs, aff_t, thr, nxt, jb, x1, g, ye, cap):
    m = x1.shape[0]
    nb = m // RB
    col = lambda i, o: (0, 0)
    return pl.pallas_call(
        functools.partial(_combine_kernel, cap, nb),
        grid_spec=pltpu.PrefetchScalarGridSpec(
            num_scalar_prefetch=1,
            grid=(nb,),
            in_specs=[
                pl.BlockSpec((N_EXPERTS, RB), lambda i, o: (0, i)),
                pl.BlockSpec((N_EXPERTS, 1), col), pl.BlockSpec((N_EXPERTS, 1), col),
                pl.BlockSpec((N_EXPERTS, 1), col),
                pl.BlockSpec((RB, D_MODEL), lambda i, o: (i, 0)),
                pl.BlockSpec((1, D_MODEL), col),
                pl.BlockSpec(memory_space=pl.ANY),
            ],
            out_specs=pl.BlockSpec((RB, D_MODEL), lambda i, o: (i, 0)),
            scratch_shapes=[pltpu.VMEM((2, N_EXPERTS * WIN, D_MODEL), BF16),
                            pltpu.SemaphoreType.DMA((2, N_EXPERTS))],
        ),
        out_shape=jax.ShapeDtypeStruct((m, D_MODEL), F32),
        compiler_params=_params("arbitrary"),
        name="combine",
    )(offs, aff_t, thr, nxt, jb, x1, g, ye)


def _stack_w2(w2, lo):
    w = jnp.pad(w2, ((lo, _LR - GLA_GATE_RANK - lo), (0, 0)))
    hi = w.astype(BF16)
    lo_part = (w - hi.astype(F32)).astype(BF16)
    return jnp.concatenate([hi, hi, lo_part, jnp.zeros_like(hi)], axis=0)


_TILE_INPROJ = 256
_TILE_GLA = 1024
_TILE_MERGE = 512
_TILE_OFFSETS = 2048
_TILE_EXPERTS = 1024


def _trunk(x3, wts):
    n_seq, seq_len, _ = x3.shape
    m = n_seq * seq_len
    cap = max(1, EC_CAPACITY_FACTOR * m // N_EXPERTS)
    assert seq_len % _TILE_GLA == 0 and seq_len % _TILE_MERGE == 0 and m % _TILE_OFFSETS == 0
    assert cap >= WIN and cap % _TILE_EXPERTS == 0 and m % (_STEP_BLOCKS * RB) == 0
    x = x3.reshape(m, D_MODEL)
    qkvog, cb, u, gates, lr3 = _inproj(x, wts["norm_mix_g"], wts["w_main"], wts["w_lr3"], wts["b_merge"],
                                       tm=_TILE_INPROJ)
    og = _gla(qkvog, lr3, wts["w2f"], wts["b2f"], wts["w2b"], wts["b2b"], wts["gla_norm_g"],
              n_seq, seq_len, tb=_TILE_GLA)
    x1, h2, aff = _merge(x, og, cb, u, gates, wts["conv_w"], wts["w_gla_out"], wts["w_conv_out"], wts["w_out"],
                         wts["norm_ffn_g"], wts["w_router"], seq_len, tm=_TILE_MERGE)
    aff_t = aff.T
    thr, nxt, jb = _route(aff_t, cap)
    row = lambda a: a.reshape(1, N_EXPERTS)
    offs = _offsets(aff, row(thr), row(nxt), row(jb), to=_TILE_OFFSETS)
    offs = jnp.concatenate([offs, jnp.full((1, N_EXPERTS), cap, jnp.int32)], axis=0).reshape(-1)
    xe = _gather(offs, aff_t, aff, thr, nxt, jb, h2, cap)
    ye = _ffn(xe, cap, wts["w_exp_gate"], wts["w_exp_up"], wts["w_exp_down"], tc=_TILE_EXPERTS)
    out = _combine(offs, aff_t, thr, nxt, jb, x1, wts["norm_final_g"], ye, cap)
    return out.reshape(n_seq, seq_len, D_MODEL)


def kernel(x_prompt, x_sample, norm_mix_g, w_in, w_gk2_fwd, b_gk_fwd, w_gk2_bwd, b_gk_bwd, gla_norm_g, w_gla_out,
           conv_w, w_conv_out, b_merge, w_out, norm_ffn_g, w_router, w_exp_gate, w_exp_up, w_exp_down, norm_final_g):
    w = w_in[0]
    lr0 = _QKVOG
    w_lr = w[:, lr0:lr0 + _LR].astype(BF16)
    wts = {
        "norm_mix_g": norm_mix_g[0][None, :],
        "w_main": jnp.concatenate([w[:, :lr0], w[:, lr0 + _LR:]], axis=1).astype(BF16),
        "w_lr3": jnp.concatenate([w_lr, w_lr, w_lr, jnp.zeros_like(w_lr)], axis=1),
        "w2f": _stack_w2(w_gk2_fwd[0], 0),
        "w2b": _stack_w2(w_gk2_bwd[0], GLA_GATE_RANK),
        "b2f": b_gk_fwd[0][None, :],
        "b2b": b_gk_bwd[0][None, :],
        "gla_norm_g": gla_norm_g[0].reshape(1, GLA_V_WIDTH),
        "w_gla_out": w_gla_out[0].astype(BF16),
        "conv_w": conv_w[0],
        "w_conv_out": w_conv_out[0].astype(BF16),
        "b_merge": b_merge[0][None, :],
        "w_out": w_out[0].astype(BF16),
        "norm_ffn_g": norm_ffn_g[0][None, :],
        "w_router": w_router[0].astype(BF16),
        "w_exp_gate": w_exp_gate[0].astype(BF16),
        "w_exp_up": w_exp_up[0].astype(BF16),
        "w_exp_down": w_exp_down[0].astype(BF16),
        "norm_final_g": norm_final_g[None, :],
    }
    return (_trunk(x_prompt, wts), _trunk(x_sample, wts))
```

```python
import functools

import jax
import jax.numpy as jnp
from jax import lax
from jax.experimental import pallas as pl
from jax.experimental.pallas import tpu as pltpu

D_MODEL = 1024
GLA_HEADS = 4
GLA_DK = 128
GLA_DV = 256
GLA_QK_WIDTH = GLA_HEADS * GLA_DK
GLA_V_WIDTH = GLA_HEADS * GLA_DV
GLA_GATE_RANK = 16
GLA_GATE_NORM = 16.0
GLA_CHUNK = 64
GLA_BLOCK = 256
N_EXPERTS = 16
EC_CAPACITY_FACTOR = 2
D_EXPERT = 2 * D_MODEL
EPS = 1e-6

BF16 = jnp.bfloat16
F32 = jnp.float32

VMEM_LIMIT_BYTES = 56 * 1024 * 1024

_QKVOG = 2 * GLA_QK_WIDTH + 2 * GLA_V_WIDTH
_LR = 2 * GLA_GATE_RANK
_LR3 = 128


def _dot(a, b):
    return jnp.dot(a, b, preferred_element_type=F32)


def _dot_nt(a, b):
    return lax.dot_general(a, b, (((1,), (1,)), ((), ())), preferred_element_type=F32)


def _dot_tn(a, b):
    return lax.dot_general(a, b, (((0,), (0,)), ((), ())), preferred_element_type=F32)


def _params(*sem):
    return pltpu.CompilerParams(dimension_semantics=sem, vmem_limit_bytes=VMEM_LIMIT_BYTES)


def _inproj_kernel(x_ref, g_ref, w_ref, wlr_ref, bm_ref, qkvog_ref, cb_ref, u_ref, gates_ref, lr3_ref):
    x = x_ref[...]
    h = x * lax.rsqrt(jnp.mean(x * x, axis=-1, keepdims=True) + EPS) * g_ref[...]
    hb = h.astype(BF16)
    lr = _dot(hb, wlr_ref[...])
    hi = lr.astype(BF16)
    lo = (lr - hi.astype(F32)).astype(BF16)
    lane = lax.broadcasted_iota(jnp.int32, lr.shape, 1)
    lr3_ref[...] = jnp.where((lane >= _LR) & (lane < 2 * _LR), lo, hi)
    q = _dot(hb, w_ref[:, 0:GLA_QK_WIDTH]) * (GLA_DK ** -0.5)
    qkvog_ref[:, 0:GLA_QK_WIDTH] = q.astype(BF16)
    for c0 in range(GLA_QK_WIDTH, _QKVOG, 512):
        qkvog_ref[:, c0:c0 + 512] = _dot(hb, w_ref[:, c0:c0 + 512]).astype(BF16)
    o = _QKVOG
    for c0 in range(0, D_MODEL, 512):
        cb_ref[:, c0:c0 + 512] = _dot(hb, w_ref[:, o + c0:o + c0 + 512]).astype(BF16)
    for c0 in range(0, D_MODEL, 512):
        cc = _dot(hb, w_ref[:, o + D_MODEL + c0:o + D_MODEL + c0 + 512])
        cx = _dot(hb, w_ref[:, o + 2 * D_MODEL + c0:o + 2 * D_MODEL + c0 + 512])
        u_ref[:, c0:c0 + 512] = (cc * cx).astype(BF16)
    o = _QKVOG + 3 * D_MODEL
    for c0 in range(0, 2 * D_MODEL, 512):
        gm = _dot(hb, w_ref[:, o + c0:o + c0 + 512]) + bm_ref[:, c0:c0 + 512]
        gates_ref[:, c0:c0 + 512] = jax.nn.sigmoid(gm).astype(BF16)


def _inproj(x, g, w_main, w_lr3, b_merge, tm):
    m = x.shape[0]
    ncols = w_main.shape[1]
    const = lambda i: (0, 0)
    row = lambda i: (i, 0)
    return pl.pallas_call(
        _inproj_kernel,
        grid=(m // tm,),
        in_specs=[
            pl.BlockSpec((tm, D_MODEL), row),
            pl.BlockSpec((1, D_MODEL), const),
            pl.BlockSpec((D_MODEL, ncols), const, pipeline_mode=pl.Buffered(1)),
            pl.BlockSpec((D_MODEL, _LR3), const),
            pl.BlockSpec((1, 2 * D_MODEL), const),
        ],
        out_specs=[
            pl.BlockSpec((tm, _QKVOG), row),
            pl.BlockSpec((tm, D_MODEL), row),
            pl.BlockSpec((tm, D_MODEL), row),
            pl.BlockSpec((tm, 2 * D_MODEL), row),
            pl.BlockSpec((tm, _LR3), row),
        ],
        out_shape=[
            jax.ShapeDtypeStruct((m, _QKVOG), BF16),
            jax.ShapeDtypeStruct((m, D_MODEL), BF16),
            jax.ShapeDtypeStruct((m, D_MODEL), BF16),
            jax.ShapeDtypeStruct((m, 2 * D_MODEL), BF16),
            jax.ShapeDtypeStruct((m, _LR3), BF16),
        ],
        compiler_params=_params("parallel"),
        name="inproj",
    )(x, g, w_main, w_lr3, b_merge)


def _gla_body(rev, rows, lr3_ref, w2_ref, b2_ref, q_ref, k_ref, v_ref, st_ref):
    L = GLA_CHUNK
    tb = GLA_BLOCK
    nc = tb // L
    g = _dot(lr3_ref[rows, :], w2_ref[...]) + b2_ref[...]
    la = (jnp.minimum(g, 0.0) - jnp.log(1.0 + jnp.exp(-jnp.abs(g)))) * (1.0 / GLA_GATE_NORM)
    ri = lax.broadcasted_iota(jnp.int32, (tb, tb), 0)
    ci = lax.broadcasted_iota(jnp.int32, (tb, tb), 1)
    keep = ((ri // L) == (ci // L)) & ((ci >= ri) if rev else (ci <= ri))
    tri = keep.astype(BF16)
    la_hi = la.astype(BF16)
    la_lo = (la - la_hi.astype(F32)).astype(BF16)
    b = _dot(jnp.concatenate([tri, tri], axis=1), jnp.concatenate([la_hi, la_lo], axis=0))
    ref_row = L // 2 if rev else L // 2 - 1
    end_row = 0 if rev else L - 1
    b_end_rows = [b[c * L + end_row:c * L + end_row + 1, :] for c in range(nc)]
    b_ref_rows = [b[c * L + ref_row:c * L + ref_row + 1, :] for c in range(nc)]
    spread = lambda rws: jnp.concatenate([jnp.broadcast_to(x, (L, GLA_QK_WIDTH)) for x in rws], axis=0)
    b_ref = spread(b_ref_rows)
    a_end_rows = [jnp.exp(x) for x in b_end_rows]

    q = q_ref[rows, :].astype(F32)
    k = k_ref[rows, :].astype(F32)
    qe = q * jnp.exp(b - b_ref)
    ke = k * jnp.exp(b_ref - b)
    q_in = (qe * spread([jnp.exp(x) for x in b_ref_rows])).astype(BF16)
    k_out = (ke * spread([jnp.exp(e - r) for e, r in zip(b_end_rows, b_ref_rows)])).astype(BF16)
    qe = qe.astype(BF16)
    ke = ke.astype(BF16)
    order = range(nc - 1, -1, -1) if rev else range(nc)
    outs = []
    for h in range(GLA_HEADS):
        ks = slice(h * GLA_DK, (h + 1) * GLA_DK)
        v = v_ref[rows, h * GLA_DV:(h + 1) * GLA_DV]
        a = jnp.where(keep, _dot_nt(qe[:, ks], ke[:, ks]), 0.0).astype(BF16)
        o_intra = _dot(a, v)
        st = st_ref[h]
        o_inter = [None] * nc
        for c in order:
            cr = slice(c * L, (c + 1) * L)
            o_inter[c] = _dot_nt(q_in[cr, ks], st.astype(BF16))
            st = st * a_end_rows[c][:, ks] + _dot_tn(v[cr], k_out[cr, ks])
        st_ref[h] = st
        outs.append(o_intra + jnp.concatenate(o_inter, axis=0))
    return outs


def _gla_fwd_kernel(tb, lr3_ref, w2_ref, b2_ref, q_ref, k_ref, v_ref, o_ref, st_ref):
    @pl.when(pl.program_id(1) == 0)
    def _():
        st_ref[...] = jnp.zeros_like(st_ref)
    for sb in range(tb // GLA_BLOCK):
        rows = slice(sb * GLA_BLOCK, (sb + 1) * GLA_BLOCK)
        outs = _gla_body(False, rows, lr3_ref, w2_ref, b2_ref, q_ref, k_ref, v_ref, st_ref)
        for h in range(GLA_HEADS):
            o_ref[rows, h * GLA_DV:(h + 1) * GLA_DV] = outs[h]


def _gla_bwd_kernel(tb, lr3_ref, w2_ref, b2_ref, q_ref, k_ref, v_ref, of_ref, og_ref, gn_ref, o_ref, st_ref):
    @pl.when(pl.program_id(1) == 0)
    def _():
        st_ref[...] = jnp.zeros_like(st_ref)
    for sb in range(tb // GLA_BLOCK - 1, -1, -1):
        rows = slice(sb * GLA_BLOCK, (sb + 1) * GLA_BLOCK)
        outs = _gla_body(True, rows, lr3_ref, w2_ref, b2_ref, q_ref, k_ref, v_ref, st_ref)
        for h in range(GLA_HEADS):
            vs = slice(h * GLA_DV, (h + 1) * GLA_DV)
            o = of_ref[rows, vs] + outs[h]
            o = o * lax.rsqrt(jnp.mean(o * o, axis=-1, keepdims=True) + EPS) * gn_ref[:, vs]
            og = og_ref[rows, vs].astype(F32)
            o_ref[rows, vs] = (o * (og * jax.nn.sigmoid(og))).astype(BF16)


def _gla(qkvog, lr3, w2f, b2f, w2b, b2b, gn, n_seq, seq_len, tb):
    m = qkvog.shape[0]
    nb = seq_len // tb
    grid = (n_seq, nb)
    const = lambda b, n: (0, 0)

    def specs(rowf):
        return [
            pl.BlockSpec((tb, _LR3), lambda b, n: (rowf(b, n), 0)),
            pl.BlockSpec((_LR3, GLA_QK_WIDTH), const),
            pl.BlockSpec((1, GLA_QK_WIDTH), const),
            pl.BlockSpec((tb, GLA_QK_WIDTH), lambda b, n: (rowf(b, n), 0)),
            pl.BlockSpec((tb, GLA_QK_WIDTH), lambda b, n: (rowf(b, n), 1)),
            pl.BlockSpec((tb, GLA_V_WIDTH), lambda b, n: (rowf(b, n), 1)),
        ]

    fwd_row = lambda b, n: b * nb + n
    bwd_row = lambda b, n: b * nb + (nb - 1 - n)
    st = pltpu.VMEM((GLA_HEADS, GLA_DV, GLA_DK), F32)
    o_fwd = pl.pallas_call(
        functools.partial(_gla_fwd_kernel, tb),
        grid=grid,
        in_specs=specs(fwd_row),
        out_specs=pl.BlockSpec((tb, GLA_V_WIDTH), lambda b, n: (fwd_row(b, n), 0)),
        out_shape=jax.ShapeDtypeStruct((m, GLA_V_WIDTH), F32),
        scratch_shapes=[st],
        compiler_params=_params("parallel", "arbitrary"),
        name="gla_fwd",
    )(lr3, w2f, b2f, qkvog, qkvog, qkvog)
    return pl.pallas_call(
        functools.partial(_gla_bwd_kernel, tb),
        grid=grid,
        in_specs=specs(bwd_row) + [
            pl.BlockSpec((tb, GLA_V_WIDTH), lambda b, n: (bwd_row(b, n), 0)),
            pl.BlockSpec((tb, GLA_V_WIDTH), lambda b, n: (bwd_row(b, n), 2)),
            pl.BlockSpec((1, GLA_V_WIDTH), const),
        ],
        out_specs=pl.BlockSpec((tb, GLA_V_WIDTH), lambda b, n: (bwd_row(b, n), 0)),
        out_shape=jax.ShapeDtypeStruct((m, GLA_V_WIDTH), BF16),
        scratch_shapes=[st],
        compiler_params=_params("parallel", "arbitrary"),
        name="gla_bwd",
    )(lr3, w2b, b2b, qkvog, qkvog, qkvog, o_fwd, qkvog, gn)


_HALO = 16
_MERGE_ROWS = 256


def _merge_kernel(tm, seq_len, x_ref, og_ref, cb_ref, u_ref, up_ref, un_ref, gates_ref, cw_ref,
                  wa_ref, wb_ref, wo_ref, g2_ref, wr_ref, x1_ref, h2_ref, aff_ref):
    i = pl.program_id(0)
    u = u_ref[...].astype(F32)
    first = (i * tm) % seq_len == 0
    last = ((i + 1) * tm) % seq_len == 0
    prev_row = jnp.where(first, 0.0, up_ref[_HALO - 1:_HALO, :].astype(F32))
    next_row = jnp.where(last, 0.0, un_ref[0:1, :].astype(F32))
    r = lax.broadcasted_iota(jnp.int32, (tm, D_MODEL), 0)
    um1 = jnp.where(r == 0, prev_row, pltpu.roll(u, 1, 0))
    up1 = jnp.where(r == tm - 1, next_row, pltpu.roll(u, tm - 1, 0))
    hc = cw_ref[0:1, :] * um1 + cw_ref[1:2, :] * u + cw_ref[2:3, :] * up1
    cbh = (cb_ref[...].astype(F32) * hc).astype(BF16)
    for r0 in range(0, tm, _MERGE_ROWS):
        rows = slice(r0, r0 + _MERGE_ROWS)
        y_b = _dot(cbh[rows], wb_ref[...])
        y_a = _dot(og_ref[rows, :], wa_ref[...])
        mix = (gates_ref[rows, 0:D_MODEL].astype(F32) * y_a
               + gates_ref[rows, D_MODEL:2 * D_MODEL].astype(F32) * y_b)
        x1 = x_ref[rows, :] + _dot(mix.astype(BF16), wo_ref[...])
        x1_ref[rows, :] = x1
        h2 = (x1 * lax.rsqrt(jnp.mean(x1 * x1, axis=-1, keepdims=True) + EPS) * g2_ref[...]).astype(BF16)
        h2_ref[rows, :] = h2
        logits = _dot(h2, wr_ref[...])
        e = jnp.exp(logits - jnp.max(logits, axis=-1, keepdims=True))
        aff_ref[rows, :] = e / jnp.sum(e, axis=-1, keepdims=True)


def _merge(x, og, cb, u, gates, conv_w, wa, wb, wo, g2, wr, seq_len, tm):
    m = x.shape[0]
    const = lambda i: (0, 0)
    row = lambda i: (i, 0)
    hb = tm // _HALO
    nhb = m // _HALO
    sq = lambda: pl.BlockSpec((D_MODEL, D_MODEL), const)
    return pl.pallas_call(
        functools.partial(_merge_kernel, tm, seq_len),
        grid=(m // tm,),
        in_specs=[
            pl.BlockSpec((tm, D_MODEL), row),
            pl.BlockSpec((tm, D_MODEL), row),
            pl.BlockSpec((tm, D_MODEL), row),
            pl.BlockSpec((tm, D_MODEL), row),
            pl.BlockSpec((_HALO, D_MODEL), lambda i: (jnp.maximum(i * hb - 1, 0), 0)),
            pl.BlockSpec((_HALO, D_MODEL), lambda i: (jnp.minimum((i + 1) * hb, nhb - 1), 0)),
            pl.BlockSpec((tm, 2 * D_MODEL), row),
            pl.BlockSpec((3, D_MODEL), const),
            sq(), sq(), sq(),
            pl.BlockSpec((1, D_MODEL), const),
            pl.BlockSpec((D_MODEL, N_EXPERTS), const),
        ],
        out_specs=[
            pl.BlockSpec((tm, D_MODEL), row),
            pl.BlockSpec((tm, D_MODEL), row),
            pl.BlockSpec((tm, N_EXPERTS), row),
        ],
        out_shape=[
            jax.ShapeDtypeStruct((m, D_MODEL), F32),
            jax.ShapeDtypeStruct((m, D_MODEL), BF16),
            jax.ShapeDtypeStruct((m, N_EXPERTS), F32),
        ],
        compiler_params=_params("parallel"),
        name="merge",
    )(x, og, cb, u, u, u, gates, conv_w, wa, wb, wo, g2, wr)


RB = 256
SUB = 64
ROW_TILE = 16
WIN = SUB + ROW_TILE
NARROW = SUB
_XW = D_MODEL + 128
_IDX_BITS = 30
_MIN_NORMAL_BITS = 0x00800000


def _route_kernel(cap, n_tok_bits, aff_ref, thr_ref, nxt_ref, jb_ref):
    aff = aff_ref[...]
    idx = lax.broadcasted_iota(jnp.int32, aff.shape, 1)
    capf = jnp.float32(cap)

    def count(mask):
        return jnp.sum(mask.astype(F32), axis=1, keepdims=True)

    def as_f32(bits):
        return lax.bitcast_convert_type(bits, F32)

    def value_step(i, prefix):
        cand = prefix | jnp.left_shift(jnp.int32(1), _IDX_BITS - 1 - i)
        return jnp.where(count(aff >= as_f32(cand)) >= capf, cand, prefix)

    thr_bits = lax.fori_loop(0, _IDX_BITS, value_step, jnp.zeros((N_EXPERTS, 1), jnp.int32))
    thr = as_f32(thr_bits)
    nxt = as_f32(jnp.maximum(thr_bits + 1, _MIN_NORMAL_BITS))
    need = capf - count(aff >= nxt)
    tie = (aff >= thr) & (aff < nxt)

    def index_step(i, j):
        cand = j | jnp.left_shift(jnp.int32(1), n_tok_bits - 1 - i)
        return jnp.where(count(tie & (idx < cand)) < need, cand, j)

    thr_ref[...] = thr
    nxt_ref[...] = nxt
    jb_ref[...] = lax.fori_loop(0, n_tok_bits, index_step, jnp.zeros((N_EXPERTS, 1), jnp.int32))


def _route(aff_t, cap):
    n_tok = aff_t.shape[1]
    val = jax.ShapeDtypeStruct((N_EXPERTS, 1), F32)
    return pl.pallas_call(
        functools.partial(_route_kernel, cap, max(1, (n_tok - 1).bit_length())),
        out_shape=[val, val, jax.ShapeDtypeStruct((N_EXPERTS, 1), jnp.int32)],
        compiler_params=pltpu.CompilerParams(vmem_limit_bytes=VMEM_LIMIT_BYTES),
        name="route",
    )(aff_t)


def _selected(aff, tok, thr, nxt, jb):
    return (aff >= nxt) | ((aff >= thr) & (tok <= jb))


def _offsets_kernel(to, aff_ref, thr_ref, nxt_ref, jb_ref, offs_ref, carry_ref):
    i = pl.program_id(0)

    @pl.when(i == 0)
    def _():
        carry_ref[...] = jnp.zeros_like(carry_ref)

    aff = aff_ref[...]
    tok = i * to + lax.broadcasted_iota(jnp.int32, aff.shape, 0)
    sel = _selected(aff, tok, thr_ref[...], nxt_ref[...], jb_ref[...]).astype(BF16)
    ns = to // SUB
    grp = (lax.broadcasted_iota(jnp.int32, (ns, to), 1) // SUB == lax.broadcasted_iota(jnp.int32, (ns, to), 0))
    cnt = _dot(grp.astype(BF16), sel)
    before = (lax.broadcasted_iota(jnp.int32, (ns, ns), 1) < lax.broadcasted_iota(jnp.int32, (ns, ns), 0))
    offs_ref[...] = (carry_ref[...] + _dot(before.astype(BF16), cnt.astype(BF16))).astype(jnp.int32)
    carry_ref[...] += jnp.sum(cnt, axis=0, keepdims=True)


def _offsets(aff, thr_row, nxt_row, jb_row, to):
    m = aff.shape[0]
    const = lambda i: (0, 0)
    row = lambda: pl.BlockSpec((1, N_EXPERTS), const)
    return pl.pallas_call(
        functools.partial(_offsets_kernel, to),
        grid=(m // to,),
        in_specs=[pl.BlockSpec((to, N_EXPERTS), lambda i: (i, 0)), row(), row(), row()],
        out_specs=pl.BlockSpec((to // SUB, N_EXPERTS), lambda i: (i, 0)),
        out_shape=jax.ShapeDtypeStruct((m // SUB, N_EXPERTS), jnp.int32),
        scratch_shapes=[pltpu.VMEM((1, N_EXPERTS), F32)],
        compiler_params=_params("arbitrary"),
        name="offsets",
    )(aff, thr_row, nxt_row, jb_row)


def _block_selection(blk, aff_t_ref, thr_ref, nxt_ref, jb_ref):
    aff = aff_t_ref[...]
    tok = blk * RB + lax.broadcasted_iota(jnp.int32, aff.shape, 1)
    return _selected(aff, tok, thr_ref[...], nxt_ref[...], jb_ref[...])


def _floor_tile(x):
    return pl.multiple_of((x // ROW_TILE) * ROW_TILE, ROW_TILE)


def _slot_onehot(sel, shift, width):
    self = sel.astype(F32)
    before = (lax.broadcasted_iota(jnp.int32, (RB, RB), 0) < lax.broadcasted_iota(jnp.int32, (RB, RB), 1))
    rank = _dot(self.astype(BF16), before.astype(BF16)).astype(jnp.int32)
    slot = lax.broadcasted_iota(jnp.int32, (width, RB), 0)
    rows = [jnp.where(rank[e:e + 1, :] + shift[e] == slot, self[e:e + 1, :], 0.0) for e in range(N_EXPERTS)]
    return jnp.concatenate(rows, axis=0).astype(BF16)


def _block_counts(offs_ref, blk):
    per = RB // SUB
    start = [offs_ref[blk * per * N_EXPERTS + e] for e in range(N_EXPERTS)]
    end = [offs_ref[(blk + 1) * per * N_EXPERTS + e] for e in range(N_EXPERTS)]
    most = functools.reduce(jnp.maximum, [b - a for a, b in zip(start, end)])
    span = functools.reduce(jnp.maximum, [b - _floor_tile(a) for a, b in zip(start, end)])
    return most <= SUB, span <= NARROW


def _sub_mask(sub):
    return lax.broadcasted_iota(jnp.int32, (N_EXPERTS, RB), 1) // SUB == sub


_STEP_BLOCKS = 2


def _gather_kernel(cap, nb, offs_ref, aff_t_ref, aff_ref, thr_ref, nxt_ref, jb_ref, h2_ref, xe_ref,
                   stage_ref, carry_ref, sem_ref):
    for j in range(_STEP_BLOCKS):
        rows = pl.ds(j * RB, RB)
        _gather_block(cap, nb, pl.program_id(0) * _STEP_BLOCKS + j, j % 2, offs_ref, aff_t_ref.at[:, rows],
                      aff_ref.at[rows], thr_ref, nxt_ref, jb_ref, h2_ref.at[rows], xe_ref,
                      stage_ref, carry_ref, sem_ref)


def _gather_block(cap, nb, blk, par, offs_ref, aff_t_ref, aff_ref, thr_ref, nxt_ref, jb_ref, h2_ref, xe_ref,
                  stage_ref, carry_ref, sem_ref):
    per = RB // SUB

    def copy(parity, e, dst_row, width):
        return pltpu.make_async_copy(stage_ref.at[parity, pl.ds(e * width, width)],
                                     xe_ref.at[e, pl.ds(dst_row, width)], sem_ref.at[parity, e])

    def wait_all(parity, width):
        for e in range(N_EXPERTS):
            copy(parity, e, 0, width).wait()

    @pl.when(blk == 0)
    def _():
        carry_ref[...] = jnp.zeros_like(carry_ref)
        stage_ref[1] = jnp.zeros(stage_ref.shape[1:], BF16)
        for e in range(N_EXPERTS):
            copy(1, e, cap, WIN).start()
        wait_all(1, WIN)

    sel = _block_selection(blk, aff_t_ref, thr_ref, nxt_ref, jb_ref)
    aff = aff_ref[...]
    hi = aff.astype(BF16)
    r1 = aff - hi.astype(F32)
    mid = r1.astype(BF16)
    lo = (r1 - mid.astype(F32)).astype(BF16)
    er = lax.broadcasted_iota(jnp.int32, (N_EXPERTS, 128), 0)
    ec = lax.broadcasted_iota(jnp.int32, (N_EXPERTS, 128), 1)
    g3 = (_dot(hi, (ec == 3 * er).astype(BF16)) + _dot(mid, (ec == 3 * er + 1).astype(BF16))
          + _dot(lo, (ec == 3 * er + 2).astype(BF16)))
    src = jnp.concatenate([h2_ref[...], g3.astype(BF16)], axis=1)

    fits, narrow = _block_counts(offs_ref, blk)
    wide = fits & jnp.logical_not(narrow)
    _, prev_narrow = _block_counts(offs_ref, jnp.maximum(blk - 1, 0))

    def emit(mask, seg_lo, seg_hi, first, width):
        lo_rows = [offs_ref[seg_lo * N_EXPERTS + e] for e in range(N_EXPERTS)]
        hi_rows = [offs_ref[seg_hi * N_EXPERTS + e] for e in range(N_EXPERTS)]
        base = [_floor_tile(r) for r in lo_rows]
        onehot = _slot_onehot(mask, [r - b for r, b in zip(lo_rows, base)], width)
        stage_ref[par, pl.ds(0, N_EXPERTS * width)] = _dot(onehot, src).astype(BF16)
        for e in range(N_EXPERTS):
            head = pl.ds(e * width, ROW_TILE)
            stage_ref[par, head] = stage_ref[par, head] + carry_ref[e]
            nxt = _floor_tile(hi_rows[e]) - base[e]
            start = jnp.minimum(nxt, width - ROW_TILE)
            tail = stage_ref[par, pl.ds(pl.multiple_of(e * width + start, ROW_TILE), ROW_TILE)]
            carry_ref[e] = jnp.where(nxt < width, tail, jnp.zeros_like(tail))
        if first:
            @pl.when((blk > 0) & prev_narrow)
            def _():
                wait_all(1 - par, NARROW)

            @pl.when((blk > 0) & jnp.logical_not(prev_narrow))
            def _():
                wait_all(1 - par, WIN)
        for e in range(N_EXPERTS):
            copy(par, e, base[e], width).start()

    @pl.when(narrow)
    def _():
        emit(sel, blk * per, (blk + 1) * per, True, NARROW)

    @pl.when(wide)
    def _():
        emit(sel, blk * per, (blk + 1) * per, True, WIN)

    @pl.when(jnp.logical_not(fits))
    def _():
        for sub in range(per):
            if sub > 0:
                wait_all(par, WIN)
            emit(sel & _sub_mask(sub), blk * per + sub, blk * per + sub + 1, sub == 0, WIN)

    @pl.when((blk == nb - 1) & narrow)
    def _():
        wait_all(par, NARROW)

    @pl.when((blk == nb - 1) & jnp.logical_not(narrow))
    def _():
        wait_all(par, WIN)


def _gather(offs, aff_t, aff, thr, nxt, jb, h2, cap):
    m = h2.shape[0]
    nb = m // RB
    step = _STEP_BLOCKS * RB
    col = lambda i, o: (0, 0)
    return pl.pallas_call(
        functools.partial(_gather_kernel, cap, nb),
        grid_spec=pltpu.PrefetchScalarGridSpec(
            num_scalar_prefetch=1,
            grid=(m // step,),
            in_specs=[
                pl.BlockSpec((N_EXPERTS, step), lambda i, o: (0, i)),
                pl.BlockSpec((step, N_EXPERTS), lambda i, o: (i, 0)),
                pl.BlockSpec((N_EXPERTS, 1), col), pl.BlockSpec((N_EXPERTS, 1), col),
                pl.BlockSpec((N_EXPERTS, 1), col),
                pl.BlockSpec((step, D_MODEL), lambda i, o: (i, 0)),
            ],
            out_specs=pl.BlockSpec(memory_space=pl.ANY),
            scratch_shapes=[pltpu.VMEM((2, N_EXPERTS * WIN, _XW), BF16),
                            pltpu.VMEM((N_EXPERTS, ROW_TILE, _XW), BF16),
                            pltpu.SemaphoreType.DMA((2, N_EXPERTS))],
        ),
        out_shape=jax.ShapeDtypeStruct((N_EXPERTS, cap + WIN, _XW), BF16),
        compiler_params=_params("arbitrary"),
        name="gather",
    )(offs, aff_t, aff, thr, nxt, jb, h2)


def _ffn_kernel(x_ref, wg_ref, wu_ref, wd_ref, o_ref, hid_ref):
    e = pl.program_id(0)
    x = x_ref[0, :, 0:D_MODEL]
    gcols = x_ref[0, :, D_MODEL:_XW].astype(F32)
    lane = lax.broadcasted_iota(jnp.int32, gcols.shape, 1)
    gate = jnp.sum(jnp.where((lane >= 3 * e) & (lane < 3 * e + 3), gcols, 0.0), axis=-1, keepdims=True)
    for f0 in range(0, D_EXPERT, 1024):
        g = _dot(x, wg_ref[0, :, f0:f0 + 1024])
        up = _dot(x, wu_ref[0, :, f0:f0 + 1024])
        hid_ref[:, f0:f0 + 1024] = (g * jax.nn.sigmoid(g) * up).astype(BF16)
    o_ref[0] = (_dot(hid_ref[...], wd_ref[0]) * gate).astype(BF16)


def _ffn(xe, cap, wg, wu, wd, tc):
    e = xe.shape[0]
    return pl.pallas_call(
        _ffn_kernel,
        grid=(e, cap // tc),
        in_specs=[
            pl.BlockSpec((1, tc, _XW), lambda e, j: (e, j, 0)),
            pl.BlockSpec((1, D_MODEL, D_EXPERT), lambda e, j: (e, 0, 0)),
            pl.BlockSpec((1, D_MODEL, D_EXPERT), lambda e, j: (e, 0, 0)),
            pl.BlockSpec((1, D_EXPERT, D_MODEL), lambda e, j: (e, 0, 0)),
        ],
        out_specs=pl.BlockSpec((1, tc, D_MODEL), lambda e, j: (e, j, 0)),
        out_shape=jax.ShapeDtypeStruct((e, cap, D_MODEL), BF16),
        scratch_shapes=[pltpu.VMEM((tc, D_EXPERT), BF16)],
        compiler_params=_params("parallel", "parallel"),
        name="experts",
    )(xe, wg, wu, wd)


def _combine_kernel(cap, nb, offs_ref, aff_t_ref, thr_ref, nxt_ref, jb_ref, x1_ref, g_ref, ye_ref, o_ref,
                    win_ref, sem_ref):
    blk = pl.program_id(0)
    par = blk % 2
    per = RB // SUB

    def window(row, width):
        first = pl.multiple_of(jnp.minimum(_floor_tile(row), cap - width), ROW_TILE)
        return first, row - first

    def copy(parity, e, first, width):
        return pltpu.make_async_copy(ye_ref.at[e, pl.ds(first, width)],
                                     win_ref.at[parity, pl.ds(e * width, width)], sem_ref.at[parity, e])

    def fetch(parity, rows, width):
        for e in range(N_EXPERTS):
            copy(parity, e, window(rows[e], width)[0], width).start()

    def wait_all(parity, width):
        for e in range(N_EXPERTS):
            copy(parity, e, 0, width).wait()

    def block_rows(b, sub):
        return [offs_ref[(b * per + sub) * N_EXPERTS + e] for e in range(N_EXPERTS)]

    def fetch_block(parity, b):
        _, b_narrow = _block_counts(offs_ref, b)

        @pl.when(b_narrow)
        def _():
            fetch(parity, block_rows(b, 0), NARROW)

        @pl.when(jnp.logical_not(b_narrow))
        def _():
            fetch(parity, block_rows(b, 0), WIN)

    @pl.when(blk == 0)
    def _():
        fetch_block(par, blk)

    @pl.when(blk + 1 < nb)
    def _():
        fetch_block(1 - par, jnp.minimum(blk + 1, nb - 1))

    sel = _block_selection(blk, aff_t_ref, thr_ref, nxt_ref, jb_ref)
    fits, narrow = _block_counts(offs_ref, blk)
    start = block_rows(blk, 0)

    def contribution(mask, rows, width):
        onehot = _slot_onehot(mask, [window(r, width)[1] for r in rows], width)
        return _dot_tn(onehot, win_ref[par, pl.ds(0, N_EXPERTS * width)])

    def finish(y):
        x = x1_ref[...] + y
        o_ref[...] = x * lax.rsqrt(jnp.mean(x * x, axis=-1, keepdims=True) + EPS) * g_ref[...]

    @pl.when(narrow)
    def _():
        wait_all(par, NARROW)
        finish(contribution(sel, start, NARROW))

    @pl.when(fits & jnp.logical_not(narrow))
    def _():
        wait_all(par, WIN)
        finish(contribution(sel, start, WIN))

    @pl.when(jnp.logical_not(fits))
    def _():
        wait_all(par, WIN)
        y = contribution(sel & _sub_mask(0), start, WIN)
        for sub in range(1, per):
            rows = block_rows(blk, sub)
            fetch(par, rows, WIN)
            wait_all(par, WIN)
            y = y + contribution(sel & _sub_mask(sub), rows, WIN)
        finish(y)


def _combine(offs, aff_t, thr, nxt, jb, x1, g, ye, cap):
    m = x1.shape[0]
    nb = m // RB
    col = lambda i, o: (0, 0)
    return pl.pallas_call(
        functools.partial(_combine_kernel, cap, nb),
        grid_spec=pltpu.PrefetchScalarGridSpec(
            num_scalar_prefetch=1,
            grid=(nb,),
            in_specs=[
                pl.BlockSpec((N_EXPERTS, RB), lambda i, o: (0, i)),
                pl.BlockSpec((N_EXPERTS, 1), col), pl.BlockSpec((N_EXPERTS, 1), col),
                pl.BlockSpec((N_EXPERTS, 1), col),
                pl.BlockSpec((RB, D_MODEL), lambda i, o: (i, 0)),
                pl.BlockSpec((1, D_MODEL), col),
                pl.BlockSpec(memory_space=pl.ANY),
            ],
            out_specs=pl.BlockSpec((RB, D_MODEL), lambda i, o: (i, 0)),
            scratch_shapes=[pltpu.VMEM((2, N_EXPERTS * WIN, D_MODEL), BF16),
                            pltpu.SemaphoreType.DMA((2, N_EXPERTS))],
        ),
        out_shape=jax.ShapeDtypeStruct((m, D_MODEL), F32),
        compiler_params=_params("arbitrary"),
        name="combine",
    )(offs, aff_t, thr, nxt, jb, x1, g, ye)


def _stack_w2(w2, lo):
    w = jnp.pad(w2, ((lo, _LR - GLA_GATE_RANK - lo), (0, 0)))
    hi = w.astype(BF16)
    lo_part = (w - hi.astype(F32)).astype(BF16)
    return jnp.concatenate([hi, hi, lo_part, jnp.zeros_like(hi)], axis=0)


_TILE_INPROJ = 256
_TILE_GLA = 1024
_TILE_MERGE = 1024
_TILE_OFFSETS = 2048
_TILE_EXPERTS = 1024


def _trunk(x3, wts):
    n_seq, seq_len, _ = x3.shape
    m = n_seq * seq_len
    cap = max(1, EC_CAPACITY_FACTOR * m // N_EXPERTS)
    assert seq_len % _TILE_GLA == 0 and seq_len % _TILE_MERGE == 0 and m % _TILE_OFFSETS == 0
    assert cap >= WIN and cap % _TILE_EXPERTS == 0 and m % (_STEP_BLOCKS * RB) == 0
    x = x3.reshape(m, D_MODEL)
    qkvog, cb, u, gates, lr3 = _inproj(x, wts["norm_mix_g"], wts["w_main"], wts["w_lr3"], wts["b_merge"],
                                       tm=_TILE_INPROJ)
    og = _gla(qkvog, lr3, wts["w2f"], wts["b2f"], wts["w2b"], wts["b2b"], wts["gla_norm_g"],
              n_seq, seq_len, tb=_TILE_GLA)
    x1, h2, aff = _merge(x, og, cb, u, gates, wts["conv_w"], wts["w_gla_out"], wts["w_conv_out"], wts["w_out"],
                         wts["norm_ffn_g"], wts["w_router"], seq_len, tm=_TILE_MERGE)
    aff_t = aff.T
    thr, nxt, jb = _route(aff_t, cap)
    row = lambda a: a.reshape(1, N_EXPERTS)
    offs = _offsets(aff, row(thr), row(nxt), row(jb), to=_TILE_OFFSETS)
    offs = jnp.concatenate([offs, jnp.full((1, N_EXPERTS), cap, jnp.int32)], axis=0).reshape(-1)
    xe = _gather(offs, aff_t, aff, thr, nxt, jb, h2, cap)
    ye = _ffn(xe, cap, wts["w_exp_gate"], wts["w_exp_up"], wts["w_exp_down"], tc=_TILE_EXPERTS)
    out = _combine(offs, aff_t, thr, nxt, jb, x1, wts["norm_final_g"], ye, cap)
    return out.reshape(n_seq, seq_len, D_MODEL)


def kernel(x_prompt, x_sample, norm_mix_g, w_in, w_gk2_fwd, b_gk_fwd, w_gk2_bwd, b_gk_bwd, gla_norm_g, w_gla_out,
           conv_w, w_conv_out, b_merge, w_out, norm_ffn_g, w_router, w_exp_gate, w_exp_up, w_exp_down, norm_final_g):
    w = w_in[0]
    lr0 = _QKVOG
    w_lr = w[:, lr0:lr0 + _LR].astype(BF16)
    wts = {
        "norm_mix_g": norm_mix_g[0][None, :],
        "w_main": jnp.concatenate([w[:, :lr0], w[:, lr0 + _LR:]], axis=1).astype(BF16),
        "w_lr3": jnp.concatenate([w_lr, w_lr, w_lr, jnp.zeros_like(w_lr)], axis=1),
        "w2f": _stack_w2(w_gk2_fwd[0], 0),
        "w2b": _stack_w2(w_gk2_bwd[0], GLA_GATE_RANK),
        "b2f": b_gk_fwd[0][None, :],
        "b2b": b_gk_bwd[0][None, :],
        "gla_norm_g": gla_norm_g[0].reshape(1, GLA_V_WIDTH),
        "w_gla_out": w_gla_out[0].astype(BF16),
        "conv_w": conv_w[0],
        "w_conv_out": w_conv_out[0].astype(BF16),
        "b_merge": b_merge[0][None, :],
        "w_out": w_out[0].astype(BF16),
        "norm_ffn_g": norm_ffn_g[0][None, :],
        "w_router": w_router[0].astype(BF16),
        "w_exp_gate": w_exp_gate[0].astype(BF16),
        "w_exp_up": w_exp_up[0].astype(BF16),
        "w_exp_down": w_exp_down[0].astype(BF16),
        "norm_final_g": norm_final_g[None, :],
    }
    return (_trunk(x_prompt, wts), _trunk(x_sample, wts))
```

```python
import functools

import jax
import jax.numpy as jnp
from jax import lax
from jax.experimental import pallas as pl
from jax.experimental.pallas import tpu as pltpu

D_MODEL = 1024
GLA_HEADS = 4
GLA_DK = 128
GLA_DV = 256
GLA_QK_WIDTH = GLA_HEADS * GLA_DK
GLA_V_WIDTH = GLA_HEADS * GLA_DV
GLA_GATE_RANK = 16
GLA_GATE_NORM = 16.0
GLA_CHUNK = 64
GLA_BLOCK = 256
N_EXPERTS = 16
EC_CAPACITY_FACTOR = 2
D_EXPERT = 2 * D_MODEL
EPS = 1e-6

BF16 = jnp.bfloat16
F32 = jnp.float32

VMEM_LIMIT_BYTES = 56 * 1024 * 1024
ROW_TILE = 16

_QKVOG = 2 * GLA_QK_WIDTH + 2 * GLA_V_WIDTH
_LR = 2 * GLA_GATE_RANK
_LR3 = 128


def _dot(a, b):
    return jnp.dot(a, b, preferred_element_type=F32)


def _dot_nt(a, b):
    return lax.dot_general(a, b, (((1,), (1,)), ((), ())), preferred_element_type=F32)


def _dot_tn(a, b):
    return lax.dot_general(a, b, (((0,), (0,)), ((), ())), preferred_element_type=F32)


def _params(*sem):
    return pltpu.CompilerParams(dimension_semantics=sem, vmem_limit_bytes=VMEM_LIMIT_BYTES)


def _inproj_kernel(n_riders, x_ref, g_ref, w_ref, wlr_ref, bm_ref, *refs):
    rider_in = refs[:n_riders]
    qkvog_ref, cb_ref, u_ref, gates_ref, lr3_ref = refs[n_riders:n_riders + 5]
    rider_out = refs[n_riders + 5:]
    for src, dst in zip(rider_in, rider_out):
        dst[...] = src[...].astype(BF16)
    x = x_ref[...]
    h = x * lax.rsqrt(jnp.mean(x * x, axis=-1, keepdims=True) + EPS) * g_ref[...]
    hb = h.astype(BF16)
    lr = _dot(hb, wlr_ref[...])
    hi = lr.astype(BF16)
    lo = (lr - hi.astype(F32)).astype(BF16)
    lane = lax.broadcasted_iota(jnp.int32, lr.shape, 1)
    lr3_ref[...] = jnp.where((lane >= _LR) & (lane < 2 * _LR), lo, hi)
    q = _dot(hb, w_ref[:, 0:GLA_QK_WIDTH]) * (GLA_DK ** -0.5)
    qkvog_ref[:, 0:GLA_QK_WIDTH] = q.astype(BF16)
    for c0 in range(GLA_QK_WIDTH, _QKVOG, 512):
        qkvog_ref[:, c0:c0 + 512] = _dot(hb, w_ref[:, c0:c0 + 512]).astype(BF16)
    o = _QKVOG
    for c0 in range(0, D_MODEL, 512):
        cb_ref[:, c0:c0 + 512] = _dot(hb, w_ref[:, o + c0:o + c0 + 512]).astype(BF16)
    for c0 in range(0, D_MODEL, 512):
        cc = _dot(hb, w_ref[:, o + D_MODEL + c0:o + D_MODEL + c0 + 512])
        cx = _dot(hb, w_ref[:, o + 2 * D_MODEL + c0:o + 2 * D_MODEL + c0 + 512])
        u_ref[:, c0:c0 + 512] = (cc * cx).astype(BF16)
    o = _QKVOG + 3 * D_MODEL
    for c0 in range(0, 2 * D_MODEL, 512):
        gm = _dot(hb, w_ref[:, o + c0:o + c0 + 512]) + bm_ref[:, c0:c0 + 512]
        gates_ref[:, c0:c0 + 512] = jax.nn.sigmoid(gm).astype(BF16)


def _inproj(x, g, w_main, w_lr3, b_merge, tm, riders=()):
    m = x.shape[0]
    steps = m // tm
    ncols = w_main.shape[1]
    const = lambda i: (0, 0)
    row = lambda i: (i, 0)
    assert all(r.shape[0] % (steps * ROW_TILE) == 0 for r in riders)
    rider_specs = [pl.BlockSpec((r.shape[0] // steps, r.shape[1]), row) for r in riders]
    return pl.pallas_call(
        functools.partial(_inproj_kernel, len(riders)),
        grid=(steps,),
        in_specs=[
            pl.BlockSpec((tm, D_MODEL), row),
            pl.BlockSpec((1, D_MODEL), const),
            pl.BlockSpec((D_MODEL, ncols), const, pipeline_mode=pl.Buffered(1)),
            pl.BlockSpec((D_MODEL, _LR3), const),
            pl.BlockSpec((1, 2 * D_MODEL), const),
        ] + rider_specs,
        out_specs=[
            pl.BlockSpec((tm, _QKVOG), row),
            pl.BlockSpec((tm, D_MODEL), row),
            pl.BlockSpec((tm, D_MODEL), row),
            pl.BlockSpec((tm, 2 * D_MODEL), row),
            pl.BlockSpec((tm, _LR3), row),
        ] + rider_specs,
        out_shape=[
            jax.ShapeDtypeStruct((m, _QKVOG), BF16),
            jax.ShapeDtypeStruct((m, D_MODEL), BF16),
            jax.ShapeDtypeStruct((m, D_MODEL), BF16),
            jax.ShapeDtypeStruct((m, 2 * D_MODEL), BF16),
            jax.ShapeDtypeStruct((m, _LR3), BF16),
        ] + [jax.ShapeDtypeStruct(r.shape, BF16) for r in riders],
        compiler_params=_params("parallel"),
        name="inproj",
    )(x, g, w_main, w_lr3, b_merge, *riders)


def _gla_body(rev, rows, lr3_ref, w2_ref, b2_ref, q_ref, k_ref, v_ref, st_ref):
    L = GLA_CHUNK
    tb = GLA_BLOCK
    nc = tb // L
    g = _dot(lr3_ref[rows, :], w2_ref[...]) + b2_ref[...]
    la = (jnp.minimum(g, 0.0) - jnp.log(1.0 + jnp.exp(-jnp.abs(g)))) * (1.0 / GLA_GATE_NORM)
    ri = lax.broadcasted_iota(jnp.int32, (tb, tb), 0)
    ci = lax.broadcasted_iota(jnp.int32, (tb, tb), 1)
    keep = ((ri // L) == (ci // L)) & ((ci >= ri) if rev else (ci <= ri))
    tri = keep.astype(BF16)
    la_hi = la.astype(BF16)
    la_lo = (la - la_hi.astype(F32)).astype(BF16)
    b = _dot(jnp.concatenate([tri, tri], axis=1), jnp.concatenate([la_hi, la_lo], axis=0))
    ref_row = L // 2 if rev else L // 2 - 1
    end_row = 0 if rev else L - 1
    b_end_rows = [b[c * L + end_row:c * L + end_row + 1, :] for c in range(nc)]
    b_ref_rows = [b[c * L + ref_row:c * L + ref_row + 1, :] for c in range(nc)]
    spread = lambda rws: jnp.concatenate([jnp.broadcast_to(x, (L, GLA_QK_WIDTH)) for x in rws], axis=0)
    b_ref = spread(b_ref_rows)
    a_end_rows = [jnp.exp(x) for x in b_end_rows]

    q = q_ref[rows, :].astype(F32)
    k = k_ref[rows, :].astype(F32)
    qe = q * jnp.exp(b - b_ref)
    ke = k * jnp.exp(b_ref - b)
    q_in = (qe * spread([jnp.exp(x) for x in b_ref_rows])).astype(BF16)
    k_out = (ke * spread([jnp.exp(e - r) for e, r in zip(b_end_rows, b_ref_rows)])).astype(BF16)
    qe = qe.astype(BF16)
    ke = ke.astype(BF16)
    order = range(nc - 1, -1, -1) if rev else range(nc)
    outs = []
    for h in range(GLA_HEADS):
        ks = slice(h * GLA_DK, (h + 1) * GLA_DK)
        v = v_ref[rows, h * GLA_DV:(h + 1) * GLA_DV]
        a = jnp.where(keep, _dot_nt(qe[:, ks], ke[:, ks]), 0.0).astype(BF16)
        o_intra = _dot(a, v)
        st = st_ref[h]
        o_inter = [None] * nc
        for c in order:
            cr = slice(c * L, (c + 1) * L)
            o_inter[c] = _dot_nt(q_in[cr, ks], st.astype(BF16))
            st = st * a_end_rows[c][:, ks] + _dot_tn(v[cr], k_out[cr, ks])
        st_ref[h] = st
        outs.append(o_intra + jnp.concatenate(o_inter, axis=0))
    return outs


def _gla_fwd_kernel(tb, lr3_ref, w2_ref, b2_ref, q_ref, k_ref, v_ref, o_ref, st_ref):
    @pl.when(pl.program_id(1) == 0)
    def _():
        st_ref[...] = jnp.zeros_like(st_ref)
    for sb in range(tb // GLA_BLOCK):
        rows = slice(sb * GLA_BLOCK, (sb + 1) * GLA_BLOCK)
        outs = _gla_body(False, rows, lr3_ref, w2_ref, b2_ref, q_ref, k_ref, v_ref, st_ref)
        for h in range(GLA_HEADS):
            o_ref[rows, h * GLA_DV:(h + 1) * GLA_DV] = outs[h]


def _gla_bwd_kernel(tb, lr3_ref, w2_ref, b2_ref, q_ref, k_ref, v_ref, of_ref, og_ref, gn_ref, o_ref, st_ref):
    @pl.when(pl.program_id(1) == 0)
    def _():
        st_ref[...] = jnp.zeros_like(st_ref)
    for sb in range(tb // GLA_BLOCK - 1, -1, -1):
        rows = slice(sb * GLA_BLOCK, (sb + 1) * GLA_BLOCK)
        outs = _gla_body(True, rows, lr3_ref, w2_ref, b2_ref, q_ref, k_ref, v_ref, st_ref)
        for h in range(GLA_HEADS):
            vs = slice(h * GLA_DV, (h + 1) * GLA_DV)
            o = of_ref[rows, vs] + outs[h]
            o = o * lax.rsqrt(jnp.mean(o * o, axis=-1, keepdims=True) + EPS) * gn_ref[:, vs]
            og = og_ref[rows, vs].astype(F32)
            o_ref[rows, vs] = (o * (og * jax.nn.sigmoid(og))).astype(BF16)


def _gla(qkvog, lr3, w2f, b2f, w2b, b2b, gn, n_seq, seq_len, tb):
    m = qkvog.shape[0]
    nb = seq_len // tb
    grid = (n_seq, nb)
    const = lambda b, n: (0, 0)

    def specs(rowf):
        return [
            pl.BlockSpec((tb, _LR3), lambda b, n: (rowf(b, n), 0)),
            pl.BlockSpec((_LR3, GLA_QK_WIDTH), const),
            pl.BlockSpec((1, GLA_QK_WIDTH), const),
            pl.BlockSpec((tb, GLA_QK_WIDTH), lambda b, n: (rowf(b, n), 0)),
            pl.BlockSpec((tb, GLA_QK_WIDTH), lambda b, n: (rowf(b, n), 1)),
            pl.BlockSpec((tb, GLA_V_WIDTH), lambda b, n: (rowf(b, n), 1)),
        ]

    fwd_row = lambda b, n: b * nb + n
    bwd_row = lambda b, n: b * nb + (nb - 1 - n)
    st = pltpu.VMEM((GLA_HEADS, GLA_DV, GLA_DK), F32)
    o_fwd = pl.pallas_call(
        functools.partial(_gla_fwd_kernel, tb),
        grid=grid,
        in_specs=specs(fwd_row),
        out_specs=pl.BlockSpec((tb, GLA_V_WIDTH), lambda b, n: (fwd_row(b, n), 0)),
        out_shape=jax.ShapeDtypeStruct((m, GLA_V_WIDTH), F32),
        scratch_shapes=[st],
        compiler_params=_params("parallel", "arbitrary"),
        name="gla_fwd",
    )(lr3, w2f, b2f, qkvog, qkvog, qkvog)
    return pl.pallas_call(
        functools.partial(_gla_bwd_kernel, tb),
        grid=grid,
        in_specs=specs(bwd_row) + [
            pl.BlockSpec((tb, GLA_V_WIDTH), lambda b, n: (bwd_row(b, n), 0)),
            pl.BlockSpec((tb, GLA_V_WIDTH), lambda b, n: (bwd_row(b, n), 2)),
            pl.BlockSpec((1, GLA_V_WIDTH), const),
        ],
        out_specs=pl.BlockSpec((tb, GLA_V_WIDTH), lambda b, n: (bwd_row(b, n), 0)),
        out_shape=jax.ShapeDtypeStruct((m, GLA_V_WIDTH), BF16),
        scratch_shapes=[st],
        compiler_params=_params("parallel", "arbitrary"),
        name="gla_bwd",
    )(lr3, w2b, b2b, qkvog, qkvog, qkvog, o_fwd, qkvog, gn)


_HALO = 16
_MERGE_ROWS = 256


def _merge_kernel(tm, seq_len, x_ref, og_ref, cb_ref, u_ref, up_ref, un_ref, gates_ref, cw_ref,
                  wa_ref, wb_ref, wo_ref, g2_ref, wr_ref, x1_ref, h2_ref, aff_ref):
    i = pl.program_id(0)
    u = u_ref[...].astype(F32)
    first = (i * tm) % seq_len == 0
    last = ((i + 1) * tm) % seq_len == 0
    prev_row = jnp.where(first, 0.0, up_ref[_HALO - 1:_HALO, :].astype(F32))
    next_row = jnp.where(last, 0.0, un_ref[0:1, :].astype(F32))
    r = lax.broadcasted_iota(jnp.int32, (tm, D_MODEL), 0)
    um1 = jnp.where(r == 0, prev_row, pltpu.roll(u, 1, 0))
    up1 = jnp.where(r == tm - 1, next_row, pltpu.roll(u, tm - 1, 0))
    hc = cw_ref[0:1, :] * um1 + cw_ref[1:2, :] * u + cw_ref[2:3, :] * up1
    cbh = (cb_ref[...].astype(F32) * hc).astype(BF16)
    for r0 in range(0, tm, _MERGE_ROWS):
        rows = slice(r0, r0 + _MERGE_ROWS)
        y_b = _dot(cbh[rows], wb_ref[...])
        y_a = _dot(og_ref[rows, :], wa_ref[...])
        mix = (gates_ref[rows, 0:D_MODEL].astype(F32) * y_a
               + gates_ref[rows, D_MODEL:2 * D_MODEL].astype(F32) * y_b)
        x1 = x_ref[rows, :] + _dot(mix.astype(BF16), wo_ref[...])
        x1_ref[rows, :] = x1
        h2 = (x1 * lax.rsqrt(jnp.mean(x1 * x1, axis=-1, keepdims=True) + EPS) * g2_ref[...]).astype(BF16)
        h2_ref[rows, :] = h2
        logits = _dot(h2, wr_ref[...])
        e = jnp.exp(logits - jnp.max(logits, axis=-1, keepdims=True))
        aff_ref[rows, :] = e / jnp.sum(e, axis=-1, keepdims=True)


def _merge(x, og, cb, u, gates, conv_w, wa, wb, wo, g2, wr, seq_len, tm):
    m = x.shape[0]
    const = lambda i: (0, 0)
    row = lambda i: (i, 0)
    hb = tm // _HALO
    nhb = m // _HALO
    sq = lambda: pl.BlockSpec((D_MODEL, D_MODEL), const)
    return pl.pallas_call(
        functools.partial(_merge_kernel, tm, seq_len),
        grid=(m // tm,),
        in_specs=[
            pl.BlockSpec((tm, D_MODEL), row),
            pl.BlockSpec((tm, D_MODEL), row),
            pl.BlockSpec((tm, D_MODEL), row),
            pl.BlockSpec((tm, D_MODEL), row),
            pl.BlockSpec((_HALO, D_MODEL), lambda i: (jnp.maximum(i * hb - 1, 0), 0)),
            pl.BlockSpec((_HALO, D_MODEL), lambda i: (jnp.minimum((i + 1) * hb, nhb - 1), 0)),
            pl.BlockSpec((tm, 2 * D_MODEL), row),
            pl.BlockSpec((3, D_MODEL), const),
            sq(), sq(), sq(),
            pl.BlockSpec((1, D_MODEL), const),
            pl.BlockSpec((D_MODEL, N_EXPERTS), const),
        ],
        out_specs=[
            pl.BlockSpec((tm, D_MODEL), row),
            pl.BlockSpec((tm, D_MODEL), row),
            pl.BlockSpec((tm, N_EXPERTS), row),
        ],
        out_shape=[
            jax.ShapeDtypeStruct((m, D_MODEL), F32),
            jax.ShapeDtypeStruct((m, D_MODEL), BF16),
            jax.ShapeDtypeStruct((m, N_EXPERTS), F32),
        ],
        compiler_params=_params("parallel"),
        name="merge",
    )(x, og, cb, u, u, u, gates, conv_w, wa, wb, wo, g2, wr)


RB = 256
SUB = 64
WIN = SUB + ROW_TILE
NARROW = SUB
_XW = D_MODEL + 128
_IDX_BITS = 30
_MIN_NORMAL_BITS = 0x00800000


def _route_kernel(cap, n_tok_bits, aff_ref, thr_ref, nxt_ref, jb_ref):
    aff = aff_ref[...]
    idx = lax.broadcasted_iota(jnp.int32, aff.shape, 1)
    capf = jnp.float32(cap)

    def count(mask):
        return jnp.sum(mask.astype(F32), axis=1, keepdims=True)

    def as_f32(bits):
        return lax.bitcast_convert_type(bits, F32)

    def value_step(i, prefix):
        cand = prefix | jnp.left_shift(jnp.int32(1), _IDX_BITS - 1 - i)
        return jnp.where(count(aff >= as_f32(cand)) >= capf, cand, prefix)

    thr_bits = lax.fori_loop(0, _IDX_BITS, value_step, jnp.zeros((N_EXPERTS, 1), jnp.int32))
    thr = as_f32(thr_bits)
    nxt = as_f32(jnp.maximum(thr_bits + 1, _MIN_NORMAL_BITS))
    need = capf - count(aff >= nxt)
    tie = (aff >= thr) & (aff < nxt)

    def index_step(i, j):
        cand = j | jnp.left_shift(jnp.int32(1), n_tok_bits - 1 - i)
        return jnp.where(count(tie & (idx < cand)) < need, cand, j)

    thr_ref[...] = thr
    nxt_ref[...] = nxt
    jb_ref[...] = lax.fori_loop(0, n_tok_bits, index_step, jnp.zeros((N_EXPERTS, 1), jnp.int32))


def _route(aff_t, cap):
    n_tok = aff_t.shape[1]
    val = jax.ShapeDtypeStruct((N_EXPERTS, 1), F32)
    return pl.pallas_call(
        functools.partial(_route_kernel, cap, max(1, (n_tok - 1).bit_length())),
        out_shape=[val, val, jax.ShapeDtypeStruct((N_EXPERTS, 1), jnp.int32)],
        compiler_params=pltpu.CompilerParams(vmem_limit_bytes=VMEM_LIMIT_BYTES),
        name="route",
    )(aff_t)


def _selected(aff, tok, thr, nxt, jb):
    return (aff >= nxt) | ((aff >= thr) & (tok <= jb))


def _offsets_kernel(to, aff_ref, thr_ref, nxt_ref, jb_ref, offs_ref, carry_ref):
    i = pl.program_id(0)

    @pl.when(i == 0)
    def _():
        carry_ref[...] = jnp.zeros_like(carry_ref)

    aff = aff_ref[...]
    tok = i * to + lax.broadcasted_iota(jnp.int32, aff.shape, 0)
    sel = _selected(aff, tok, thr_ref[...], nxt_ref[...], jb_ref[...]).astype(BF16)
    ns = to // SUB
    grp = (lax.broadcasted_iota(jnp.int32, (ns, to), 1) // SUB == lax.broadcasted_iota(jnp.int32, (ns, to), 0))
    cnt = _dot(grp.astype(BF16), sel)
    before = (lax.broadcasted_iota(jnp.int32, (ns, ns), 1) < lax.broadcasted_iota(jnp.int32, (ns, ns), 0))
    offs_ref[...] = (carry_ref[...] + _dot(before.astype(BF16), cnt.astype(BF16))).astype(jnp.int32)
    carry_ref[...] += jnp.sum(cnt, axis=0, keepdims=True)


def _offsets(aff, thr_row, nxt_row, jb_row, to):
    m = aff.shape[0]
    const = lambda i: (0, 0)
    row = lambda: pl.BlockSpec((1, N_EXPERTS), const)
    return pl.pallas_call(
        functools.partial(_offsets_kernel, to),
        grid=(m // to,),
        in_specs=[pl.BlockSpec((to, N_EXPERTS), lambda i: (i, 0)), row(), row(), row()],
        out_specs=pl.BlockSpec((to // SUB, N_EXPERTS), lambda i: (i, 0)),
        out_shape=jax.ShapeDtypeStruct((m // SUB, N_EXPERTS), jnp.int32),
        scratch_shapes=[pltpu.VMEM((1, N_EXPERTS), F32)],
        compiler_params=_params("arbitrary"),
        name="offsets",
    )(aff, thr_row, nxt_row, jb_row)


def _block_selection(blk, aff_t_ref, thr_ref, nxt_ref, jb_ref):
    aff = aff_t_ref[...]
    tok = blk * RB + lax.broadcasted_iota(jnp.int32, aff.shape, 1)
    return _selected(aff, tok, thr_ref[...], nxt_ref[...], jb_ref[...])


def _floor_tile(x):
    return pl.multiple_of((x // ROW_TILE) * ROW_TILE, ROW_TILE)


def _slot_onehot(sel, shift, width):
    self = sel.astype(F32)
    before = (lax.broadcasted_iota(jnp.int32, (RB, RB), 0) < lax.broadcasted_iota(jnp.int32, (RB, RB), 1))
    rank = _dot(self.astype(BF16), before.astype(BF16)).astype(jnp.int32)
    slot = lax.broadcasted_iota(jnp.int32, (width, RB), 0)
    rows = [jnp.where(rank[e:e + 1, :] + shift[e] == slot, self[e:e + 1, :], 0.0) for e in range(N_EXPERTS)]
    return jnp.concatenate(rows, axis=0).astype(BF16)


def _block_counts(offs_ref, blk):
    per = RB // SUB
    start = [offs_ref[blk * per * N_EXPERTS + e] for e in range(N_EXPERTS)]
    end = [offs_ref[(blk + 1) * per * N_EXPERTS + e] for e in range(N_EXPERTS)]
    most = functools.reduce(jnp.maximum, [b - a for a, b in zip(start, end)])
    span = functools.reduce(jnp.maximum, [b - _floor_tile(a) for a, b in zip(start, end)])
    return most <= SUB, span <= NARROW


def _sub_mask(sub):
    return lax.broadcasted_iota(jnp.int32, (N_EXPERTS, RB), 1) // SUB == sub


_STEP_BLOCKS = 2


def _gather_kernel(cap, nb, offs_ref, aff_t_ref, aff_ref, thr_ref, nxt_ref, jb_ref, h2_ref, xe_ref,
                   stage_ref, carry_ref, sem_ref):
    for j in range(_STEP_BLOCKS):
        rows = pl.ds(j * RB, RB)
        _gather_block(cap, nb, pl.program_id(0) * _STEP_BLOCKS + j, j % 2, offs_ref, aff_t_ref.at[:, rows],
                      aff_ref.at[rows], thr_ref, nxt_ref, jb_ref, h2_ref.at[rows], xe_ref,
                      stage_ref, carry_ref, sem_ref)


def _gather_block(cap, nb, blk, par, offs_ref, aff_t_ref, aff_ref, thr_ref, nxt_ref, jb_ref, h2_ref, xe_ref,
                  stage_ref, carry_ref, sem_ref):
    per = RB // SUB

    def copy(parity, e, dst_row, width):
        return pltpu.make_async_copy(stage_ref.at[parity, pl.ds(e * width, width)],
                                     xe_ref.at[e, pl.ds(dst_row, width)], sem_ref.at[parity, e])

    def wait_all(parity, width):
        for e in range(N_EXPERTS):
            copy(parity, e, 0, width).wait()

    @pl.when(blk == 0)
    def _():
        carry_ref[...] = jnp.zeros_like(carry_ref)
        stage_ref[1] = jnp.zeros(stage_ref.shape[1:], BF16)
        for e in range(N_EXPERTS):
            copy(1, e, cap, WIN).start()
        wait_all(1, WIN)

    sel = _block_selection(blk, aff_t_ref, thr_ref, nxt_ref, jb_ref)
    aff = aff_ref[...]
    hi = aff.astype(BF16)
    r1 = aff - hi.astype(F32)
    mid = r1.astype(BF16)
    lo = (r1 - mid.astype(F32)).astype(BF16)
    er = lax.broadcasted_iota(jnp.int32, (N_EXPERTS, 128), 0)
    ec = lax.broadcasted_iota(jnp.int32, (N_EXPERTS, 128), 1)
    g3 = (_dot(hi, (ec == 3 * er).astype(BF16)) + _dot(mid, (ec == 3 * er + 1).astype(BF16))
          + _dot(lo, (ec == 3 * er + 2).astype(BF16)))
    src = jnp.concatenate([h2_ref[...], g3.astype(BF16)], axis=1)

    fits, narrow = _block_counts(offs_ref, blk)
    wide = fits & jnp.logical_not(narrow)
    _, prev_narrow = _block_counts(offs_ref, jnp.maximum(blk - 1, 0))

    def emit(mask, seg_lo, seg_hi, first, width):
        lo_rows = [offs_ref[seg_lo * N_EXPERTS + e] for e in range(N_EXPERTS)]
        hi_rows = [offs_ref[seg_hi * N_EXPERTS + e] for e in range(N_EXPERTS)]
        base = [_floor_tile(r) for r in lo_rows]
        onehot = _slot_onehot(mask, [r - b for r, b in zip(lo_rows, base)], width)
        stage_ref[par, pl.ds(0, N_EXPERTS * width)] = _dot(onehot, src).astype(BF16)
        for e in range(N_EXPERTS):
            head = pl.ds(e * width, ROW_TILE)
            stage_ref[par, head] = stage_ref[par, head] + carry_ref[e]
            nxt = _floor_tile(hi_rows[e]) - base[e]
            start = jnp.minimum(nxt, width - ROW_TILE)
            tail = stage_ref[par, pl.ds(pl.multiple_of(e * width + start, ROW_TILE), ROW_TILE)]
            carry_ref[e] = jnp.where(nxt < width, tail, jnp.zeros_like(tail))
        if first:
            @pl.when((blk > 0) & prev_narrow)
            def _():
                wait_all(1 - par, NARROW)

            @pl.when((blk > 0) & jnp.logical_not(prev_narrow))
            def _():
                wait_all(1 - par, WIN)
        for e in range(N_EXPERTS):
            copy(par, e, base[e], width).start()

    @pl.when(narrow)
    def _():
        emit(sel, blk * per, (blk + 1) * per, True, NARROW)

    @pl.when(wide)
    def _():
        emit(sel, blk * per, (blk + 1) * per, True, WIN)

    @pl.when(jnp.logical_not(fits))
    def _():
        for sub in range(per):
            if sub > 0:
                wait_all(par, WIN)
            emit(sel & _sub_mask(sub), blk * per + sub, blk * per + sub + 1, sub == 0, WIN)

    @pl.when((blk == nb - 1) & narrow)
    def _():
        wait_all(par, NARROW)

    @pl.when((blk == nb - 1) & jnp.logical_not(narrow))
    def _():
        wait_all(par, WIN)


def _gather(offs, aff_t, aff, thr, nxt, jb, h2, cap):
    m = h2.shape[0]
    nb = m // RB
    step = _STEP_BLOCKS * RB
    col = lambda i, o: (0, 0)
    return pl.pallas_call(
        functools.partial(_gather_kernel, cap, nb),
        grid_spec=pltpu.PrefetchScalarGridSpec(
            num_scalar_prefetch=1,
            grid=(m // step,),
            in_specs=[
                pl.BlockSpec((N_EXPERTS, step), lambda i, o: (0, i)),
                pl.BlockSpec((step, N_EXPERTS), lambda i, o: (i, 0)),
                pl.BlockSpec((N_EXPERTS, 1), col), pl.BlockSpec((N_EXPERTS, 1), col),
                pl.BlockSpec((N_EXPERTS, 1), col),
                pl.BlockSpec((step, D_MODEL), lambda i, o: (i, 0)),
            ],
            out_specs=pl.BlockSpec(memory_space=pl.ANY),
            scratch_shapes=[pltpu.VMEM((2, N_EXPERTS * WIN, _XW), BF16),
                            pltpu.VMEM((N_EXPERTS, ROW_TILE, _XW), BF16),
                            pltpu.SemaphoreType.DMA((2, N_EXPERTS))],
        ),
        out_shape=jax.ShapeDtypeStruct((N_EXPERTS, cap + WIN, _XW), BF16),
        compiler_params=_params("arbitrary"),
        name="gather",
    )(offs, aff_t, aff, thr, nxt, jb, h2)


def _ffn_kernel(x_ref, wg_ref, wu_ref, wd_ref, o_ref, hid_ref):
    e = pl.program_id(0)
    x = x_ref[0, :, 0:D_MODEL]
    gcols = x_ref[0, :, D_MODEL:_XW].astype(F32)
    lane = lax.broadcasted_iota(jnp.int32, gcols.shape, 1)
    gate = jnp.sum(jnp.where((lane >= 3 * e) & (lane < 3 * e + 3), gcols, 0.0), axis=-1, keepdims=True)
    for f0 in range(0, D_EXPERT, 1024):
        g = _dot(x, wg_ref[0, :, f0:f0 + 1024])
        up = _dot(x, wu_ref[0, :, f0:f0 + 1024])
        hid_ref[:, f0:f0 + 1024] = (g * jax.nn.sigmoid(g) * up).astype(BF16)
    o_ref[0] = (_dot(hid_ref[...], wd_ref[0]) * gate).astype(BF16)


def _ffn(xe, cap, wg, wu, wd, tc):
    e = xe.shape[0]
    return pl.pallas_call(
        _ffn_kernel,
        grid=(e, cap // tc),
        in_specs=[
            pl.BlockSpec((1, tc, _XW), lambda e, j: (e, j, 0)),
            pl.BlockSpec((1, D_MODEL, D_EXPERT), lambda e, j: (e, 0, 0)),
            pl.BlockSpec((1, D_MODEL, D_EXPERT), lambda e, j: (e, 0, 0)),
            pl.BlockSpec((1, D_EXPERT, D_MODEL), lambda e, j: (e, 0, 0)),
        ],
        out_specs=pl.BlockSpec((1, tc, D_MODEL), lambda e, j: (e, j, 0)),
        out_shape=jax.ShapeDtypeStruct((e, cap, D_MODEL), BF16),
        scratch_shapes=[pltpu.VMEM((tc, D_EXPERT), BF16)],
        compiler_params=_params("parallel", "parallel"),
        name="experts",
    )(xe, wg, wu, wd)


def _combine_kernel(cap, nb, offs_ref, aff_t_ref, thr_ref, nxt_ref, jb_ref, x1_ref, g_ref, ye_ref, o_ref,
                    win_ref, sem_ref):
    blk = pl.program_id(0)
    par = blk % 2
    per = RB // SUB

    def window(row, width):
        first = pl.multiple_of(jnp.minimum(_floor_tile(row), cap - width), ROW_TILE)
        return first, row - first

    def copy(parity, e, first, width):
        return pltpu.make_async_copy(ye_ref.at[e, pl.ds(first, width)],
                                     win_ref.at[parity, pl.ds(e * width, width)], sem_ref.at[parity, e])

    def fetch(parity, rows, width):
        for e in range(N_EXPERTS):
            copy(parity, e, window(rows[e], width)[0], width).start()

    def wait_all(parity, width):
        for e in range(N_EXPERTS):
            copy(parity, e, 0, width).wait()

    def block_rows(b, sub):
        return [offs_ref[(b * per + sub) * N_EXPERTS + e] for e in range(N_EXPERTS)]

    def fetch_block(parity, b):
        _, b_narrow = _block_counts(offs_ref, b)

        @pl.when(b_narrow)
        def _():
            fetch(parity, block_rows(b, 0), NARROW)

        @pl.when(jnp.logical_not(b_narrow))
        def _():
            fetch(parity, block_rows(b, 0), WIN)

    @pl.when(blk == 0)
    def _():
        fetch_block(par, blk)

    @pl.when(blk + 1 < nb)
    def _():
        fetch_block(1 - par, jnp.minimum(blk + 1, nb - 1))

    sel = _block_selection(blk, aff_t_ref, thr_ref, nxt_ref, jb_ref)
    fits, narrow = _block_counts(offs_ref, blk)
    start = block_rows(blk, 0)

    def contribution(mask, rows, width):
        onehot = _slot_onehot(mask, [window(r, width)[1] for r in rows], width)
        return _dot_tn(onehot, win_ref[par, pl.ds(0, N_EXPERTS * width)])

    def finish(y):
        x = x1_ref[...] + y
        o_ref[...] = x * lax.rsqrt(jnp.mean(x * x, axis=-1, keepdims=True) + EPS) * g_ref[...]

    @pl.when(narrow)
    def _():
        wait_all(par, NARROW)
        finish(contribution(sel, start, NARROW))

    @pl.when(fits & jnp.logical_not(narrow))
    def _():
        wait_all(par, WIN)
        finish(contribution(sel, start, WIN))

    @pl.when(jnp.logical_not(fits))
    def _():
        wait_all(par, WIN)
        y = contribution(sel & _sub_mask(0), start, WIN)
        for sub in range(1, per):
            rows = block_rows(blk, sub)
            fetch(par, rows, WIN)
            wait_all(par, WIN)
            y = y + contribution(sel & _sub_mask(sub), rows, WIN)
        finish(y)


def _combine(offs, aff_t, thr, nxt, jb, x1, g, ye, cap):
    m = x1.shape[0]
    nb = m // RB
    col = lambda i, o: (0, 0)
    return pl.pallas_call(
        functools.partial(_combine_kernel, cap, nb),
        grid_spec=pltpu.PrefetchScalarGridSpec(
            num_scalar_prefetch=1,
            grid=(nb,),
            in_specs=[
                pl.BlockSpec((N_EXPERTS, RB), lambda i, o: (0, i)),
                pl.BlockSpec((N_EXPERTS, 1), col), pl.BlockSpec((N_EXPERTS, 1), col),
                pl.BlockSpec((N_EXPERTS, 1), col),
                pl.BlockSpec((RB, D_MODEL), lambda i, o: (i, 0)),
                pl.BlockSpec((1, D_MODEL), col),
                pl.BlockSpec(memory_space=pl.ANY),
            ],
            out_specs=pl.BlockSpec((RB, D_MODEL), lambda i, o: (i, 0)),
            scratch_shapes=[pltpu.VMEM((2, N_EXPERTS * WIN, D_MODEL), BF16),
                            pltpu.SemaphoreType.DMA((2, N_EXPERTS))],
        ),
        out_shape=jax.ShapeDtypeStruct((m, D_MODEL), F32),
        compiler_params=_params("arbitrary"),
        name="combine",
    )(offs, aff_t, thr, nxt, jb, x1, g, ye)


def _stack_w2(w2, lo):
    w = jnp.pad(w2, ((lo, _LR - GLA_GATE_RANK - lo), (0, 0)))
    hi = w.astype(BF16)
    lo_part = (w - hi.astype(F32)).astype(BF16)
    return jnp.concatenate([hi, hi, lo_part, jnp.zeros_like(hi)], axis=0)


_TILE_INPROJ = 256
_TILE_GLA = 1024
_TILE_MERGE = 1024
_TILE_OFFSETS = 2048
_TILE_EXPERTS = 1024


def _trunk(x3, wts, expert_w):
    n_seq, seq_len, _ = x3.shape
    m = n_seq * seq_len
    cap = max(1, EC_CAPACITY_FACTOR * m // N_EXPERTS)
    assert seq_len % _TILE_GLA == 0 and seq_len % _TILE_MERGE == 0 and m % _TILE_OFFSETS == 0
    assert cap >= WIN and cap % _TILE_EXPERTS == 0 and m % (_STEP_BLOCKS * RB) == 0
    x = x3.reshape(m, D_MODEL)
    riders = tuple(w.reshape(-1, w.shape[-1]) for w in expert_w if w.dtype != BF16)
    qkvog, cb, u, gates, lr3, *cast = _inproj(x, wts["norm_mix_g"], wts["w_main"], wts["w_lr3"], wts["b_merge"],
                                              tm=_TILE_INPROJ, riders=riders)
    if cast:
        expert_w = tuple(c.reshape(w.shape) for c, w in zip(cast, expert_w))
    og = _gla(qkvog, lr3, wts["w2f"], wts["b2f"], wts["w2b"], wts["b2b"], wts["gla_norm_g"],
              n_seq, seq_len, tb=_TILE_GLA)
    x1, h2, aff = _merge(x, og, cb, u, gates, wts["conv_w"], wts["w_gla_out"], wts["w_conv_out"], wts["w_out"],
                         wts["norm_ffn_g"], wts["w_router"], seq_len, tm=_TILE_MERGE)
    aff_t = aff.T
    thr, nxt, jb = _route(aff_t, cap)
    row = lambda a: a.reshape(1, N_EXPERTS)
    offs = _offsets(aff, row(thr), row(nxt), row(jb), to=_TILE_OFFSETS)
    offs = jnp.concatenate([offs, jnp.full((1, N_EXPERTS), cap, jnp.int32)], axis=0).reshape(-1)
    xe = _gather(offs, aff_t, aff, thr, nxt, jb, h2, cap)
    ye = _ffn(xe, cap, *expert_w, tc=_TILE_EXPERTS)
    out = _combine(offs, aff_t, thr, nxt, jb, x1, wts["norm_final_g"], ye, cap)
    return out.reshape(n_seq, seq_len, D_MODEL), expert_w


def kernel(x_prompt, x_sample, norm_mix_g, w_in, w_gk2_fwd, b_gk_fwd, w_gk2_bwd, b_gk_bwd, gla_norm_g, w_gla_out,
           conv_w, w_conv_out, b_merge, w_out, norm_ffn_g, w_router, w_exp_gate, w_exp_up, w_exp_down, norm_final_g):
    w = w_in[0]
    lr0 = _QKVOG
    w_lr = w[:, lr0:lr0 + _LR].astype(BF16)
    wts = {
        "norm_mix_g": norm_mix_g[0][None, :],
        "w_main": jnp.concatenate([w[:, :lr0], w[:, lr0 + _LR:]], axis=1).astype(BF16),
        "w_lr3": jnp.concatenate([w_lr, w_lr, w_lr, jnp.zeros_like(w_lr)], axis=1),
        "w2f": _stack_w2(w_gk2_fwd[0], 0),
        "w2b": _stack_w2(w_gk2_bwd[0], GLA_GATE_RANK),
        "b2f": b_gk_fwd[0][None, :],
        "b2b": b_gk_bwd[0][None, :],
        "gla_norm_g": gla_norm_g[0].reshape(1, GLA_V_WIDTH),
        "w_gla_out": w_gla_out[0].astype(BF16),
        "conv_w": conv_w[0],
        "w_conv_out": w_conv_out[0].astype(BF16),
        "b_merge": b_merge[0][None, :],
        "w_out": w_out[0].astype(BF16),
        "norm_ffn_g": norm_ffn_g[0][None, :],
        "w_router": w_router[0].astype(BF16),
        "norm_final_g": norm_final_g[None, :],
    }
    y_prompt, expert_w = _trunk(x_prompt, wts, (w_exp_gate[0], w_exp_up[0], w_exp_down[0]))
    y_sample, _ = _trunk(x_sample, wts, expert_w)
    return (y_prompt, y_sample)
```

```python
import functools

import jax
import jax.numpy as jnp
from jax import lax
from jax.experimental import pallas as pl
from jax.experimental.pallas import tpu as pltpu

D_MODEL = 1024
GLA_HEADS = 4
GLA_DK = 128
GLA_DV = 256
GLA_QK_WIDTH = GLA_HEADS * GLA_DK
GLA_V_WIDTH = GLA_HEADS * GLA_DV
GLA_GATE_RANK = 16
GLA_GATE_NORM = 16.0
GLA_CHUNK = 64
GLA_BLOCK = 256
N_EXPERTS = 16
EC_CAPACITY_FACTOR = 2
D_EXPERT = 2 * D_MODEL
EPS = 1e-6

BF16 = jnp.bfloat16
F32 = jnp.float32

VMEM_LIMIT_BYTES = 56 * 1024 * 1024
ROW_TILE = 16

_QKVOG = 2 * GLA_QK_WIDTH + 2 * GLA_V_WIDTH
_LR = 2 * GLA_GATE_RANK
_LR3 = 128


def _dot(a, b):
    return jnp.dot(a, b, preferred_element_type=F32)


def _dot_nt(a, b):
    return lax.dot_general(a, b, (((1,), (1,)), ((), ())), preferred_element_type=F32)


def _dot_tn(a, b):
    return lax.dot_general(a, b, (((0,), (0,)), ((), ())), preferred_element_type=F32)


def _params(*sem):
    return pltpu.CompilerParams(dimension_semantics=sem, vmem_limit_bytes=VMEM_LIMIT_BYTES)


def _inproj_kernel(n_riders, x_ref, g_ref, w_ref, wlr_ref, bm_ref, *refs):
    rider_in = refs[:n_riders]
    qkvog_ref, cb_ref, u_ref, gates_ref, lr3_ref = refs[n_riders:n_riders + 5]
    rider_out = refs[n_riders + 5:]
    for src, dst in zip(rider_in, rider_out):
        dst[...] = src[...].astype(BF16)
    x = x_ref[...]
    h = x * lax.rsqrt(jnp.mean(x * x, axis=-1, keepdims=True) + EPS) * g_ref[...]
    hb = h.astype(BF16)
    lr = _dot(hb, wlr_ref[...])
    hi = lr.astype(BF16)
    lo = (lr - hi.astype(F32)).astype(BF16)
    lane = lax.broadcasted_iota(jnp.int32, lr.shape, 1)
    lr3_ref[...] = jnp.where((lane >= _LR) & (lane < 2 * _LR), lo, hi)
    q = _dot(hb, w_ref[:, 0:GLA_QK_WIDTH]) * (GLA_DK ** -0.5)
    qkvog_ref[:, 0:GLA_QK_WIDTH] = q.astype(BF16)
    for c0 in range(GLA_QK_WIDTH, _QKVOG, 512):
        qkvog_ref[:, c0:c0 + 512] = _dot(hb, w_ref[:, c0:c0 + 512]).astype(BF16)
    o = _QKVOG
    for c0 in range(0, D_MODEL, 512):
        cb_ref[:, c0:c0 + 512] = _dot(hb, w_ref[:, o + c0:o + c0 + 512]).astype(BF16)
    for c0 in range(0, D_MODEL, 512):
        cc = _dot(hb, w_ref[:, o + D_MODEL + c0:o + D_MODEL + c0 + 512])
        cx = _dot(hb, w_ref[:, o + 2 * D_MODEL + c0:o + 2 * D_MODEL + c0 + 512])
        u_ref[:, c0:c0 + 512] = (cc * cx).astype(BF16)
    o = _QKVOG + 3 * D_MODEL
    for c0 in range(0, 2 * D_MODEL, 512):
        gm = _dot(hb, w_ref[:, o + c0:o + c0 + 512]) + bm_ref[:, c0:c0 + 512]
        gates_ref[:, c0:c0 + 512] = jax.nn.sigmoid(gm).astype(BF16)


def _inproj(x, g, w_main, w_lr3, b_merge, tm, riders=()):
    m = x.shape[0]
    steps = m // tm
    ncols = w_main.shape[1]
    const = lambda i: (0, 0)
    row = lambda i: (i, 0)
    assert all(r.shape[0] % (steps * ROW_TILE) == 0 for r in riders)
    rider_specs = [pl.BlockSpec((r.shape[0] // steps, r.shape[1]), row) for r in riders]
    return pl.pallas_call(
        functools.partial(_inproj_kernel, len(riders)),
        grid=(steps,),
        in_specs=[
            pl.BlockSpec((tm, D_MODEL), row),
            pl.BlockSpec((1, D_MODEL), const),
            pl.BlockSpec((D_MODEL, ncols), const, pipeline_mode=pl.Buffered(1)),
            pl.BlockSpec((D_MODEL, _LR3), const),
            pl.BlockSpec((1, 2 * D_MODEL), const),
        ] + rider_specs,
        out_specs=[
            pl.BlockSpec((tm, _QKVOG), row),
            pl.BlockSpec((tm, D_MODEL), row),
            pl.BlockSpec((tm, D_MODEL), row),
            pl.BlockSpec((tm, 2 * D_MODEL), row),
            pl.BlockSpec((tm, _LR3), row),
        ] + rider_specs,
        out_shape=[
            jax.ShapeDtypeStruct((m, _QKVOG), BF16),
            jax.ShapeDtypeStruct((m, D_MODEL), BF16),
            jax.ShapeDtypeStruct((m, D_MODEL), BF16),
            jax.ShapeDtypeStruct((m, 2 * D_MODEL), BF16),
            jax.ShapeDtypeStruct((m, _LR3), BF16),
        ] + [jax.ShapeDtypeStruct(r.shape, BF16) for r in riders],
        compiler_params=_params("parallel"),
        name="inproj",
    )(x, g, w_main, w_lr3, b_merge, *riders)


def _gla_body(rev, rows, lr3_ref, w2_ref, b2_ref, q_ref, k_ref, v_ref, st_ref):
    L = GLA_CHUNK
    tb = GLA_BLOCK
    nc = tb // L
    g = _dot(lr3_ref[rows, :], w2_ref[...]) + b2_ref[...]
    la = (jnp.minimum(g, 0.0) - jnp.log(1.0 + jnp.exp(-jnp.abs(g)))) * (1.0 / GLA_GATE_NORM)
    ri = lax.broadcasted_iota(jnp.int32, (tb, tb), 0)
    ci = lax.broadcasted_iota(jnp.int32, (tb, tb), 1)
    keep = ((ri // L) == (ci // L)) & ((ci >= ri) if rev else (ci <= ri))
    tri = keep.astype(BF16)
    la_hi = la.astype(BF16)
    la_lo = (la - la_hi.astype(F32)).astype(BF16)
    b = _dot(jnp.concatenate([tri, tri], axis=1), jnp.concatenate([la_hi, la_lo], axis=0))
    ref_row = L // 2 if rev else L // 2 - 1
    end_row = 0 if rev else L - 1
    b_end_rows = [b[c * L + end_row:c * L + end_row + 1, :] for c in range(nc)]
    b_ref_rows = [b[c * L + ref_row:c * L + ref_row + 1, :] for c in range(nc)]
    spread = lambda rws: jnp.concatenate([jnp.broadcast_to(x, (L, GLA_QK_WIDTH)) for x in rws], axis=0)
    b_ref = spread(b_ref_rows)
    a_end_rows = [jnp.exp(x) for x in b_end_rows]

    q = q_ref[rows, :].astype(F32)
    k = k_ref[rows, :].astype(F32)
    qe = q * jnp.exp(b - b_ref)
    ke = k * jnp.exp(b_ref - b)
    q_in = (qe * spread([jnp.exp(x) for x in b_ref_rows])).astype(BF16)
    k_out = (ke * spread([jnp.exp(e - r) for e, r in zip(b_end_rows, b_ref_rows)])).astype(BF16)
    qe = qe.astype(BF16)
    ke = ke.astype(BF16)
    order = range(nc - 1, -1, -1) if rev else range(nc)
    outs = []
    for h in range(GLA_HEADS):
        ks = slice(h * GLA_DK, (h + 1) * GLA_DK)
        v = v_ref[rows, h * GLA_DV:(h + 1) * GLA_DV]
        a = jnp.where(keep, _dot_nt(qe[:, ks], ke[:, ks]), 0.0).astype(BF16)
        o_intra = _dot(a, v)
        st = st_ref[h]
        o_inter = [None] * nc
        for c in order:
            cr = slice(c * L, (c + 1) * L)
            o_inter[c] = _dot_nt(q_in[cr, ks], st.astype(BF16))
            st = st * a_end_rows[c][:, ks] + _dot_tn(v[cr], k_out[cr, ks])
        st_ref[h] = st
        outs.append(o_intra + jnp.concatenate(o_inter, axis=0))
    return outs


def _gla_fwd_kernel(tb, lr3_ref, w2_ref, b2_ref, q_ref, k_ref, v_ref, o_ref, st_ref):
    @pl.when(pl.program_id(1) == 0)
    def _():
        st_ref[...] = jnp.zeros_like(st_ref)
    for sb in range(tb // GLA_BLOCK):
        rows = slice(sb * GLA_BLOCK, (sb + 1) * GLA_BLOCK)
        outs = _gla_body(False, rows, lr3_ref, w2_ref, b2_ref, q_ref, k_ref, v_ref, st_ref)
        for h in range(GLA_HEADS):
            o_ref[rows, h * GLA_DV:(h + 1) * GLA_DV] = outs[h]


def _gla_bwd_kernel(tb, lr3_ref, w2_ref, b2_ref, q_ref, k_ref, v_ref, of_ref, og_ref, gn_ref, o_ref, st_ref):
    @pl.when(pl.program_id(1) == 0)
    def _():
        st_ref[...] = jnp.zeros_like(st_ref)
    for sb in range(tb // GLA_BLOCK - 1, -1, -1):
        rows = slice(sb * GLA_BLOCK, (sb + 1) * GLA_BLOCK)
        outs = _gla_body(True, rows, lr3_ref, w2_ref, b2_ref, q_ref, k_ref, v_ref, st_ref)
        for h in range(GLA_HEADS):
            vs = slice(h * GLA_DV, (h + 1) * GLA_DV)
            o = of_ref[rows, vs] + outs[h]
            o = o * lax.rsqrt(jnp.mean(o * o, axis=-1, keepdims=True) + EPS) * gn_ref[:, vs]
            og = og_ref[rows, vs].astype(F32)
            o_ref[rows, vs] = (o * (og * jax.nn.sigmoid(og))).astype(BF16)


def _gla(qkvog, lr3, w2f, b2f, w2b, b2b, gn, n_seq, seq_len, tb):
    m = qkvog.shape[0]
    nb = seq_len // tb
    grid = (n_seq, nb)
    const = lambda b, n: (0, 0)

    def specs(rowf):
        return [
            pl.BlockSpec((tb, _LR3), lambda b, n: (rowf(b, n), 0)),
            pl.BlockSpec((_LR3, GLA_QK_WIDTH), const),
            pl.BlockSpec((1, GLA_QK_WIDTH), const),
            pl.BlockSpec((tb, GLA_QK_WIDTH), lambda b, n: (rowf(b, n), 0)),
            pl.BlockSpec((tb, GLA_QK_WIDTH), lambda b, n: (rowf(b, n), 1)),
            pl.BlockSpec((tb, GLA_V_WIDTH), lambda b, n: (rowf(b, n), 1)),
        ]

    fwd_row = lambda b, n: b * nb + n
    bwd_row = lambda b, n: b * nb + (nb - 1 - n)
    st = pltpu.VMEM((GLA_HEADS, GLA_DV, GLA_DK), F32)
    o_fwd = pl.pallas_call(
        functools.partial(_gla_fwd_kernel, tb),
        grid=grid,
        in_specs=specs(fwd_row),
        out_specs=pl.BlockSpec((tb, GLA_V_WIDTH), lambda b, n: (fwd_row(b, n), 0)),
        out_shape=jax.ShapeDtypeStruct((m, GLA_V_WIDTH), F32),
        scratch_shapes=[st],
        compiler_params=_params("parallel", "arbitrary"),
        name="gla_fwd",
    )(lr3, w2f, b2f, qkvog, qkvog, qkvog)
    return pl.pallas_call(
        functools.partial(_gla_bwd_kernel, tb),
        grid=grid,
        in_specs=specs(bwd_row) + [
            pl.BlockSpec((tb, GLA_V_WIDTH), lambda b, n: (bwd_row(b, n), 0)),
            pl.BlockSpec((tb, GLA_V_WIDTH), lambda b, n: (bwd_row(b, n), 2)),
            pl.BlockSpec((1, GLA_V_WIDTH), const),
        ],
        out_specs=pl.BlockSpec((tb, GLA_V_WIDTH), lambda b, n: (bwd_row(b, n), 0)),
        out_shape=jax.ShapeDtypeStruct((m, GLA_V_WIDTH), BF16),
        scratch_shapes=[st],
        compiler_params=_params("parallel", "arbitrary"),
        name="gla_bwd",
    )(lr3, w2b, b2b, qkvog, qkvog, qkvog, o_fwd, qkvog, gn)


_HALO = 16
_MERGE_ROWS = 256


def _merge_kernel(tm, seq_len, x_ref, og_ref, cb_ref, u_ref, up_ref, un_ref, gates_ref, cw_ref,
                  wa_ref, wb_ref, wo_ref, g2_ref, wr_ref, x1_ref, h2_ref, aff_ref, aff_t_ref):
    i = pl.program_id(0)
    u = u_ref[...].astype(F32)
    first = (i * tm) % seq_len == 0
    last = ((i + 1) * tm) % seq_len == 0
    prev_row = jnp.where(first, 0.0, up_ref[_HALO - 1:_HALO, :].astype(F32))
    next_row = jnp.where(last, 0.0, un_ref[0:1, :].astype(F32))
    r = lax.broadcasted_iota(jnp.int32, (tm, D_MODEL), 0)
    um1 = jnp.where(r == 0, prev_row, pltpu.roll(u, 1, 0))
    up1 = jnp.where(r == tm - 1, next_row, pltpu.roll(u, tm - 1, 0))
    hc = cw_ref[0:1, :] * um1 + cw_ref[1:2, :] * u + cw_ref[2:3, :] * up1
    cbh = (cb_ref[...].astype(F32) * hc).astype(BF16)
    for r0 in range(0, tm, _MERGE_ROWS):
        rows = slice(r0, r0 + _MERGE_ROWS)
        y_b = _dot(cbh[rows], wb_ref[...])
        y_a = _dot(og_ref[rows, :], wa_ref[...])
        mix = (gates_ref[rows, 0:D_MODEL].astype(F32) * y_a
               + gates_ref[rows, D_MODEL:2 * D_MODEL].astype(F32) * y_b)
        x1 = x_ref[rows, :] + _dot(mix.astype(BF16), wo_ref[...])
        x1_ref[rows, :] = x1
        h2 = (x1 * lax.rsqrt(jnp.mean(x1 * x1, axis=-1, keepdims=True) + EPS) * g2_ref[...]).astype(BF16)
        h2_ref[rows, :] = h2
        logits = _dot(h2, wr_ref[...])
        e = jnp.exp(logits - jnp.max(logits, axis=-1, keepdims=True))
        aff = e / jnp.sum(e, axis=-1, keepdims=True)
        aff_ref[rows, :] = aff
        aff_t_ref[:, rows] = aff.T


def _merge(x, og, cb, u, gates, conv_w, wa, wb, wo, g2, wr, seq_len, tm):
    m = x.shape[0]
    const = lambda i: (0, 0)
    row = lambda i: (i, 0)
    hb = tm // _HALO
    nhb = m // _HALO
    sq = lambda: pl.BlockSpec((D_MODEL, D_MODEL), const)
    return pl.pallas_call(
        functools.partial(_merge_kernel, tm, seq_len),
        grid=(m // tm,),
        in_specs=[
            pl.BlockSpec((tm, D_MODEL), row),
            pl.BlockSpec((tm, D_MODEL), row),
            pl.BlockSpec((tm, D_MODEL), row),
            pl.BlockSpec((tm, D_MODEL), row),
            pl.BlockSpec((_HALO, D_MODEL), lambda i: (jnp.maximum(i * hb - 1, 0), 0)),
            pl.BlockSpec((_HALO, D_MODEL), lambda i: (jnp.minimum((i + 1) * hb, nhb - 1), 0)),
            pl.BlockSpec((tm, 2 * D_MODEL), row),
            pl.BlockSpec((3, D_MODEL), const),
            sq(), sq(), sq(),
            pl.BlockSpec((1, D_MODEL), const),
            pl.BlockSpec((D_MODEL, N_EXPERTS), const),
        ],
        out_specs=[
            pl.BlockSpec((tm, D_MODEL), row),
            pl.BlockSpec((tm, D_MODEL), row),
            pl.BlockSpec((tm, N_EXPERTS), row),
            pl.BlockSpec((N_EXPERTS, tm), lambda i: (0, i)),
        ],
        out_shape=[
            jax.ShapeDtypeStruct((m, D_MODEL), F32),
            jax.ShapeDtypeStruct((m, D_MODEL), BF16),
            jax.ShapeDtypeStruct((m, N_EXPERTS), F32),
            jax.ShapeDtypeStruct((N_EXPERTS, m), F32),
        ],
        compiler_params=_params("parallel"),
        name="merge",
    )(x, og, cb, u, u, u, gates, conv_w, wa, wb, wo, g2, wr)


RB = 256
SUB = 64
WIN = SUB + ROW_TILE
NARROW = SUB
_XW = D_MODEL + 128
_IDX_BITS = 30
_MIN_NORMAL_BITS = 0x00800000


def _route_kernel(cap, n_tok_bits, aff_ref, thr_ref, nxt_ref, jb_ref):
    aff = aff_ref[...]
    idx = lax.broadcasted_iota(jnp.int32, aff.shape, 1)
    capf = jnp.float32(cap)

    def count(mask):
        return jnp.sum(mask.astype(F32), axis=1, keepdims=True)

    def as_f32(bits):
        return lax.bitcast_convert_type(bits, F32)

    def value_step(i, prefix):
        cand = prefix | jnp.left_shift(jnp.int32(1), _IDX_BITS - 1 - i)
        return jnp.where(count(aff >= as_f32(cand)) >= capf, cand, prefix)

    thr_bits = lax.fori_loop(0, _IDX_BITS, value_step, jnp.zeros((N_EXPERTS, 1), jnp.int32))
    thr = as_f32(thr_bits)
    nxt = as_f32(jnp.maximum(thr_bits + 1, _MIN_NORMAL_BITS))
    need = capf - count(aff >= nxt)
    tie = (aff >= thr) & (aff < nxt)

    def index_step(i, j):
        cand = j | jnp.left_shift(jnp.int32(1), n_tok_bits - 1 - i)
        return jnp.where(count(tie & (idx < cand)) < need, cand, j)

    thr_ref[...] = thr
    nxt_ref[...] = nxt
    jb_ref[...] = lax.fori_loop(0, n_tok_bits, index_step, jnp.zeros((N_EXPERTS, 1), jnp.int32))


def _route(aff_t, cap):
    n_tok = aff_t.shape[1]
    val = jax.ShapeDtypeStruct((N_EXPERTS, 1), F32)
    return pl.pallas_call(
        functools.partial(_route_kernel, cap, max(1, (n_tok - 1).bit_length())),
        out_shape=[val, val, jax.ShapeDtypeStruct((N_EXPERTS, 1), jnp.int32)],
        compiler_params=pltpu.CompilerParams(vmem_limit_bytes=VMEM_LIMIT_BYTES),
        name="route",
    )(aff_t)


def _selected(aff, tok, thr, nxt, jb):
    return (aff >= nxt) | ((aff >= thr) & (tok <= jb))


def _offsets_kernel(to, aff_ref, thr_ref, nxt_ref, jb_ref, offs_ref, carry_ref):
    i = pl.program_id(0)

    @pl.when(i == 0)
    def _():
        carry_ref[...] = jnp.zeros_like(carry_ref)

    aff = aff_ref[...]
    tok = i * to + lax.broadcasted_iota(jnp.int32, aff.shape, 0)
    sel = _selected(aff, tok, thr_ref[...], nxt_ref[...], jb_ref[...]).astype(BF16)
    ns = to // SUB
    grp = (lax.broadcasted_iota(jnp.int32, (ns, to), 1) // SUB == lax.broadcasted_iota(jnp.int32, (ns, to), 0))
    cnt = _dot(grp.astype(BF16), sel)
    before = (lax.broadcasted_iota(jnp.int32, (ns, ns), 1) < lax.broadcasted_iota(jnp.int32, (ns, ns), 0))
    offs_ref[...] = (carry_ref[...] + _dot(before.astype(BF16), cnt.astype(BF16))).astype(jnp.int32)
    carry_ref[...] += jnp.sum(cnt, axis=0, keepdims=True)


def _offsets(aff, thr_row, nxt_row, jb_row, to):
    m = aff.shape[0]
    const = lambda i: (0, 0)
    row = lambda: pl.BlockSpec((1, N_EXPERTS), const)
    return pl.pallas_call(
        functools.partial(_offsets_kernel, to),
        grid=(m // to,),
        in_specs=[pl.BlockSpec((to, N_EXPERTS), lambda i: (i, 0)), row(), row(), row()],
        out_specs=pl.BlockSpec((to // SUB, N_EXPERTS), lambda i: (i, 0)),
        out_shape=jax.ShapeDtypeStruct((m // SUB, N_EXPERTS), jnp.int32),
        scratch_shapes=[pltpu.VMEM((1, N_EXPERTS), F32)],
        compiler_params=_params("arbitrary"),
        name="offsets",
    )(aff, thr_row, nxt_row, jb_row)


def _block_selection(blk, aff_t_ref, thr_ref, nxt_ref, jb_ref):
    aff = aff_t_ref[...]
    tok = blk * RB + lax.broadcasted_iota(jnp.int32, aff.shape, 1)
    return _selected(aff, tok, thr_ref[...], nxt_ref[...], jb_ref[...])


def _floor_tile(x):
    return pl.multiple_of((x // ROW_TILE) * ROW_TILE, ROW_TILE)


def _slot_onehot(sel, shift, width):
    self = sel.astype(F32)
    before = (lax.broadcasted_iota(jnp.int32, (RB, RB), 0) < lax.broadcasted_iota(jnp.int32, (RB, RB), 1))
    rank = _dot(self.astype(BF16), before.astype(BF16)).astype(jnp.int32)
    slot = lax.broadcasted_iota(jnp.int32, (width, RB), 0)
    rows = [jnp.where(rank[e:e + 1, :] + shift[e] == slot, self[e:e + 1, :], 0.0) for e in range(N_EXPERTS)]
    return jnp.concatenate(rows, axis=0).astype(BF16)


def _block_counts(offs_ref, blk):
    per = RB // SUB
    start = [offs_ref[blk * per * N_EXPERTS + e] for e in range(N_EXPERTS)]
    end = [offs_ref[(blk + 1) * per * N_EXPERTS + e] for e in range(N_EXPERTS)]
    most = functools.reduce(jnp.maximum, [b - a for a, b in zip(start, end)])
    span = functools.reduce(jnp.maximum, [b - _floor_tile(a) for a, b in zip(start, end)])
    return most <= SUB, span <= NARROW


def _sub_mask(sub):
    return lax.broadcasted_iota(jnp.int32, (N_EXPERTS, RB), 1) // SUB == sub


_STEP_BLOCKS = 2


def _gather_kernel(cap, nb, offs_ref, aff_t_ref, aff_ref, thr_ref, nxt_ref, jb_ref, h2_ref, xe_ref,
                   stage_ref, carry_ref, sem_ref):
    for j in range(_STEP_BLOCKS):
        rows = pl.ds(j * RB, RB)
        _gather_block(cap, nb, pl.program_id(0) * _STEP_BLOCKS + j, j % 2, offs_ref, aff_t_ref.at[:, rows],
                      aff_ref.at[rows], thr_ref, nxt_ref, jb_ref, h2_ref.at[rows], xe_ref,
                      stage_ref, carry_ref, sem_ref)


def _gather_block(cap, nb, blk, par, offs_ref, aff_t_ref, aff_ref, thr_ref, nxt_ref, jb_ref, h2_ref, xe_ref,
                  stage_ref, carry_ref, sem_ref):
    per = RB // SUB

    def copy(parity, e, dst_row, width):
        return pltpu.make_async_copy(stage_ref.at[parity, pl.ds(e * width, width)],
                                     xe_ref.at[e, pl.ds(dst_row, width)], sem_ref.at[parity, e])

    def wait_all(parity, width):
        for e in range(N_EXPERTS):
            copy(parity, e, 0, width).wait()

    @pl.when(blk == 0)
    def _():
        carry_ref[...] = jnp.zeros_like(carry_ref)
        stage_ref[1] = jnp.zeros(stage_ref.shape[1:], BF16)
        for e in range(N_EXPERTS):
            copy(1, e, cap, WIN).start()
        wait_all(1, WIN)

    sel = _block_selection(blk, aff_t_ref, thr_ref, nxt_ref, jb_ref)
    aff = aff_ref[...]
    hi = aff.astype(BF16)
    r1 = aff - hi.astype(F32)
    mid = r1.astype(BF16)
    lo = (r1 - mid.astype(F32)).astype(BF16)
    er = lax.broadcasted_iota(jnp.int32, (N_EXPERTS, 128), 0)
    ec = lax.broadcasted_iota(jnp.int32, (N_EXPERTS, 128), 1)
    g3 = (_dot(hi, (ec == 3 * er).astype(BF16)) + _dot(mid, (ec == 3 * er + 1).astype(BF16))
          + _dot(lo, (ec == 3 * er + 2).astype(BF16)))
    src = jnp.concatenate([h2_ref[...], g3.astype(BF16)], axis=1)

    fits, narrow = _block_counts(offs_ref, blk)
    wide = fits & jnp.logical_not(narrow)
    _, prev_narrow = _block_counts(offs_ref, jnp.maximum(blk - 1, 0))

    def emit(mask, seg_lo, seg_hi, first, width):
        lo_rows = [offs_ref[seg_lo * N_EXPERTS + e] for e in range(N_EXPERTS)]
        hi_rows = [offs_ref[seg_hi * N_EXPERTS + e] for e in range(N_EXPERTS)]
        base = [_floor_tile(r) for r in lo_rows]
        onehot = _slot_onehot(mask, [r - b for r, b in zip(lo_rows, base)], width)
        stage_ref[par, pl.ds(0, N_EXPERTS * width)] = _dot(onehot, src).astype(BF16)
        for e in range(N_EXPERTS):
            head = pl.ds(e * width, ROW_TILE)
            stage_ref[par, head] = stage_ref[par, head] + carry_ref[e]
            nxt = _floor_tile(hi_rows[e]) - base[e]
            start = jnp.minimum(nxt, width - ROW_TILE)
            tail = stage_ref[par, pl.ds(pl.multiple_of(e * width + start, ROW_TILE), ROW_TILE)]
            carry_ref[e] = jnp.where(nxt < width, tail, jnp.zeros_like(tail))
        if first:
            @pl.when((blk > 0) & prev_narrow)
            def _():
                wait_all(1 - par, NARROW)

            @pl.when((blk > 0) & jnp.logical_not(prev_narrow))
            def _():
                wait_all(1 - par, WIN)
        for e in range(N_EXPERTS):
            copy(par, e, base[e], width).start()

    @pl.when(narrow)
    def _():
        emit(sel, blk * per, (blk + 1) * per, True, NARROW)

    @pl.when(wide)
    def _():
        emit(sel, blk * per, (blk + 1) * per, True, WIN)

    @pl.when(jnp.logical_not(fits))
    def _():
        for sub in range(per):
            if sub > 0:
                wait_all(par, WIN)
            emit(sel & _sub_mask(sub), blk * per + sub, blk * per + sub + 1, sub == 0, WIN)

    @pl.when((blk == nb - 1) & narrow)
    def _():
        wait_all(par, NARROW)

    @pl.when((blk == nb - 1) & jnp.logical_not(narrow))
    def _():
        wait_all(par, WIN)


def _gather(offs, aff_t, aff, thr, nxt, jb, h2, cap):
    m = h2.shape[0]
    nb = m // RB
    step = _STEP_BLOCKS * RB
    col = lambda i, o: (0, 0)
    return pl.pallas_call(
        functools.partial(_gather_kernel, cap, nb),
        grid_spec=pltpu.PrefetchScalarGridSpec(
            num_scalar_prefetch=1,
            grid=(m // step,),
            in_specs=[
                pl.BlockSpec((N_EXPERTS, step), lambda i, o: (0, i)),
                pl.BlockSpec((step, N_EXPERTS), lambda i, o: (i, 0)),
                pl.BlockSpec((N_EXPERTS, 1), col), pl.BlockSpec((N_EXPERTS, 1), col),
                pl.BlockSpec((N_EXPERTS, 1), col),
                pl.BlockSpec((step, D_MODEL), lambda i, o: (i, 0)),
            ],
            out_specs=pl.BlockSpec(memory_space=pl.ANY),
            scratch_shapes=[pltpu.VMEM((2, N_EXPERTS * WIN, _XW), BF16),
                            pltpu.VMEM((N_EXPERTS, ROW_TILE, _XW), BF16),
                            pltpu.SemaphoreType.DMA((2, N_EXPERTS))],
        ),
        out_shape=jax.ShapeDtypeStruct((N_EXPERTS, cap + WIN, _XW), BF16),
        compiler_params=_params("arbitrary"),
        name="gather",
    )(offs, aff_t, aff, thr, nxt, jb, h2)


def _ffn_kernel(x_ref, wg_ref, wu_ref, wd_ref, o_ref, hid_ref):
    e = pl.program_id(0)
    x = x_ref[0, :, 0:D_MODEL]
    gcols = x_ref[0, :, D_MODEL:_XW].astype(F32)
    lane = lax.broadcasted_iota(jnp.int32, gcols.shape, 1)
    gate = jnp.sum(jnp.where((lane >= 3 * e) & (lane < 3 * e + 3), gcols, 0.0), axis=-1, keepdims=True)
    for f0 in range(0, D_EXPERT, 1024):
        g = _dot(x, wg_ref[0, :, f0:f0 + 1024])
        up = _dot(x, wu_ref[0, :, f0:f0 + 1024])
        hid_ref[:, f0:f0 + 1024] = (g * jax.nn.sigmoid(g) * up).astype(BF16)
    o_ref[0] = (_dot(hid_ref[...], wd_ref[0]) * gate).astype(BF16)


def _ffn(xe, cap, wg, wu, wd, tc):
    e = xe.shape[0]
    return pl.pallas_call(
        _ffn_kernel,
        grid=(e, cap // tc),
        in_specs=[
            pl.BlockSpec((1, tc, _XW), lambda e, j: (e, j, 0)),
            pl.BlockSpec((1, D_MODEL, D_EXPERT), lambda e, j: (e, 0, 0)),
            pl.BlockSpec((1, D_MODEL, D_EXPERT), lambda e, j: (e, 0, 0)),
            pl.BlockSpec((1, D_EXPERT, D_MODEL), lambda e, j: (e, 0, 0)),
        ],
        out_specs=pl.BlockSpec((1, tc, D_MODEL), lambda e, j: (e, j, 0)),
        out_shape=jax.ShapeDtypeStruct((e, cap, D_MODEL), BF16),
        scratch_shapes=[pltpu.VMEM((tc, D_EXPERT), BF16)],
        compiler_params=_params("parallel", "parallel"),
        name="experts",
    )(xe, wg, wu, wd)


def _combine_kernel(cap, nb, offs_ref, aff_t_ref, thr_ref, nxt_ref, jb_ref, x1_ref, g_ref, ye_ref, o_ref,
                    win_ref, sem_ref):
    blk = pl.program_id(0)
    par = blk % 2
    per = RB // SUB

    def window(row, width):
        first = pl.multiple_of(jnp.minimum(_floor_tile(row), cap - width), ROW_TILE)
        return first, row - first

    def copy(parity, e, first, width):
        return pltpu.make_async_copy(ye_ref.at[e, pl.ds(first, width)],
                                     win_ref.at[parity, pl.ds(e * width, width)], sem_ref.at[parity, e])

    def fetch(parity, rows, width):
        for e in range(N_EXPERTS):
            copy(parity, e, window(rows[e], width)[0], width).start()

    def wait_all(parity, width):
        for e in range(N_EXPERTS):
            copy(parity, e, 0, width).wait()

    def block_rows(b, sub):
        return [offs_ref[(b * per + sub) * N_EXPERTS + e] for e in range(N_EXPERTS)]

    def fetch_block(parity, b):
        _, b_narrow = _block_counts(offs_ref, b)

        @pl.when(b_narrow)
        def _():
            fetch(parity, block_rows(b, 0), NARROW)

        @pl.when(jnp.logical_not(b_narrow))
        def _():
            fetch(parity, block_rows(b, 0), WIN)

    @pl.when(blk == 0)
    def _():
        fetch_block(par, blk)

    @pl.when(blk + 1 < nb)
    def _():
        fetch_block(1 - par, jnp.minimum(blk + 1, nb - 1))

    sel = _block_selection(blk, aff_t_ref, thr_ref, nxt_ref, jb_ref)
    fits, narrow = _block_counts(offs_ref, blk)
    start = block_rows(blk, 0)

    def contribution(mask, rows, width):
        onehot = _slot_onehot(mask, [window(r, width)[1] for r in rows], width)
        return _dot_tn(onehot, win_ref[par, pl.ds(0, N_EXPERTS * width)])

    def finish(y):
        x = x1_ref[...] + y
        o_ref[...] = x * lax.rsqrt(jnp.mean(x * x, axis=-1, keepdims=True) + EPS) * g_ref[...]

    @pl.when(narrow)
    def _():
        wait_all(par, NARROW)
        finish(contribution(sel, start, NARROW))

    @pl.when(fits & jnp.logical_not(narrow))
    def _():
        wait_all(par, WIN)
        finish(contribution(sel, start, WIN))

    @pl.when(jnp.logical_not(fits))
    def _():
        wait_all(par, WIN)
        y = contribution(sel & _sub_mask(0), start, WIN)
        for sub in range(1, per):
            rows = block_rows(blk, sub)
            fetch(par, rows, WIN)
            wait_all(par, WIN)
            y = y + contribution(sel & _sub_mask(sub), rows, WIN)
        finish(y)


def _combine(offs, aff_t, thr, nxt, jb, x1, g, ye, cap):
    m = x1.shape[0]
    nb = m // RB
    col = lambda i, o: (0, 0)
    return pl.pallas_call(
        functools.partial(_combine_kernel, cap, nb),
        grid_spec=pltpu.PrefetchScalarGridSpec(
            num_scalar_prefetch=1,
            grid=(nb,),
            in_specs=[
                pl.BlockSpec((N_EXPERTS, RB), lambda i, o: (0, i)),
                pl.BlockSpec((N_EXPERTS, 1), col), pl.BlockSpec((N_EXPERTS, 1), col),
                pl.BlockSpec((N_EXPERTS, 1), col),
                pl.BlockSpec((RB, D_MODEL), lambda i, o: (i, 0)),
                pl.BlockSpec((1, D_MODEL), col),
                pl.BlockSpec(memory_space=pl.ANY),
            ],
            out_specs=pl.BlockSpec((RB, D_MODEL), lambda i, o: (i, 0)),
            scratch_shapes=[pltpu.VMEM((2, N_EXPERTS * WIN, D_MODEL), BF16),
                            pltpu.SemaphoreType.DMA((2, N_EXPERTS))],
        ),
        out_shape=jax.ShapeDtypeStruct((m, D_MODEL), F32),
        compiler_params=_params("arbitrary"),
        name="combine",
    )(offs, aff_t, thr, nxt, jb, x1, g, ye)


def _stack_w2(w2, lo):
    w = jnp.pad(w2, ((lo, _LR - GLA_GATE_RANK - lo), (0, 0)))
    hi = w.astype(BF16)
    lo_part = (w - hi.astype(F32)).astype(BF16)
    return jnp.concatenate([hi, hi, lo_part, jnp.zeros_like(hi)], axis=0)


_TILE_INPROJ = 256
_TILE_GLA = 1024
_TILE_MERGE = 1024
_TILE_OFFSETS = 2048
_TILE_EXPERTS = 1024


def _trunk(x3, wts, expert_w):
    n_seq, seq_len, _ = x3.shape
    m = n_seq * seq_len
    cap = max(1, EC_CAPACITY_FACTOR * m // N_EXPERTS)
    assert seq_len % _TILE_GLA == 0 and seq_len % _TILE_MERGE == 0 and m % _TILE_OFFSETS == 0
    assert cap >= WIN and cap % _TILE_EXPERTS == 0 and m % (_STEP_BLOCKS * RB) == 0
    x = x3.reshape(m, D_MODEL)
    riders = tuple(w.reshape(-1, w.shape[-1]) for w in expert_w if w.dtype != BF16)
    qkvog, cb, u, gates, lr3, *cast = _inproj(x, wts["norm_mix_g"], wts["w_main"], wts["w_lr3"], wts["b_merge"],
                                              tm=_TILE_INPROJ, riders=riders)
    if cast:
        expert_w = tuple(c.reshape(w.shape) for c, w in zip(cast, expert_w))
    og = _gla(qkvog, lr3, wts["w2f"], wts["b2f"], wts["w2b"], wts["b2b"], wts["gla_norm_g"],
              n_seq, seq_len, tb=_TILE_GLA)
    x1, h2, aff, aff_t = _merge(x, og, cb, u, gates, wts["conv_w"], wts["w_gla_out"], wts["w_conv_out"],
                                wts["w_out"], wts["norm_ffn_g"], wts["w_router"], seq_len, tm=_TILE_MERGE)
    thr, nxt, jb = _route(aff_t, cap)
    row = lambda a: a.reshape(1, N_EXPERTS)
    offs = _offsets(aff, row(thr), row(nxt), row(jb), to=_TILE_OFFSETS)
    offs = jnp.concatenate([offs, jnp.full((1, N_EXPERTS), cap, jnp.int32)], axis=0).reshape(-1)
    xe = _gather(offs, aff_t, aff, thr, nxt, jb, h2, cap)
    ye = _ffn(xe, cap, *expert_w, tc=_TILE_EXPERTS)
    out = _combine(offs, aff_t, thr, nxt, jb, x1, wts["norm_final_g"], ye, cap)
    return out.reshape(n_seq, seq_len, D_MODEL), expert_w


def kernel(x_prompt, x_sample, norm_mix_g, w_in, w_gk2_fwd, b_gk_fwd, w_gk2_bwd, b_gk_bwd, gla_norm_g, w_gla_out,
           conv_w, w_conv_out, b_merge, w_out, norm_ffn_g, w_router, w_exp_gate, w_exp_up, w_exp_down, norm_final_g):
    w = w_in[0]
    lr0 = _QKVOG
    w_lr = w[:, lr0:lr0 + _LR].astype(BF16)
    wts = {
        "norm_mix_g": norm_mix_g[0][None, :],
        "w_main": jnp.concatenate([w[:, :lr0], w[:, lr0 + _LR:]], axis=1).astype(BF16),
        "w_lr3": jnp.concatenate([w_lr, w_lr, w_lr, jnp.zeros_like(w_lr)], axis=1),
        "w2f": _stack_w2(w_gk2_fwd[0], 0),
        "w2b": _stack_w2(w_gk2_bwd[0], GLA_GATE_RANK),
        "b2f": b_gk_fwd[0][None, :],
        "b2b": b_gk_bwd[0][None, :],
        "gla_norm_g": gla_norm_g[0].reshape(1, GLA_V_WIDTH),
        "w_gla_out": w_gla_out[0].astype(BF16),
        "conv_w": conv_w[0],
        "w_conv_out": w_conv_out[0].astype(BF16),
        "b_merge": b_merge[0][None, :],
        "w_out": w_out[0].astype(BF16),
        "norm_ffn_g": norm_ffn_g[0][None, :],
        "w_router": w_router[0].astype(BF16),
        "norm_final_g": norm_final_g[None, :],
    }
    y_prompt, expert_w = _trunk(x_prompt, wts, (w_exp_gate[0], w_exp_up[0], w_exp_down[0]))
    y_sample, _ = _trunk(x_sample, wts, expert_w)
    return (y_prompt, y_sample)
```
